```python
import numpy as np
import jax
import jax.numpy as jnp
from jax import lax

D_MODEL = 1024
BATCH = 8
SEQ = 2048
DEPTH = 4

HEAD_DIM = 64
FOX_HEADS = 4
NSA_HEADS = 6
NSA_KV_HEADS = 2
DSA_HEADS = 6
MIX_WIDTH = (FOX_HEADS + NSA_HEADS + DSA_HEADS) * HEAD_DIM
ROPE_THETA = 10000.0
Q_BLOCK = 128
NSA_Q_BLOCK = 64
CMP_LEN = 32
CMP_STRIDE = 16
CMP_HIDDEN = 2 * HEAD_DIM
SEL_BLOCK = 64
SEL_TOPN = 16
WINDOW = 512
IDX_HEADS = 4
IDX_DIM = 64
DSA_TOPK = 256
N_EXPERTS = 64
N_GROUPS = 8
TOPK_GROUPS = 4
MOE_TOPK = 8
EXPERT_DIM = 256
SHARED_DIM = 256
ROUTED_SCALE = 2.5
MOE_BLOCK = 256
ALPHA = (2 * DEPTH) ** 0.25
BETA = (8 * DEPTH) ** -0.25
LN_EPS = 1e-5
NEG = -1e30
FORCE = 1e6

SEGMENTS = (
    ('fox_q', FOX_HEADS * HEAD_DIM), ('fox_k', FOX_HEADS * HEAD_DIM),
    ('fox_v', FOX_HEADS * HEAD_DIM), ('fox_f', FOX_HEADS),
    ('nsa_q', NSA_HEADS * HEAD_DIM),
    ('nsa_kc', NSA_KV_HEADS * HEAD_DIM), ('nsa_vc', NSA_KV_HEADS * HEAD_DIM),
    ('nsa_ks', NSA_KV_HEADS * HEAD_DIM), ('nsa_vs', NSA_KV_HEADS * HEAD_DIM),
    ('nsa_kw', NSA_KV_HEADS * HEAD_DIM), ('nsa_vw', NSA_KV_HEADS * HEAD_DIM),
    ('nsa_g', 3 * NSA_HEADS),
    ('dsa_q', DSA_HEADS * HEAD_DIM), ('dsa_k', HEAD_DIM), ('dsa_v', HEAD_DIM),
    ('idx_q', IDX_HEADS * IDX_DIM), ('idx_k', IDX_DIM), ('idx_w', IDX_HEADS),
)
N_IN = sum(width for _, width in SEGMENTS)

kernel_name = 'hybrid_fox_nsa_dsa_moe_deepnorm'


def _split_columns(h):
    parts, off = {}, 0
    for name, width in SEGMENTS:
        parts[name] = h[..., off:off + width]
        off += width
    return parts


def _rope(x, pos):
    d = x.shape[-1]
    half = d // 2
    inv_freq = ROPE_THETA ** (-jnp.arange(half, dtype=jnp.float32) * (2.0 / d))
    ang = pos.astype(jnp.float32)[:, None] * inv_freq[None, :]
    cos = jnp.cos(ang)[:, None, :]
    sin = jnp.sin(ang)[:, None, :]
    xf = x.astype(jnp.float32)
    x1, x2 = xf[..., :half], xf[..., half:]
    return jnp.concatenate([x1 * cos - x2 * sin, x2 * cos + x1 * sin], axis=-1).astype(x.dtype)


def _layer_norm(x, g, b):
    xf = x.astype(jnp.float32)
    mu = jnp.mean(xf, -1, keepdims=True)
    var = jnp.mean(jnp.square(xf - mu), -1, keepdims=True)
    return ((xf - mu) * lax.rsqrt(var + LN_EPS) * g + b).astype(x.dtype)


def _block_stack(y, batch, seq):
    y = jnp.moveaxis(y, 0, 1)
    return y.reshape((batch, seq) + y.shape[3:])


def _fox_attention(q, k, v, f_logit, f_bias):
    B, T, H, D = q.shape
    scale = D ** -0.5
    log_f = jax.nn.log_sigmoid((f_logit + f_bias).astype(jnp.float32))
    c = jnp.cumsum(log_f, axis=1).transpose(0, 2, 1)
    pos = jnp.arange(T)

    def sweep(i):
        q0 = i * Q_BLOCK
        tq = q0 + jnp.arange(Q_BLOCK)
        qb = lax.dynamic_slice_in_dim(q, q0, Q_BLOCK, axis=1)
        cb = lax.dynamic_slice_in_dim(c, q0, Q_BLOCK, axis=2)
        s = jnp.einsum('bqhd,bkhd->bhqk', qb, k).astype(jnp.float32) * scale
        s = s + (cb[..., :, None] - c[..., None, :])
        s = jnp.where(pos[None, :] <= tq[:, None], s, NEG)
        p = jax.nn.softmax(s, axis=-1).astype(v.dtype)
        return jnp.einsum('bhqk,bkhd->bqhd', p, v)

    return _block_stack(lax.map(sweep, jnp.arange(T // Q_BLOCK)), B, T)


def _compress(tok, pe, w1, w2):
    B, T, G, D = tok.shape
    n_c = (T - CMP_LEN) // CMP_STRIDE + 1
    idx = np.arange(n_c)[:, None] * CMP_STRIDE + np.arange(CMP_LEN)[None, :]
    blk = tok[:, idx] + pe[:, None, :]
    blk = jnp.swapaxes(blk, 2, 3).reshape(B, n_c, G, CMP_LEN * D)
    return jax.nn.gelu(blk @ w1) @ w2


def _nsa_attention(q, kc, vc, ks, vs, kw, vw, gate_logits, cpk, c1k, c2k, cpv, c1v, c2v):
    B, T, H, D = q.shape
    G = kc.shape[2]
    R = H // G
    scale = D ** -0.5
    pos = jnp.arange(T)
    qg = q.reshape(B, T, G, R, D)

    k_cmp = _compress(kc, cpk, c1k, c2k)
    v_cmp = _compress(vc, cpv, c1v, c2v)
    n_c = k_cmp.shape[1]
    c_start = np.arange(n_c) * CMP_STRIDE
    vis_c = (c_start + CMP_LEN - 1)[None, :] <= pos[:, None]
    s_c = jnp.einsum('btgrd,bcgd->bgrtc', qg, k_cmp).astype(jnp.float32) * scale
    p_c = jax.nn.softmax(jnp.where(vis_c, s_c, NEG), axis=-1) * jnp.any(vis_c, -1)[:, None]
    o_cmp = jnp.einsum('bgrtc,bcgd->btgrd', p_c.astype(v_cmp.dtype), v_cmp)

    n_s = T // SEL_BLOCK
    b_start = np.arange(n_s) * SEL_BLOCK
    overlap = ((c_start[:, None] < b_start[None, :] + SEL_BLOCK)
               & (c_start[:, None] + CMP_LEN > b_start[None, :])).astype(np.float32)
    imp = jnp.einsum('bgrtc,cs->bgts', p_c, jnp.asarray(overlap))
    blk_t = (pos // SEL_BLOCK)[:, None]
    jj = jnp.arange(n_s)[None, :]
    forced = (jj == 0) | (jj == blk_t) | (jj == blk_t - 1)
    imp = jnp.where(jj <= blk_t, jnp.where(forced, FORCE, imp), -1.0)
    n_top = min(SEL_TOPN, n_s)
    _, sel = lax.top_k(imp, n_top)

    ks_blk = ks.reshape(B, n_s, SEL_BLOCK, G, D).transpose(0, 3, 1, 2, 4)
    vs_blk = vs.reshape(B, n_s, SEL_BLOCK, G, D).transpose(0, 3, 1, 2, 4)
    kw_pad = jnp.pad(kw, ((0, 0), (WINDOW, 0), (0, 0), (0, 0)))
    vw_pad = jnp.pad(vw, ((0, 0), (WINDOW, 0), (0, 0), (0, 0)))
    gather = jax.vmap(jax.vmap(lambda blocks, ids: blocks[ids]))
    n_keys = n_top * SEL_BLOCK

    def sweep(i):
        q0 = i * NSA_Q_BLOCK
        tq = q0 + jnp.arange(NSA_Q_BLOCK)
        qb = lax.dynamic_slice_in_dim(qg, q0, NSA_Q_BLOCK, axis=1)
        selb = lax.dynamic_slice_in_dim(sel, q0, NSA_Q_BLOCK, axis=2)
        kg = gather(ks_blk, selb).reshape(B, G, NSA_Q_BLOCK, n_keys, D)
        vg = gather(vs_blk, selb).reshape(B, G, NSA_Q_BLOCK, n_keys, D)
        tk = (selb[..., None] * SEL_BLOCK + jnp.arange(SEL_BLOCK)).reshape(B, G, NSA_Q_BLOCK, n_keys)
        s = jnp.einsum('bqgrd,bgqkd->bgrqk', qb, kg).astype(jnp.float32) * scale
        s = jnp.where((tk <= tq[:, None])[:, :, None], s, NEG)
        o_slc = jnp.einsum('bgrqk,bgqkd->bqgrd', jax.nn.softmax(s, axis=-1).astype(vg.dtype), vg)
        kwb = lax.dynamic_slice_in_dim(kw_pad, q0, WINDOW + NSA_Q_BLOCK, axis=1)
        vwb = lax.dynamic_slice_in_dim(vw_pad, q0, WINDOW + NSA_Q_BLOCK, axis=1)
        tw = q0 - WINDOW + jnp.arange(WINDOW + NSA_Q_BLOCK)
        dist = tq[:, None] - tw[None, :]
        band = (dist >= 0) & (dist < WINDOW) & (tw >= 0)[None, :]
        s_w = jnp.einsum('bqgrd,bkgd->bgrqk', qb, kwb).astype(jnp.float32) * scale
        p_w = jax.nn.softmax(jnp.where(band, s_w, NEG), axis=-1).astype(vwb.dtype)
        o_win = jnp.einsum('bgrqk,bkgd->bqgrd', p_w, vwb)
        return o_slc, o_win

    o_slc, o_win = lax.map(sweep, jnp.arange(T // NSA_Q_BLOCK))
    o_slc = _block_stack(o_slc, B, T)
    o_win = _block_stack(o_win, B, T)
    g = jax.nn.sigmoid(gate_logits).reshape(B, T, 3, G, R, 1)
    return g[:, :, 0] * o_cmp + g[:, :, 1] * o_slc + g[:, :, 2] * o_win


def _dsa_attention(q, k, v, iq, ik, iw):
    B, T, H, D = q.shape
    scale = D ** -0.5
    topk = min(DSA_TOPK, T // 4)
    w_idx = iw.astype(jnp.float32) * (IDX_HEADS ** -0.5) * (IDX_DIM ** -0.5)
    pos = jnp.arange(T)
    gather = jax.vmap(lambda rows, ids: rows[ids])

    def sweep(i):
        q0 = i * Q_BLOCK
        tq = q0 + jnp.arange(Q_BLOCK)
        iqb = lax.dynamic_slice_in_dim(iq, q0, Q_BLOCK, axis=1)
        iwb = lax.dynamic_slice_in_dim(w_idx, q0, Q_BLOCK, axis=1)
        dots = jnp.einsum('bqhe,bke->bqhk', iqb, ik).astype(jnp.float32)
        score = jnp.einsum('bqh,bqhk->bqk', iwb, jax.nn.relu(dots))
        score = jnp.where(pos[None, :] <= tq[:, None], score, NEG)
        _, idx = lax.top_k(score, topk)
        kg = gather(k, idx)
        vg = gather(v, idx)
        qb = lax.dynamic_slice_in_dim(q, q0, Q_BLOCK, axis=1)
        s = jnp.einsum('bqhd,bqkd->bhqk', qb, kg).astype(jnp.float32) * scale
        s = jnp.where((idx <= tq[None, :, None])[:, None], s, NEG)
        p = jax.nn.softmax(s, axis=-1).astype(vg.dtype)
        return jnp.einsum('bhqk,bqkd->bqhd', p, vg)

    return _block_stack(lax.map(sweep, jnp.arange(T // Q_BLOCK)), B, T)


def _heads(a, n_heads):
    return a.reshape(a.shape[0], a.shape[1], n_heads, -1)


def _token_mixer(x, w_in, fox_fb, cpk, c1k, c2k, cpv, c1v, c2v, w_out):
    B, T, _ = x.shape
    pos = jnp.arange(T)
    p = _split_columns(x @ w_in)
    G = NSA_KV_HEADS
    o_fox = _fox_attention(_heads(p['fox_q'], FOX_HEADS), _heads(p['fox_k'], FOX_HEADS),
                           _heads(p['fox_v'], FOX_HEADS), p['fox_f'], fox_fb)
    o_nsa = _nsa_attention(
        _rope(_heads(p['nsa_q'], NSA_HEADS), pos),
        _rope(_heads(p['nsa_kc'], G), pos), _heads(p['nsa_vc'], G),
        _rope(_heads(p['nsa_ks'], G), pos), _heads(p['nsa_vs'], G),
        _rope(_heads(p['nsa_kw'], G), pos), _heads(p['nsa_vw'], G),
        p['nsa_g'], cpk, c1k, c2k, cpv, c1v, c2v)
    o_dsa = _dsa_attention(
        _rope(_heads(p['dsa_q'], DSA_HEADS), pos),
        _rope(_heads(p['dsa_k'], 1), pos)[:, :, 0], p['dsa_v'],
        _rope(_heads(p['idx_q'], IDX_HEADS), pos),
        _rope(_heads(p['idx_k'], 1), pos)[:, :, 0], p['idx_w'])
    o = jnp.concatenate([o_fox.reshape(B, T, -1), o_nsa.reshape(B, T, -1),
                         o_dsa.reshape(B, T, -1)], axis=-1)
    return o @ w_out


def _moe(x, w_router, b_router, w_gate, w_up, w_down, ws_gate, ws_up, ws_down):
    B, T, Dm = x.shape
    N = B * T
    E = N_EXPERTS
    xt = x.reshape(N, Dm)
    s = jax.nn.sigmoid((xt @ w_router).astype(jnp.float32))
    sb = s + b_router.astype(jnp.float32)
    grp_score = lax.top_k(sb.reshape(N, N_GROUPS, E // N_GROUPS), 2)[0].sum(-1)
    _, gi = lax.top_k(grp_score, TOPK_GROUPS)
    gmask = jax.nn.one_hot(gi, N_GROUPS, dtype=jnp.float32).sum(1) > 0
    emask = jnp.repeat(gmask, E // N_GROUPS, axis=1)
    _, ei = lax.top_k(jnp.where(emask, sb, NEG), MOE_TOPK)
    gw = jnp.take_along_axis(s, ei, axis=1)
    gw = gw / jnp.sum(gw, -1, keepdims=True) * ROUTED_SCALE

    A = N * MOE_TOPK
    e_flat = ei.reshape(A)
    tok_flat = jnp.repeat(jnp.arange(N), MOE_TOPK)
    order = jnp.argsort(e_flat)
    e_s, tok_s, w_s = e_flat[order], tok_flat[order], gw.reshape(A)[order]
    counts = jnp.bincount(e_flat, length=E)
    start = jnp.cumsum(counts) - counts
    padded = (counts + MOE_BLOCK - 1) // MOE_BLOCK * MOE_BLOCK
    pstart = jnp.cumsum(padded) - padded
    dest = pstart[e_s] + (jnp.arange(A) - start[e_s])
    rows = A + E * MOE_BLOCK
    n_chunks = rows // MOE_BLOCK
    buf = jnp.zeros((rows, Dm), x.dtype).at[dest].set(xt[tok_s])
    chunk_e = jnp.minimum(jnp.searchsorted(pstart + padded, jnp.arange(n_chunks) * MOE_BLOCK, side='right'), E - 1)

    def expert_rows(args):
        xb, e = args
        h = jax.nn.silu(xb @ w_gate[e]) * (xb @ w_up[e])
        return h @ w_down[e]

    ybuf = lax.map(expert_rows, (buf.reshape(n_chunks, MOE_BLOCK, Dm), chunk_e)).reshape(rows, Dm)
    routed = jnp.zeros((N, Dm), x.dtype).at[tok_s].add(ybuf[dest] * w_s[:, None].astype(x.dtype))
    shared = (jax.nn.silu(xt @ ws_gate) * (xt @ ws_up)) @ ws_down
    return (routed + shared).reshape(B, T, Dm)


def setup_inputs(seed: int = 0) -> dict:
    key = jax.random.key(seed)
    ks = jax.random.split(key, 24)

    def nrm(k, shape, scale):
        return jax.random.normal(k, shape, jnp.float32) * scale

    L = DEPTH
    return {
        'x': nrm(ks[0], (BATCH, SEQ, D_MODEL), 1.0),
        'w_in': nrm(ks[1], (L, D_MODEL, N_IN), D_MODEL ** -0.5),
        'fox_forget_bias': 3.0 + nrm(ks[2], (L, FOX_HEADS), 0.5),
        'cmp_pos_k': nrm(ks[3], (L, CMP_LEN, HEAD_DIM), 0.1),
        'cmp_w1_k': nrm(ks[4], (L, CMP_LEN * HEAD_DIM, CMP_HIDDEN), (CMP_LEN * HEAD_DIM) ** -0.5),
        'cmp_w2_k': nrm(ks[5], (L, CMP_HIDDEN, HEAD_DIM), CMP_HIDDEN ** -0.5),
        'cmp_pos_v': nrm(ks[6], (L, CMP_LEN, HEAD_DIM), 0.1),
        'cmp_w1_v': nrm(ks[7], (L, CMP_LEN * HEAD_DIM, CMP_HIDDEN), (CMP_LEN * HEAD_DIM) ** -0.5),
        'cmp_w2_v': nrm(ks[8], (L, CMP_HIDDEN, HEAD_DIM), CMP_HIDDEN ** -0.5),
        'w_out': nrm(ks[9], (L, MIX_WIDTH, D_MODEL), MIX_WIDTH ** -0.5 * BETA),
        'ln1_g': 1.0 + nrm(ks[10], (L, D_MODEL), 0.01),
        'ln1_b': nrm(ks[11], (L, D_MODEL), 0.01),
        'w_router': nrm(ks[12], (L, D_MODEL, N_EXPERTS), D_MODEL ** -0.5),
        'b_router': nrm(ks[13], (L, N_EXPERTS), 0.01),
        'w_gate': nrm(ks[14], (L, N_EXPERTS, D_MODEL, EXPERT_DIM), D_MODEL ** -0.5),
        'w_up': nrm(ks[15], (L, N_EXPERTS, D_MODEL, EXPERT_DIM), D_MODEL ** -0.5),
        'w_down': nrm(ks[16], (L, N_EXPERTS, EXPERT_DIM, D_MODEL), EXPERT_DIM ** -0.5 * BETA),
        'ws_gate': nrm(ks[17], (L, D_MODEL, SHARED_DIM), D_MODEL ** -0.5),
        'ws_up': nrm(ks[18], (L, D_MODEL, SHARED_DIM), D_MODEL ** -0.5),
        'ws_down': nrm(ks[19], (L, SHARED_DIM, D_MODEL), SHARED_DIM ** -0.5 * BETA),
        'ln2_g': 1.0 + nrm(ks[20], (L, D_MODEL), 0.01),
        'ln2_b': nrm(ks[21], (L, D_MODEL), 0.01),
    }


def reference(x, w_in, fox_forget_bias, cmp_pos_k, cmp_w1_k, cmp_w2_k, cmp_pos_v, cmp_w1_v,
              cmp_w2_v, w_out, ln1_g, ln1_b, w_router, b_router, w_gate, w_up, w_down,
              ws_gate, ws_up, ws_down, ln2_g, ln2_b):
    for l in range(DEPTH):
        mix = _token_mixer(x, w_in[l], fox_forget_bias[l], cmp_pos_k[l], cmp_w1_k[l], cmp_w2_k[l],
                           cmp_pos_v[l], cmp_w1_v[l], cmp_w2_v[l], w_out[l])
        x = _layer_norm(ALPHA * x + mix, ln1_g[l], ln1_b[l])
        ffn = _moe(x, w_router[l], b_router[l], w_gate[l], w_up[l], w_down[l],
                   ws_gate[l], ws_up[l], ws_down[l])
        x = _layer_norm(ALPHA * x + ffn, ln2_g[l], ln2_b[l])
    return x
```

```python
import functools

import numpy as np
import jax
import jax.numpy as jnp
from jax import lax
from jax.experimental import pallas as pl
from jax.experimental.pallas import tpu as pltpu

D_MODEL = 1024
HEAD_DIM = 64
FOX_HEADS = 4
NSA_HEADS = 6
NSA_KV_HEADS = 2
NSA_REP = NSA_HEADS // NSA_KV_HEADS
DSA_HEADS = 6
ROPE_THETA = 10000.0
CMP_LEN = 32
CMP_STRIDE = 16
CMP_HIDDEN = 2 * HEAD_DIM
SEL_BLOCK = 64
SEL_TOPN = 16
WINDOW = 512
IDX_HEADS = 4
IDX_DIM = 64
DSA_TOPK = 256
N_EXPERTS = 64
N_GROUPS = 8
GROUP_SIZE = N_EXPERTS // N_GROUPS
TOPK_GROUPS = 4
MOE_TOPK = 8
EXPERT_DIM = 256
SHARED_DIM = 256
ROUTED_SCALE = 2.5
LN_EPS = 1e-5
NEG = -1e30
FORCE = 1e6
SCALE = HEAD_DIM ** -0.5

F32 = jnp.float32
BF16 = jnp.bfloat16
I32 = jnp.int32

VMEM_LIMIT_BYTES = 52 * 1024 * 1024
LANES = 128

SEGMENTS = (
    ('fox_q', FOX_HEADS * HEAD_DIM), ('fox_k', FOX_HEADS * HEAD_DIM),
    ('fox_v', FOX_HEADS * HEAD_DIM), ('fox_f', FOX_HEADS),
    ('nsa_q', NSA_HEADS * HEAD_DIM),
    ('nsa_kc', NSA_KV_HEADS * HEAD_DIM), ('nsa_vc', NSA_KV_HEADS * HEAD_DIM),
    ('nsa_ks', NSA_KV_HEADS * HEAD_DIM), ('nsa_vs', NSA_KV_HEADS * HEAD_DIM),
    ('nsa_kw', NSA_KV_HEADS * HEAD_DIM), ('nsa_vw', NSA_KV_HEADS * HEAD_DIM),
    ('nsa_g', 3 * NSA_HEADS),
    ('dsa_q', DSA_HEADS * HEAD_DIM), ('dsa_k', HEAD_DIM), ('dsa_v', HEAD_DIM),
    ('idx_q', IDX_HEADS * IDX_DIM), ('idx_k', IDX_DIM), ('idx_w', IDX_HEADS),
)
SEG_OFF = {}
_off = 0
for _name, _width in SEGMENTS:
    SEG_OFF[_name] = (_off, _width)
    _off += _width
N_IN = _off

PLAIN_ORDER = ('fox_q', 'fox_k', 'fox_v', 'nsa_vc', 'nsa_vs', 'nsa_vw', 'dsa_v', (None, 64),
               'fox_f', 'nsa_g', 'idx_w', (None, 2 * LANES - FOX_HEADS - 3 * NSA_HEADS - IDX_HEADS))
ROPE_ORDER = ('nsa_q', 'nsa_kc', 'nsa_ks', 'nsa_kw', 'dsa_q', 'dsa_k', 'idx_k', 'idx_q')


def _layout(order):
    pos, off = {}, 0
    for item in order:
        if isinstance(item, tuple):
            off += item[1]
        else:
            pos[item] = off
            off += SEG_OFF[item][1]
    return pos, off


PLAIN_POS, PLAIN_W = _layout(PLAIN_ORDER)
ROPE_POS, ROPE_W = _layout(ROPE_ORDER)
SMALL_COL = PLAIN_POS['fox_f']
SMALL_F = 0
SMALL_G = FOX_HEADS
SMALL_W = FOX_HEADS + 3 * NSA_HEADS


def _cparams(sem):
    return pltpu.CompilerParams(dimension_semantics=sem, vmem_limit_bytes=VMEM_LIMIT_BYTES)


def _nt_dot(a, b):
    return lax.dot_general(a, b, (((1,), (1,)), ((), ())), preferred_element_type=F32)


def _dot(a, b):
    return jnp.dot(a, b, preferred_element_type=F32)


def _softmax_rows(s):
    m = jnp.max(s, axis=-1, keepdims=True)
    p = jnp.exp(s - m)
    return p, jnp.sum(p, axis=-1, keepdims=True)


def _layer_norm(z, g, b):
    mu = jnp.mean(z, axis=-1, keepdims=True)
    zc = z - mu
    var = jnp.mean(zc * zc, axis=-1, keepdims=True)
    return zc * lax.rsqrt(var + LN_EPS) * g + b


def _proj_kernel(x_ref, w_ref, o_ref, xb_ref):
    @pl.when(pl.program_id(1) == 0)
    def _():
        xb_ref[...] = x_ref[...].astype(BF16)

    o_ref[...] = _dot(xb_ref[...], w_ref[...])


def _proj_rope_kernel(x_ref, wa_ref, wb_ref, cos_ref, sin_ref, o_ref, xb_ref):
    @pl.when(pl.program_id(1) == 0)
    def _():
        xb_ref[...] = x_ref[...].astype(BF16)

    xb = xb_ref[...]
    o_ref[...] = _dot(xb, wa_ref[...]) * cos_ref[...] + _dot(xb, wb_ref[...]) * sin_ref[...]


def _project(x2d, w_plain, w_a, w_b, cos_t, sin_t, seq):
    n = x2d.shape[0]
    tm = min(1024, seq)
    tc = 256
    plain = pl.pallas_call(
        _proj_kernel,
        grid=(n // tm, PLAIN_W // tc),
        in_specs=[pl.BlockSpec((tm, D_MODEL), lambda i, j: (i, 0)),
                  pl.BlockSpec((D_MODEL, tc), lambda i, j: (0, j))],
        out_specs=pl.BlockSpec((tm, tc), lambda i, j: (i, j)),
        out_shape=jax.ShapeDtypeStruct((n, PLAIN_W), F32),
        scratch_shapes=[pltpu.VMEM((tm, D_MODEL), BF16)],
        compiler_params=_cparams(("parallel", "arbitrary")),
    )(x2d, w_plain)
    nt = seq // tm
    roped = pl.pallas_call(
        _proj_rope_kernel,
        grid=(n // tm, ROPE_W // tc),
        in_specs=[pl.BlockSpec((tm, D_MODEL), lambda i, j: (i, 0)),
                  pl.BlockSpec((D_MODEL, tc), lambda i, j: (0, j)),
                  pl.BlockSpec((D_MODEL, tc), lambda i, j: (0, j)),
                  pl.BlockSpec((tm, tc), lambda i, j: (i % nt, 0)),
                  pl.BlockSpec((tm, tc), lambda i, j: (i % nt, 0))],
        out_specs=pl.BlockSpec((tm, tc), lambda i, j: (i, j)),
        out_shape=jax.ShapeDtypeStruct((n, ROPE_W), F32),
        scratch_shapes=[pltpu.VMEM((tm, D_MODEL), BF16)],
        compiler_params=_cparams(("parallel", "arbitrary")),
    )(x2d, w_a, w_b, cos_t, sin_t)
    return plain, roped


def _fox_cum_kernel(f_ref, fb_ref, o_ref):
    x = f_ref[0] + fb_ref[...]
    c = jnp.minimum(x, 0.0) - jnp.log1p(jnp.exp(-jnp.abs(x)))
    seq = c.shape[-1]
    lane = lax.broadcasted_iota(I32, c.shape, 1)
    sh = 1
    while sh < seq:
        c = c + jnp.where(lane >= sh, pltpu.roll(c, sh, 1), 0.0)
        sh *= 2
    o_ref[0] = c


def _fox_cumsum(f_rows, f_bias):
    b, h, seq = f_rows.shape
    return pl.pallas_call(
        _fox_cum_kernel,
        grid=(b,),
        in_specs=[pl.BlockSpec((1, h, seq), lambda i: (i, 0, 0)),
                  pl.BlockSpec((h, 1), lambda i: (0, 0))],
        out_specs=pl.BlockSpec((1, h, seq), lambda i: (i, 0, 0)),
        out_shape=jax.ShapeDtypeStruct((b, h, seq), F32),
        compiler_params=_cparams(("parallel",)),
    )(f_rows, f_bias)


def _fox_kernel(q_ref, k_ref, v_ref, cc_ref, cr_ref, o_ref, *, tq, seq):
    q0 = pl.program_id(1) * tq
    qpos = q0 + lax.broadcasted_iota(I32, (tq, seq), 0)
    kpos = lax.broadcasted_iota(I32, (tq, seq), 1)
    causal = kpos <= qpos
    for h in range(FOX_HEADS):
        sl = slice(h * HEAD_DIM, (h + 1) * HEAD_DIM)
        qh = q_ref[0, :, sl].astype(BF16)
        kh = k_ref[0, :, sl].astype(BF16)
        vh = v_ref[0, :, sl].astype(BF16)
        s = _nt_dot(qh, kh) * SCALE
        s = s + (cc_ref[0, :, h:h + 1] - cr_ref[0, h:h + 1, :])
        p, l = _softmax_rows(jnp.where(causal, s, NEG))
        o_ref[0, :, sl] = (_dot(p.astype(BF16), vh) / l).astype(o_ref.dtype)


def _fox_attention(hp3, c_cols, c_rows):
    b, seq, _ = hp3.shape
    tq = min(256, seq)
    w = FOX_HEADS * HEAD_DIM
    return pl.pallas_call(
        functools.partial(_fox_kernel, tq=tq, seq=seq),
        grid=(b, seq // tq),
        in_specs=[pl.BlockSpec((1, tq, w), lambda i, j: (i, j, PLAIN_POS['fox_q'] // w)),
                  pl.BlockSpec((1, seq, w), lambda i, j: (i, 0, PLAIN_POS['fox_k'] // w)),
                  pl.BlockSpec((1, seq, w), lambda i, j: (i, 0, PLAIN_POS['fox_v'] // w)),
                  pl.BlockSpec((1, tq, FOX_HEADS), lambda i, j: (i, j, 0)),
                  pl.BlockSpec((1, FOX_HEADS, seq), lambda i, j: (i, 0, 0))],
        out_specs=pl.BlockSpec((1, tq, w), lambda i, j: (i, j, 0)),
        out_shape=jax.ShapeDtypeStruct((b, seq, w), BF16),
        compiler_params=_cparams(("parallel", "arbitrary")),
    )(hp3, hp3, hp3, c_cols, c_rows)


def _gelu_tanh(x):
    return 0.5 * x * (1.0 + jnp.tanh(np.float32(np.sqrt(2.0 / np.pi)) * (x + 0.044715 * (x * x * x))))


def _compress_kernel(rk_ref, rv_ref, pek_ref, pev_ref, w1k_ref, w2k_ref, w1v_ref, w2v_ref, ok_ref, ov_ref):
    half = CMP_STRIDE * HEAD_DIM

    def one(r_ref, pe_ref, w1_ref, w2_ref, o_ref):
        r = r_ref[0]
        n_rows = r.shape[0]
        lo = _dot((r + pe_ref[:, :half]).astype(BF16), w1_ref[:half, :])
        hi = _dot((r + pe_ref[:, half:]).astype(BF16), w1_ref[half:, :])
        hid = _gelu_tanh(lo + pltpu.roll(hi, n_rows - 1, 0))
        o_ref[0] = _dot(hid.astype(BF16), w2_ref[...])

    one(rk_ref, pek_ref, w1k_ref, w2k_ref, ok_ref)
    one(rv_ref, pev_ref, w1v_ref, w2v_ref, ov_ref)


def _nsa_compress(rk, rv, pek, pev, w1k, w2k, w1v, w2v):
    bg, rows, width = rk.shape
    tok = pl.BlockSpec((1, rows, width), lambda i: (i, 0, 0))
    full = lambda a: pl.BlockSpec(a.shape, lambda i: (0,) * a.ndim)
    out = pl.BlockSpec((1, rows, HEAD_DIM), lambda i: (i, 0, 0))
    return pl.pallas_call(
        _compress_kernel,
        grid=(bg,),
        in_specs=[tok, tok, full(pek), full(pev), full(w1k), full(w2k), full(w1v), full(w2v)],
        out_specs=[out, out],
        out_shape=[jax.ShapeDtypeStruct((bg, rows, HEAD_DIM), F32)] * 2,
        compiler_params=_cparams(("parallel",)),
    )(rk, rv, pek, pev, w1k, w2k, w1v, w2v)


def _split3_dot(a, b01):
    a1 = a.astype(BF16)
    r1 = a - a1.astype(F32)
    a2 = r1.astype(BF16)
    a3 = (r1 - a2.astype(F32)).astype(BF16)
    return _dot(a1, b01) + _dot(a2, b01) + _dot(a3, b01)


def _nsa_kernel(q_ref, kc_ref, vc_ref, ks_ref, kw_ref, vs_ref, vw_ref, sm_ref, o_ref, *, tq, seq, wlen):
    q0 = pl.program_id(1) * tq
    n_c = kc_ref.shape[2]
    n_s = seq // SEL_BLOCK
    rows = NSA_REP * tq
    tcol = q0 + lax.broadcasted_iota(I32, (tq, 1), 0)
    tcol_r = jnp.concatenate([tcol] * NSA_REP, axis=0)

    cidx = lax.broadcasted_iota(I32, (rows, n_c), 1)
    vis_r = (cidx * CMP_STRIDE + (CMP_LEN - 1)) <= tcol_r
    any_vis_r = (tcol_r >= (CMP_LEN - 1)).astype(F32)
    oc = lax.broadcasted_iota(I32, (n_c, n_s), 0) * CMP_STRIDE
    ob = lax.broadcasted_iota(I32, (n_c, n_s), 1) * SEL_BLOCK
    overlap = ((oc < ob + SEL_BLOCK) & (oc + CMP_LEN > ob)).astype(BF16)
    expand = (lax.broadcasted_iota(I32, (n_s, seq), 1) // SEL_BLOCK
              == lax.broadcasted_iota(I32, (n_s, seq), 0)).astype(BF16)
    jj = lax.broadcasted_iota(I32, (tq, n_s), 1)
    blk_t = tcol // SEL_BLOCK
    causal_r = lax.broadcasted_iota(I32, (rows, seq), 1) <= tcol_r
    w0 = pl.multiple_of(jnp.maximum(q0 + tq - wlen, 0), 8)
    dist = tcol_r - (w0 + lax.broadcasted_iota(I32, (rows, wlen), 1))
    band_r = (dist >= 0) & (dist < WINDOW)
    gates = sm_ref[0]

    for g in range(NSA_KV_HEADS):
        ksl = slice(g * HEAD_DIM, (g + 1) * HEAD_DIM)
        qs = jnp.concatenate(
            [q_ref[0, :, (g * NSA_REP + r) * HEAD_DIM:(g * NSA_REP + r + 1) * HEAD_DIM] for r in range(NSA_REP)],
            axis=0).astype(BF16)

        s_c = _nt_dot(qs, kc_ref[0, g].astype(BF16)) * SCALE
        p_c, l_c = _softmax_rows(jnp.where(vis_r, s_c, NEG))
        p_c = p_c / l_c * any_vis_r
        o_cmp = _dot(p_c.astype(BF16), vc_ref[0, g].astype(BF16))

        p_sum = p_c[0:tq]
        for r in range(1, NSA_REP):
            p_sum = p_sum + p_c[r * tq:(r + 1) * tq]
        imp = _split3_dot(p_sum, overlap)
        forced = (jj == 0) | (jj == blk_t) | (jj == blk_t - 1)
        imp = jnp.where(jj <= blk_t, jnp.where(forced, FORCE, imp), -1.0)
        rank = jnp.zeros((tq, n_s), F32)
        for j2 in range(n_s):
            col = imp[:, j2:j2 + 1]
            beats = (col > imp) | ((col == imp) & (j2 < jj))
            rank = rank + beats.astype(F32)
        sel = (rank < float(min(SEL_TOPN, n_s))).astype(F32)
        sel_r = jnp.concatenate([sel] * NSA_REP, axis=0).astype(BF16)
        slc_mask = (_dot(sel_r, expand) > 0.5) & causal_r

        s_s = _nt_dot(qs, ks_ref[0, :, ksl].astype(BF16)) * SCALE
        p_s, l_s = _softmax_rows(jnp.where(slc_mask, s_s, NEG))
        o_slc = _dot(p_s.astype(BF16), vs_ref[0, :, ksl].astype(BF16)) / l_s

        kw = kw_ref[0, pl.ds(w0, wlen), ksl].astype(BF16)
        vw = vw_ref[0, pl.ds(w0, wlen), ksl].astype(BF16)
        s_w = _nt_dot(qs, kw) * SCALE
        p_w, l_w = _softmax_rows(jnp.where(band_r, s_w, NEG))
        o_win = _dot(p_w.astype(BF16), vw) / l_w

        for r in range(NSA_REP):
            head = g * NSA_REP + r
            rs = slice(r * tq, (r + 1) * tq)

            def gate(branch, head=head):
                col = SMALL_G + branch * NSA_HEADS + head
                return 1.0 / (1.0 + jnp.exp(-gates[:, col:col + 1]))

            out = gate(0) * o_cmp[rs] + gate(1) * o_slc[rs] + gate(2) * o_win[rs]
            o_ref[0, :, head * HEAD_DIM:(head + 1) * HEAD_DIM] = out.astype(o_ref.dtype)


def _nsa_attention(hr3, hp3, k_cmp, v_cmp):
    b, seq, _ = hr3.shape
    tq = min(128, seq)
    wlen = min(WINDOW + tq, seq)
    qw = NSA_HEADS * HEAD_DIM
    kvw = NSA_KV_HEADS * HEAD_DIM
    n_c = k_cmp.shape[2]
    cmp_spec = pl.BlockSpec((1, NSA_KV_HEADS, n_c, HEAD_DIM), lambda i, j: (i, 0, 0, 0))

    def seq_spec(col):
        return pl.BlockSpec((1, seq, kvw), lambda i, j: (i, 0, col // kvw))

    return pl.pallas_call(
        functools.partial(_nsa_kernel, tq=tq, seq=seq, wlen=wlen),
        grid=(b, seq // tq),
        in_specs=[pl.BlockSpec((1, tq, qw), lambda i, j: (i, j, ROPE_POS['nsa_q'] // qw)),
                  cmp_spec, cmp_spec,
                  seq_spec(ROPE_POS['nsa_ks']), seq_spec(ROPE_POS['nsa_kw']),
                  seq_spec(PLAIN_POS['nsa_vs']), seq_spec(PLAIN_POS['nsa_vw']),
                  pl.BlockSpec((1, tq, LANES), lambda i, j: (i, j, SMALL_COL // LANES))],
        out_specs=pl.BlockSpec((1, tq, qw), lambda i, j: (i, j, 0)),
        out_shape=jax.ShapeDtypeStruct((b, seq, qw), BF16),
        compiler_params=_cparams(("parallel", "arbitrary")),
    )(hr3, k_cmp, v_cmp, hr3, hr3, hp3, hp3, hp3)


def _row_count(mask):
    return jnp.sum(mask.astype(F32), axis=-1, keepdims=True)


def _dsa_kernel(q_ref, kk_ref, v_ref, iq_ref, sm_ref, o_ref, key_ref, *, tq, seq, topk):
    q0 = pl.program_id(1) * tq
    tcol = q0 + lax.broadcasted_iota(I32, (tq, 1), 0)
    kpos = lax.broadcasted_iota(I32, (tq, seq), 1)
    causal = kpos <= tcol

    ik = kk_ref[0, :, HEAD_DIM:2 * HEAD_DIM].astype(BF16)
    w_idx = sm_ref[0, :, SMALL_W:SMALL_W + IDX_HEADS] * (IDX_HEADS ** -0.5) * (IDX_DIM ** -0.5)
    score = jnp.zeros((tq, seq), F32)
    for h in range(IDX_HEADS):
        d = _nt_dot(iq_ref[0, :, h * IDX_DIM:(h + 1) * IDX_DIM].astype(BF16), ik)
        score = score + w_idx[:, h:h + 1] * jnp.maximum(d, 0.0)
    score = jnp.where(causal, score, NEG)

    bits = lax.bitcast_convert_type(score, I32)
    key_ref[...] = jnp.where(bits < 0, bits ^ 0x7FFFFFFF, bits)
    kf = float(topk)

    def tau_step(i, tau):
        cand = tau + jnp.left_shift(jnp.int32(1), 31 - i)
        cnt = _row_count(key_ref[...] >= cand)
        return jnp.where(cnt >= kf, cand, tau)

    tau = lax.fori_loop(0, 32, tau_step, jnp.full((tq, 1), -2 ** 31, I32))
    need = kf - _row_count(key_ref[...] > tau)

    n_bits = (seq - 1).bit_length()

    def cut_step(i, cut):
        cand = cut + jnp.left_shift(jnp.int32(1), n_bits - 1 - i)
        idx = lax.broadcasted_iota(I32, (tq, seq), 1)
        cnt = _row_count((key_ref[...] == tau) & (idx < cand))
        return jnp.where(cnt < need, cand, cut)

    cut = lax.fori_loop(0, n_bits, cut_step, jnp.zeros((tq, 1), I32))
    key = key_ref[...]
    chosen = ((key > tau) | ((key == tau) & (kpos <= cut))) & causal

    k = kk_ref[0, :, 0:HEAD_DIM].astype(BF16)
    v = v_ref[0, :, 0:HEAD_DIM].astype(BF16)
    for h in range(DSA_HEADS):
        sl = slice(h * HEAD_DIM, (h + 1) * HEAD_DIM)
        s = _nt_dot(q_ref[0, :, sl].astype(BF16), k) * SCALE
        p, l = _softmax_rows(jnp.where(chosen, s, NEG))
        o_ref[0, :, sl] = (_dot(p.astype(BF16), v) / l).astype(o_ref.dtype)


def _dsa_attention(hr3, hp3):
    b, seq, _ = hr3.shape
    tq = min(128, seq)
    topk = min(DSA_TOPK, seq // 4)
    qw = DSA_HEADS * HEAD_DIM
    iqw = IDX_HEADS * IDX_DIM
    return pl.pallas_call(
        functools.partial(_dsa_kernel, tq=tq, seq=seq, topk=topk),
        grid=(b, seq // tq),
        in_specs=[pl.BlockSpec((1, tq, qw), lambda i, j: (i, j, ROPE_POS['dsa_q'] // qw)),
                  pl.BlockSpec((1, seq, LANES), lambda i, j: (i, 0, ROPE_POS['dsa_k'] // LANES)),
                  pl.BlockSpec((1, seq, LANES), lambda i, j: (i, 0, PLAIN_POS['dsa_v'] // LANES)),
                  pl.BlockSpec((1, tq, iqw), lambda i, j: (i, j, ROPE_POS['idx_q'] // iqw)),
                  pl.BlockSpec((1, tq, LANES), lambda i, j: (i, j, SMALL_COL // LANES))],
        out_specs=pl.BlockSpec((1, tq, qw), lambda i, j: (i, j, 0)),
        out_shape=jax.ShapeDtypeStruct((b, seq, qw), BF16),
        scratch_shapes=[pltpu.VMEM((tq, seq), I32)],
        compiler_params=_cparams(("parallel", "arbitrary")),
    )(hr3, hr3, hp3, hr3, hp3)


def _outproj_kernel(of_ref, on_ref, od_ref, x_ref, wf_ref, wn_ref, wd_ref, g_ref, b_ref, o_ref, *, alpha):
    mix = _dot(of_ref[...], wf_ref[...]) + _dot(on_ref[...], wn_ref[...]) + _dot(od_ref[...], wd_ref[...])
    o_ref[...] = _layer_norm(alpha * x_ref[...] + mix, g_ref[...], b_ref[...])


def _outproj_ln(o_fox, o_nsa, o_dsa, x2d, w_f, w_n, w_d, g, b, alpha):
    n = x2d.shape[0]
    tm = min(512, n)
    row = lambda a: pl.BlockSpec((tm, a.shape[1]), lambda i: (i, 0))
    full = lambda a: pl.BlockSpec(a.shape, lambda i: (0, 0))
    return pl.pallas_call(
        functools.partial(_outproj_kernel, alpha=alpha),
        grid=(n // tm,),
        in_specs=[row(o_fox), row(o_nsa), row(o_dsa), row(x2d), full(w_f), full(w_n), full(w_d), full(g), full(b)],
        out_specs=pl.BlockSpec((tm, D_MODEL), lambda i: (i, 0)),
        out_shape=jax.ShapeDtypeStruct((n, D_MODEL), F32),
        compiler_params=_cparams(("parallel",)),
    )(o_fox, o_nsa, o_dsa, x2d, w_f, w_n, w_d, g, b)


def _router_kernel(x_ref, wr_ref, br_ref, g_ref):
    tn = x_ref.shape[0]
    logits = _nt_dot(wr_ref[...], x_ref[...].astype(BF16))
    s = 1.0 / (1.0 + jnp.exp(-logits))
    sb = s + br_ref[...]
    low = jnp.float32(-3e38)

    grp = []
    for gi in range(N_GROUPS):
        blk = sb[gi * GROUP_SIZE:(gi + 1) * GROUP_SIZE]
        m1 = jnp.max(blk, axis=0, keepdims=True)
        is_max = blk == m1
        n_max = jnp.sum(is_max.astype(F32), axis=0, keepdims=True)
        m2 = jnp.max(jnp.where(is_max, low, blk), axis=0, keepdims=True)
        grp.append(m1 + jnp.where(n_max >= 2.0, m1, m2))
    masked = []
    for gi in range(N_GROUPS):
        rank = jnp.zeros((1, tn), F32)
        for g2 in range(N_GROUPS):
            if g2 == gi:
                continue
            beats = (grp[g2] > grp[gi]) | ((grp[g2] == grp[gi]) if g2 < gi else False)
            rank = rank + beats.astype(F32)
        keep = rank < float(TOPK_GROUPS)
        masked.append(jnp.where(keep, sb[gi * GROUP_SIZE:(gi + 1) * GROUP_SIZE], NEG))
    masked = jnp.concatenate(masked, axis=0)

    eidx = lax.broadcasted_iota(I32, (N_EXPERTS, tn), 0)
    rank = jnp.zeros((N_EXPERTS, tn), F32)
    for e2 in range(N_EXPERTS):
        row = masked[e2:e2 + 1]
        beats = (row > masked) | ((row == masked) & (e2 < eidx))
        rank = rank + beats.astype(F32)
    gw = jnp.where(rank < float(MOE_TOPK), s, 0.0)
    g_ref[...] = gw / jnp.sum(gw, axis=0, keepdims=True) * ROUTED_SCALE


def _router(x2d, wr_t, br_col):
    n = x2d.shape[0]
    tn = min(512, n)
    return pl.pallas_call(
        _router_kernel,
        grid=(n // tn,),
        in_specs=[pl.BlockSpec((tn, D_MODEL), lambda i: (i, 0)),
                  pl.BlockSpec(wr_t.shape, lambda i: (0, 0)),
                  pl.BlockSpec(br_col.shape, lambda i: (0, 0))],
        out_specs=pl.BlockSpec((N_EXPERTS, tn), lambda i: (0, i)),
        out_shape=jax.ShapeDtypeStruct((N_EXPERTS, n), F32),
        compiler_params=_cparams(("parallel",)),
    )(x2d, wr_t, br_col)


def _silu(x):
    return x / (1.0 + jnp.exp(-x))


def _moe_kernel(x_ref, gt_ref, wg_ref, wu_ref, wd_ref, sg_ref, su_ref, sd_ref, lg_ref, lb_ref, o_ref,
                xb_ref, acc_ref, *, alpha):
    e = pl.program_id(1)

    @pl.when(e == 0)
    def _():
        xb = x_ref[...].astype(BF16)
        xb_ref[...] = xb
        h = _silu(_dot(xb, sg_ref[...])) * _dot(xb, su_ref[...])
        acc_ref[...] = _dot(h.astype(BF16), sd_ref[...])

    xb = xb_ref[...]
    h = (_silu(_dot(xb, wg_ref[0])) * _dot(xb, wu_ref[0])).astype(BF16)
    gates = gt_ref[...]
    lane = lax.broadcasted_iota(I32, gates.shape, 1)
    gcol = jnp.sum(jnp.where(lane == e, gates, 0.0), axis=1, keepdims=True)
    chunk = 256
    for c in range(D_MODEL // chunk):
        cs = slice(c * chunk, (c + 1) * chunk)
        acc_ref[:, cs] += gcol * _dot(h, wd_ref[0, :, cs])

    @pl.when(e == pl.num_programs(1) - 1)
    def _():
        o_ref[...] = _layer_norm(alpha * x_ref[...] + acc_ref[...], lg_ref[...], lb_ref[...])


def _moe_ln(x2d, gates, w_gate, w_up, w_down, ws_gate, ws_up, ws_down, g, b, alpha):
    n = x2d.shape[0]
    tn = min(1024, n)
    full = lambda a: pl.BlockSpec(a.shape, lambda i, e: (0,) * a.ndim)
    return pl.pallas_call(
        functools.partial(_moe_kernel, alpha=alpha),
        grid=(n // tn, N_EXPERTS),
        in_specs=[pl.BlockSpec((tn, D_MODEL), lambda i, e: (i, 0)),
                  pl.BlockSpec((tn, N_EXPERTS), lambda i, e: (i, 0)),
                  pl.BlockSpec((1, D_MODEL, EXPERT_DIM), lambda i, e: (e, 0, 0)),
                  pl.BlockSpec((1, D_MODEL, EXPERT_DIM), lambda i, e: (e, 0, 0)),
                  pl.BlockSpec((1, EXPERT_DIM, D_MODEL), lambda i, e: (e, 0, 0)),
                  full(ws_gate), full(ws_up), full(ws_down), full(g), full(b)],
        out_specs=pl.BlockSpec((tn, D_MODEL), lambda i, e: (i, 0)),
        out_shape=jax.ShapeDtypeStruct((n, D_MODEL), F32),
        scratch_shapes=[pltpu.VMEM((tn, D_MODEL), BF16), pltpu.VMEM((tn, D_MODEL), F32)],
        compiler_params=_cparams(("parallel", "arbitrary")),
    )(x2d, gates, w_gate, w_up, w_down, ws_gate, ws_up, ws_down, g, b)


def _gather_cols(w, order):
    parts = []
    for item in order:
        if isinstance(item, tuple):
            parts.append(jnp.zeros(w.shape[:-1] + (item[1],), w.dtype))
        else:
            off, width = SEG_OFF[item]
            parts.append(w[..., off:off + width])
    return jnp.concatenate(parts, axis=-1)


def _rotate_half_cols(w):
    lead = w.shape[:-1]
    w4 = w.reshape(lead + (-1, 2, HEAD_DIM // 2))
    return jnp.concatenate([-w4[..., 1:2, :], w4[..., 0:1, :]], axis=-2).reshape(w.shape)


def _rope_tables(seq, width):
    half = HEAD_DIM // 2
    inv_freq = ROPE_THETA ** (-jnp.arange(half, dtype=F32) * (2.0 / HEAD_DIM))
    ang = jnp.arange(seq).astype(F32)[:, None] * inv_freq[None, :]
    reps = width // half
    return jnp.tile(jnp.cos(ang), (1, reps)), jnp.tile(jnp.sin(ang), (1, reps))


def _token_rows(tok, b, seq):
    t4 = tok.reshape(b, seq, NSA_KV_HEADS, HEAD_DIM).transpose(0, 2, 1, 3)
    return t4.reshape(b * NSA_KV_HEADS, seq // CMP_STRIDE, CMP_STRIDE * HEAD_DIM)


def kernel(x, w_in, fox_forget_bias, cmp_pos_k, cmp_w1_k, cmp_w2_k, cmp_pos_v, cmp_w1_v, cmp_w2_v, w_out,
           ln1_g, ln1_b, w_router, b_router, w_gate, w_up, w_down, ws_gate, ws_up, ws_down, ln2_g, ln2_b):
    b, seq, dm = x.shape
    depth = w_in.shape[0]
    n = b * seq
    alpha = float((2 * depth) ** 0.25)

    w_plain = _gather_cols(w_in, PLAIN_ORDER).astype(BF16)
    w_rope = _gather_cols(w_in, ROPE_ORDER)
    w_rope_a = w_rope.astype(BF16)
    w_rope_b = _rotate_half_cols(w_rope).astype(BF16)
    cos_t, sin_t = _rope_tables(seq, 256)
    fw, nw = FOX_HEADS * HEAD_DIM, NSA_HEADS * HEAD_DIM
    w_out_b = w_out.astype(BF16)

    x2d = x.reshape(n, dm)
    for l in range(depth):
        hp, hr = _project(x2d, w_plain[l], w_rope_a[l], w_rope_b[l], cos_t, sin_t, seq)
        hp3 = hp.reshape(b, seq, PLAIN_W)
        hr3 = hr.reshape(b, seq, ROPE_W)

        f_logit = hp3[:, :, SMALL_COL + SMALL_F:SMALL_COL + SMALL_F + FOX_HEADS]
        c_rows = _fox_cumsum(f_logit.transpose(0, 2, 1), fox_forget_bias[l].reshape(FOX_HEADS, 1))
        o_fox = _fox_attention(hp3, c_rows.transpose(0, 2, 1), c_rows)

        kvw = NSA_KV_HEADS * HEAD_DIM
        rk = _token_rows(hr3[:, :, ROPE_POS['nsa_kc']:ROPE_POS['nsa_kc'] + kvw], b, seq)
        rv = _token_rows(hp3[:, :, PLAIN_POS['nsa_vc']:PLAIN_POS['nsa_vc'] + kvw], b, seq)
        k_cmp, v_cmp = _nsa_compress(
            rk, rv, cmp_pos_k[l].reshape(1, -1), cmp_pos_v[l].reshape(1, -1),
            cmp_w1_k[l].astype(BF16), cmp_w2_k[l].astype(BF16), cmp_w1_v[l].astype(BF16), cmp_w2_v[l].astype(BF16))
        n_rows = seq // CMP_STRIDE
        o_nsa = _nsa_attention(hr3, hp3, k_cmp.reshape(b, NSA_KV_HEADS, n_rows, HEAD_DIM),
                               v_cmp.reshape(b, NSA_KV_HEADS, n_rows, HEAD_DIM))

        o_dsa = _dsa_attention(hr3, hp3)

        x2d = _outproj_ln(o_fox.reshape(n, fw), o_nsa.reshape(n, nw), o_dsa.reshape(n, nw), x2d,
                          w_out_b[l, :fw], w_out_b[l, fw:fw + nw], w_out_b[l, fw + nw:],
                          ln1_g[l].reshape(1, dm), ln1_b[l].reshape(1, dm), alpha)

        gates_t = _router(x2d, w_router[l].T.astype(BF16), b_router[l].reshape(N_EXPERTS, 1))
        x2d = _moe_ln(x2d, gates_t.T, w_gate[l].astype(BF16), w_up[l].astype(BF16), w_down[l].astype(BF16),
                      ws_gate[l].astype(BF16), ws_up[l].astype(BF16), ws_down[l].astype(BF16),
                      ln2_g[l].reshape(1, dm), ln2_b[l].reshape(1, dm), alpha)
    return x2d.reshape(b, seq, dm)
```

```python
import functools

import numpy as np
import jax
import jax.numpy as jnp
from jax import lax
from jax.experimental import pallas as pl
from jax.experimental.pallas import tpu as pltpu

D_MODEL = 1024
HEAD_DIM = 64
FOX_HEADS = 4
NSA_HEADS = 6
NSA_KV_HEADS = 2
NSA_REP = NSA_HEADS // NSA_KV_HEADS
DSA_HEADS = 6
ROPE_THETA = 10000.0
CMP_LEN = 32
CMP_STRIDE = 16
CMP_HIDDEN = 2 * HEAD_DIM
SEL_BLOCK = 64
SEL_TOPN = 16
WINDOW = 512
IDX_HEADS = 4
IDX_DIM = 64
DSA_TOPK = 256
N_EXPERTS = 64
N_GROUPS = 8
GROUP_SIZE = N_EXPERTS // N_GROUPS
TOPK_GROUPS = 4
MOE_TOPK = 8
EXPERT_DIM = 256
SHARED_DIM = 256
ROUTED_SCALE = 2.5
LN_EPS = 1e-5
NEG = -1e30
FORCE = 1e6
SCALE = HEAD_DIM ** -0.5

F32 = jnp.float32
BF16 = jnp.bfloat16
I32 = jnp.int32

VMEM_LIMIT_BYTES = 52 * 1024 * 1024
LANES = 128
KEY_STEP = 512

SEGMENTS = (
    ('fox_q', FOX_HEADS * HEAD_DIM), ('fox_k', FOX_HEADS * HEAD_DIM),
    ('fox_v', FOX_HEADS * HEAD_DIM), ('fox_f', FOX_HEADS),
    ('nsa_q', NSA_HEADS * HEAD_DIM),
    ('nsa_kc', NSA_KV_HEADS * HEAD_DIM), ('nsa_vc', NSA_KV_HEADS * HEAD_DIM),
    ('nsa_ks', NSA_KV_HEADS * HEAD_DIM), ('nsa_vs', NSA_KV_HEADS * HEAD_DIM),
    ('nsa_kw', NSA_KV_HEADS * HEAD_DIM), ('nsa_vw', NSA_KV_HEADS * HEAD_DIM),
    ('nsa_g', 3 * NSA_HEADS),
    ('dsa_q', DSA_HEADS * HEAD_DIM), ('dsa_k', HEAD_DIM), ('dsa_v', HEAD_DIM),
    ('idx_q', IDX_HEADS * IDX_DIM), ('idx_k', IDX_DIM), ('idx_w', IDX_HEADS),
)
SEG_OFF = {}
_off = 0
for _name, _width in SEGMENTS:
    SEG_OFF[_name] = (_off, _width)
    _off += _width
N_IN = _off

PLAIN_ORDER = ('fox_q', 'fox_k', 'fox_v', 'nsa_vc', 'nsa_vs', 'nsa_vw', 'dsa_v', (None, 64),
               'fox_f', 'nsa_g', 'idx_w', (None, 2 * LANES - FOX_HEADS - 3 * NSA_HEADS - IDX_HEADS))
ROPE_ORDER = ('nsa_q', 'nsa_kc', 'nsa_ks', 'nsa_kw', 'dsa_q', 'dsa_k', 'idx_k', 'idx_q')


def _layout(order):
    pos, off = {}, 0
    for item in order:
        if isinstance(item, tuple):
            off += item[1]
        else:
            pos[item] = off
            off += SEG_OFF[item][1]
    return pos, off


PLAIN_POS, PLAIN_W = _layout(PLAIN_ORDER)
ROPE_POS, ROPE_W = _layout(ROPE_ORDER)
SMALL_COL = PLAIN_POS['fox_f']
SMALL_F = 0
SMALL_G = FOX_HEADS
SMALL_W = FOX_HEADS + 3 * NSA_HEADS


def _cparams(sem):
    return pltpu.CompilerParams(dimension_semantics=sem, vmem_limit_bytes=VMEM_LIMIT_BYTES)


def _nt_dot(a, b):
    return lax.dot_general(a, b, (((1,), (1,)), ((), ())), preferred_element_type=F32)


def _dot(a, b):
    return jnp.dot(a, b, preferred_element_type=F32)


def _softmax_rows(s):
    m = jnp.max(s, axis=-1, keepdims=True)
    p = jnp.exp(s - m)
    return p, jnp.sum(p, axis=-1, keepdims=True)


def _for_key_extent(q_last, seq, body):
    n = seq // KEY_STEP
    if n <= 1:
        body(seq)
        return
    c = q_last // KEY_STEP
    for i in range(n):
        pl.when(c == i)(functools.partial(body, (i + 1) * KEY_STEP))


def _layer_norm(z, g, b):
    mu = jnp.mean(z, axis=-1, keepdims=True)
    zc = z - mu
    var = jnp.mean(zc * zc, axis=-1, keepdims=True)
    return zc * lax.rsqrt(var + LN_EPS) * g + b


def _proj_kernel(x_ref, w_ref, o_ref, xb_ref):
    @pl.when(pl.program_id(1) == 0)
    def _():
        xb_ref[...] = x_ref[...].astype(BF16)

    o_ref[...] = _dot(xb_ref[...], w_ref[...])


def _proj_rope_kernel(x_ref, wa_ref, wb_ref, cos_ref, sin_ref, o_ref, xb_ref):
    @pl.when(pl.program_id(1) == 0)
    def _():
        xb_ref[...] = x_ref[...].astype(BF16)

    xb = xb_ref[...]
    o_ref[...] = _dot(xb, wa_ref[...]) * cos_ref[...] + _dot(xb, wb_ref[...]) * sin_ref[...]


def _project(x2d, w_plain, w_a, w_b, cos_t, sin_t, seq):
    n = x2d.shape[0]
    tm = min(1024, seq)
    tc = 256
    plain = pl.pallas_call(
        _proj_kernel,
        grid=(n // tm, PLAIN_W // tc),
        in_specs=[pl.BlockSpec((tm, D_MODEL), lambda i, j: (i, 0)),
                  pl.BlockSpec((D_MODEL, tc), lambda i, j: (0, j))],
        out_specs=pl.BlockSpec((tm, tc), lambda i, j: (i, j)),
        out_shape=jax.ShapeDtypeStruct((n, PLAIN_W), F32),
        scratch_shapes=[pltpu.VMEM((tm, D_MODEL), BF16)],
        compiler_params=_cparams(("parallel", "arbitrary")),
    )(x2d, w_plain)
    nt = seq // tm
    roped = pl.pallas_call(
        _proj_rope_kernel,
        grid=(n // tm, ROPE_W // tc),
        in_specs=[pl.BlockSpec((tm, D_MODEL), lambda i, j: (i, 0)),
                  pl.BlockSpec((D_MODEL, tc), lambda i, j: (0, j)),
                  pl.BlockSpec((D_MODEL, tc), lambda i, j: (0, j)),
                  pl.BlockSpec((tm, tc), lambda i, j: (i % nt, 0)),
                  pl.BlockSpec((tm, tc), lambda i, j: (i % nt, 0))],
        out_specs=pl.BlockSpec((tm, tc), lambda i, j: (i, j)),
        out_shape=jax.ShapeDtypeStruct((n, ROPE_W), F32),
        scratch_shapes=[pltpu.VMEM((tm, D_MODEL), BF16)],
        compiler_params=_cparams(("parallel", "arbitrary")),
    )(x2d, w_a, w_b, cos_t, sin_t)
    return plain, roped


def _fox_cum_kernel(f_ref, fb_ref, o_ref):
    x = f_ref[0] + fb_ref[...]
    c = jnp.minimum(x, 0.0) - jnp.log1p(jnp.exp(-jnp.abs(x)))
    seq = c.shape[-1]
    lane = lax.broadcasted_iota(I32, c.shape, 1)
    sh = 1
    while sh < seq:
        c = c + jnp.where(lane >= sh, pltpu.roll(c, sh, 1), 0.0)
        sh *= 2
    o_ref[0] = c


def _fox_cumsum(f_rows, f_bias):
    b, h, seq = f_rows.shape
    return pl.pallas_call(
        _fox_cum_kernel,
        grid=(b,),
        in_specs=[pl.BlockSpec((1, h, seq), lambda i: (i, 0, 0)),
                  pl.BlockSpec((h, 1), lambda i: (0, 0))],
        out_specs=pl.BlockSpec((1, h, seq), lambda i: (i, 0, 0)),
        out_shape=jax.ShapeDtypeStruct((b, h, seq), F32),
        compiler_params=_cparams(("parallel",)),
    )(f_rows, f_bias)


def _fox_kernel(q_ref, k_ref, v_ref, cc_ref, cr_ref, o_ref, *, tq, seq):
    q0 = pl.program_id(1) * tq

    def body(klen):
        qpos = q0 + lax.broadcasted_iota(I32, (tq, klen), 0)
        causal = lax.broadcasted_iota(I32, (tq, klen), 1) <= qpos
        for h in range(FOX_HEADS):
            sl = slice(h * HEAD_DIM, (h + 1) * HEAD_DIM)
            qh = (q_ref[0, :, sl] * SCALE).astype(BF16)
            kh = k_ref[0, :klen, sl].astype(BF16)
            vh = v_ref[0, :klen, sl].astype(BF16)
            s = _nt_dot(qh, kh) + (cc_ref[0, :, h:h + 1] - cr_ref[0, h:h + 1, :klen])
            p, l = _softmax_rows(jnp.where(causal, s, NEG))
            o_ref[0, :, sl] = (_dot(p.astype(BF16), vh) / l).astype(o_ref.dtype)

    _for_key_extent(q0 + tq - 1, seq, body)


def _fox_attention(hp3, c_cols, c_rows):
    b, seq, _ = hp3.shape
    tq = min(256, seq)
    w = FOX_HEADS * HEAD_DIM
    return pl.pallas_call(
        functools.partial(_fox_kernel, tq=tq, seq=seq),
        grid=(b, seq // tq),
        in_specs=[pl.BlockSpec((1, tq, w), lambda i, j: (i, j, PLAIN_POS['fox_q'] // w)),
                  pl.BlockSpec((1, seq, w), lambda i, j: (i, 0, PLAIN_POS['fox_k'] // w)),
                  pl.BlockSpec((1, seq, w), lambda i, j: (i, 0, PLAIN_POS['fox_v'] // w)),
                  pl.BlockSpec((1, tq, FOX_HEADS), lambda i, j: (i, j, 0)),
                  pl.BlockSpec((1, FOX_HEADS, seq), lambda i, j: (i, 0, 0))],
        out_specs=pl.BlockSpec((1, tq, w), lambda i, j: (i, j, 0)),
        out_shape=jax.ShapeDtypeStruct((b, seq, w), BF16),
        compiler_params=_cparams(("parallel", "arbitrary")),
    )(hp3, hp3, hp3, c_cols, c_rows)


def _gelu_tanh(x):
    return 0.5 * x * (1.0 + jnp.tanh(np.float32(np.sqrt(2.0 / np.pi)) * (x + 0.044715 * (x * x * x))))


def _compress_kernel(rk_ref, rv_ref, pek_ref, pev_ref, w1k_ref, w2k_ref, w1v_ref, w2v_ref, ok_ref, ov_ref):
    half = CMP_STRIDE * HEAD_DIM

    def one(r_ref, pe_ref, w1_ref, w2_ref, o_ref):
        r = r_ref[0]
        n_rows = r.shape[0]
        lo = _dot((r + pe_ref[:, :half]).astype(BF16), w1_ref[:half, :])
        hi = _dot((r + pe_ref[:, half:]).astype(BF16), w1_ref[half:, :])
        hid = _gelu_tanh(lo + pltpu.roll(hi, n_rows - 1, 0))
        o_ref[0] = _dot(hid.astype(BF16), w2_ref[...])

    one(rk_ref, pek_ref, w1k_ref, w2k_ref, ok_ref)
    one(rv_ref, pev_ref, w1v_ref, w2v_ref, ov_ref)


def _nsa_compress(rk, rv, pek, pev, w1k, w2k, w1v, w2v):
    bg, rows, width = rk.shape
    tok = pl.BlockSpec((1, rows, width), lambda i: (i, 0, 0))
    full = lambda a: pl.BlockSpec(a.shape, lambda i: (0,) * a.ndim)
    out = pl.BlockSpec((1, rows, HEAD_DIM), lambda i: (i, 0, 0))
    return pl.pallas_call(
        _compress_kernel,
        grid=(bg,),
        in_specs=[tok, tok, full(pek), full(pev), full(w1k), full(w2k), full(w1v), full(w2v)],
        out_specs=[out, out],
        out_shape=[jax.ShapeDtypeStruct((bg, rows, HEAD_DIM), F32)] * 2,
        compiler_params=_cparams(("parallel",)),
    )(rk, rv, pek, pev, w1k, w2k, w1v, w2v)


def _split3_dot(a, b01):
    a1 = a.astype(BF16)
    r1 = a - a1.astype(F32)
    a2 = r1.astype(BF16)
    a3 = (r1 - a2.astype(F32)).astype(BF16)
    return _dot(a1, b01) + _dot(a2, b01) + _dot(a3, b01)


def _nsa_kernel(q_ref, kc_ref, vc_ref, ks_ref, kw_ref, vs_ref, vw_ref, sm_ref, o_ref, oslc_ref, *, tq, seq, wlen):
    q0 = pl.program_id(1) * tq
    n_c = kc_ref.shape[2]
    n_s = seq // SEL_BLOCK
    rows = NSA_REP * tq
    tcol = q0 + lax.broadcasted_iota(I32, (tq, 1), 0)
    tcol_r = jnp.concatenate([tcol] * NSA_REP, axis=0)

    cidx = lax.broadcasted_iota(I32, (rows, n_c), 1)
    vis_r = (cidx * CMP_STRIDE + (CMP_LEN - 1)) <= tcol_r
    any_vis_r = (tcol_r >= (CMP_LEN - 1)).astype(F32)
    oc = lax.broadcasted_iota(I32, (n_c, n_s), 0) * CMP_STRIDE
    ob = lax.broadcasted_iota(I32, (n_c, n_s), 1) * SEL_BLOCK
    overlap = ((oc < ob + SEL_BLOCK) & (oc + CMP_LEN > ob)).astype(BF16)
    jj = lax.broadcasted_iota(I32, (tq, n_s), 1)
    blk_t = tcol // SEL_BLOCK
    w0 = pl.multiple_of(jnp.maximum(q0 + tq - wlen, 0), 8)
    dist = tcol_r - (w0 + lax.broadcasted_iota(I32, (rows, wlen), 1))
    band_r = (dist >= 0) & (dist < WINDOW)
    gates = sm_ref[0]

    for g in range(NSA_KV_HEADS):
        ksl = slice(g * HEAD_DIM, (g + 1) * HEAD_DIM)
        qs = (jnp.concatenate(
            [q_ref[0, :, (g * NSA_REP + r) * HEAD_DIM:(g * NSA_REP + r + 1) * HEAD_DIM] for r in range(NSA_REP)],
            axis=0) * SCALE).astype(BF16)

        s_c = _nt_dot(qs, kc_ref[0, g].astype(BF16))
        p_c, l_c = _softmax_rows(jnp.where(vis_r, s_c, NEG))
        p_c = p_c / l_c * any_vis_r
        o_cmp = _dot(p_c.astype(BF16), vc_ref[0, g].astype(BF16))

        p_sum = p_c[0:tq]
        for r in range(1, NSA_REP):
            p_sum = p_sum + p_c[r * tq:(r + 1) * tq]
        imp = _split3_dot(p_sum, overlap)
        forced = (jj == 0) | (jj == blk_t) | (jj == blk_t - 1)
        imp = jnp.where(jj <= blk_t, jnp.where(forced, FORCE, imp), -1.0)
        rank = jnp.zeros((tq, n_s), F32)
        for j2 in range(n_s):
            col = imp[:, j2:j2 + 1]
            beats = (col > imp) | ((col == imp) & (j2 < jj))
            rank = rank + beats.astype(F32)
        sel = (rank < float(min(SEL_TOPN, n_s))).astype(F32)
        sel_r = jnp.concatenate([sel] * NSA_REP, axis=0).astype(BF16)

        def slc_body(klen, sel_r=sel_r, qs=qs, ksl=ksl):
            expand = (lax.broadcasted_iota(I32, (n_s, klen), 1) // SEL_BLOCK
                      == lax.broadcasted_iota(I32, (n_s, klen), 0)).astype(BF16)
            causal_r = lax.broadcasted_iota(I32, (rows, klen), 1) <= tcol_r
            slc_mask = (_dot(sel_r, expand) > 0.5) & causal_r
            s_s = _nt_dot(qs, ks_ref[0, :klen, ksl].astype(BF16))
            p_s, l_s = _softmax_rows(jnp.where(slc_mask, s_s, NEG))
            oslc_ref[...] = _dot(p_s.astype(BF16), vs_ref[0, :klen, ksl].astype(BF16)) / l_s

        _for_key_extent(q0 + tq - 1, seq, slc_body)
        o_slc = oslc_ref[...]

        kw = kw_ref[0, pl.ds(w0, wlen), ksl].astype(BF16)
        vw = vw_ref[0, pl.ds(w0, wlen), ksl].astype(BF16)
        p_w, l_w = _softmax_rows(jnp.where(band_r, _nt_dot(qs, kw), NEG))
        o_win = _dot(p_w.astype(BF16), vw) / l_w

        for r in range(NSA_REP):
            head = g * NSA_REP + r
            rs = slice(r * tq, (r + 1) * tq)

            def gate(branch, head=head):
                col = SMALL_G + branch * NSA_HEADS + head
                return 1.0 / (1.0 + jnp.exp(-gates[:, col:col + 1]))

            out = gate(0) * o_cmp[rs] + gate(1) * o_slc[rs] + gate(2) * o_win[rs]
            o_ref[0, :, head * HEAD_DIM:(head + 1) * HEAD_DIM] = out.astype(o_ref.dtype)


def _nsa_attention(hr3, hp3, k_cmp, v_cmp):
    b, seq, _ = hr3.shape
    tq = min(128, seq)
    wlen = min(WINDOW + tq, seq)
    qw = NSA_HEADS * HEAD_DIM
    kvw = NSA_KV_HEADS * HEAD_DIM
    n_c = k_cmp.shape[2]
    cmp_spec = pl.BlockSpec((1, NSA_KV_HEADS, n_c, HEAD_DIM), lambda i, j: (i, 0, 0, 0))

    def seq_spec(col):
        return pl.BlockSpec((1, seq, kvw), lambda i, j: (i, 0, col // kvw))

    return pl.pallas_call(
        functools.partial(_nsa_kernel, tq=tq, seq=seq, wlen=wlen),
        grid=(b, seq // tq),
        in_specs=[pl.BlockSpec((1, tq, qw), lambda i, j: (i, j, ROPE_POS['nsa_q'] // qw)),
                  cmp_spec, cmp_spec,
                  seq_spec(ROPE_POS['nsa_ks']), seq_spec(ROPE_POS['nsa_kw']),
                  seq_spec(PLAIN_POS['nsa_vs']), seq_spec(PLAIN_POS['nsa_vw']),
                  pl.BlockSpec((1, tq, LANES), lambda i, j: (i, j, SMALL_COL // LANES))],
        out_specs=pl.BlockSpec((1, tq, qw), lambda i, j: (i, j, 0)),
        out_shape=jax.ShapeDtypeStruct((b, seq, qw), BF16),
        scratch_shapes=[pltpu.VMEM((NSA_REP * tq, HEAD_DIM), F32)],
        compiler_params=_cparams(("parallel", "arbitrary")),
    )(hr3, k_cmp, v_cmp, hr3, hr3, hp3, hp3, hp3)


def _row_count(mask):
    return jnp.sum(mask.astype(F32), axis=-1, keepdims=True)


def _dsa_kernel(q_ref, kk_ref, v_ref, iq_ref, sm_ref, o_ref, key_ref, *, tq, seq, topk):
    q0 = pl.program_id(1) * tq
    tcol = q0 + lax.broadcasted_iota(I32, (tq, 1), 0)
    w_idx = sm_ref[0, :, SMALL_W:SMALL_W + IDX_HEADS] * (IDX_HEADS ** -0.5) * (IDX_DIM ** -0.5)
    kf = float(topk)

    def body(klen):
        kpos = lax.broadcasted_iota(I32, (tq, klen), 1)
        causal = kpos <= tcol
        ik = kk_ref[0, :klen, HEAD_DIM:2 * HEAD_DIM].astype(BF16)
        score = jnp.zeros((tq, klen), F32)
        for h in range(IDX_HEADS):
            d = _nt_dot(iq_ref[0, :, h * IDX_DIM:(h + 1) * IDX_DIM].astype(BF16), ik)
            score = score + w_idx[:, h:h + 1] * jnp.maximum(d, 0.0)
        score = jnp.where(causal, score, NEG)

        bits = lax.bitcast_convert_type(score, I32)
        key_ref[:, :klen] = jnp.where(bits < 0, bits ^ 0x7FFFFFFF, bits)

        def tau_step(i, tau):
            cand = tau + jnp.left_shift(jnp.int32(1), 31 - i)
            cnt = _row_count(key_ref[:, :klen] >= cand)
            return jnp.where(cnt >= kf, cand, tau)

        tau = lax.fori_loop(0, 32, tau_step, jnp.full((tq, 1), -2 ** 31, I32))
        need = kf - _row_count(key_ref[:, :klen] > tau)

        n_bits = (klen - 1).bit_length()

        def cut_step(i, cut):
            cand = cut + jnp.left_shift(jnp.int32(1), n_bits - 1 - i)
            idx = lax.broadcasted_iota(I32, (tq, klen), 1)
            cnt = _row_count((key_ref[:, :klen] == tau) & (idx < cand))
            return jnp.where(cnt < need, cand, cut)

        cut = lax.fori_loop(0, n_bits, cut_step, jnp.zeros((tq, 1), I32))
        key = key_ref[:, :klen]
        chosen = ((key > tau) | ((key == tau) & (kpos <= cut))) & causal

        k = kk_ref[0, :klen, 0:HEAD_DIM].astype(BF16)
        v = v_ref[0, :klen, 0:HEAD_DIM].astype(BF16)
        for h in range(DSA_HEADS):
            sl = slice(h * HEAD_DIM, (h + 1) * HEAD_DIM)
            s = _nt_dot((q_ref[0, :, sl] * SCALE).astype(BF16), k)
            p, l = _softmax_rows(jnp.where(chosen, s, NEG))
            o_ref[0, :, sl] = (_dot(p.astype(BF16), v) / l).astype(o_ref.dtype)

    _for_key_extent(q0 + tq - 1, seq, body)


def _dsa_attention(hr3, hp3):
    b, seq, _ = hr3.shape
    tq = min(128, seq)
    topk = min(DSA_TOPK, seq // 4)
    qw = DSA_HEADS * HEAD_DIM
    iqw = IDX_HEADS * IDX_DIM
    return pl.pallas_call(
        functools.partial(_dsa_kernel, tq=tq, seq=seq, topk=topk),
        grid=(b, seq // tq),
        in_specs=[pl.BlockSpec((1, tq, qw), lambda i, j: (i, j, ROPE_POS['dsa_q'] // qw)),
                  pl.BlockSpec((1, seq, LANES), lambda i, j: (i, 0, ROPE_POS['dsa_k'] // LANES)),
                  pl.BlockSpec((1, seq, LANES), lambda i, j: (i, 0, PLAIN_POS['dsa_v'] // LANES)),
                  pl.BlockSpec((1, tq, iqw), lambda i, j: (i, j, ROPE_POS['idx_q'] // iqw)),
                  pl.BlockSpec((1, tq, LANES), lambda i, j: (i, j, SMALL_COL // LANES))],
        out_specs=pl.BlockSpec((1, tq, qw), lambda i, j: (i, j, 0)),
        out_shape=jax.ShapeDtypeStruct((b, seq, qw), BF16),
        scratch_shapes=[pltpu.VMEM((tq, seq), I32)],
        compiler_params=_cparams(("parallel", "arbitrary")),
    )(hr3, hr3, hp3, hr3, hp3)


def _outproj_kernel(of_ref, on_ref, od_ref, x_ref, wf_ref, wn_ref, wd_ref, g_ref, b_ref, o_ref, *, alpha):
    mix = _dot(of_ref[...], wf_ref[...]) + _dot(on_ref[...], wn_ref[...]) + _dot(od_ref[...], wd_ref[...])
    o_ref[...] = _layer_norm(alpha * x_ref[...] + mix, g_ref[...], b_ref[...])


def _outproj_ln(o_fox, o_nsa, o_dsa, x2d, w_f, w_n, w_d, g, b, alpha):
    n = x2d.shape[0]
    tm = min(512, n)
    row = lambda a: pl.BlockSpec((tm, a.shape[1]), lambda i: (i, 0))
    full = lambda a: pl.BlockSpec(a.shape, lambda i: (0, 0))
    return pl.pallas_call(
        functools.partial(_outproj_kernel, alpha=alpha),
        grid=(n // tm,),
        in_specs=[row(o_fox), row(o_nsa), row(o_dsa), row(x2d), full(w_f), full(w_n), full(w_d), full(g), full(b)],
        out_specs=pl.BlockSpec((tm, D_MODEL), lambda i: (i, 0)),
        out_shape=jax.ShapeDtypeStruct((n, D_MODEL), F32),
        compiler_params=_cparams(("parallel",)),
    )(o_fox, o_nsa, o_dsa, x2d, w_f, w_n, w_d, g, b)


def _router_kernel(x_ref, wr_ref, br_ref, g_ref):
    tn = x_ref.shape[0]
    logits = _nt_dot(wr_ref[...], x_ref[...].astype(BF16))
    s = 1.0 / (1.0 + jnp.exp(-logits))
    sb = s + br_ref[...]
    low = jnp.float32(-3e38)

    grp = []
    for gi in range(N_GROUPS):
        blk = sb[gi * GROUP_SIZE:(gi + 1) * GROUP_SIZE]
        m1 = jnp.max(blk, axis=0, keepdims=True)
        is_max = blk == m1
        n_max = jnp.sum(is_max.astype(F32), axis=0, keepdims=True)
        m2 = jnp.max(jnp.where(is_max, low, blk), axis=0, keepdims=True)
        grp.append(m1 + jnp.where(n_max >= 2.0, m1, m2))
    masked = []
    for gi in range(N_GROUPS):
        rank = jnp.zeros((1, tn), F32)
        for g2 in range(N_GROUPS):
            if g2 == gi:
                continue
            beats = (grp[g2] > grp[gi]) | ((grp[g2] == grp[gi]) if g2 < gi else False)
            rank = rank + beats.astype(F32)
        keep = rank < float(TOPK_GROUPS)
        masked.append(jnp.where(keep, sb[gi * GROUP_SIZE:(gi + 1) * GROUP_SIZE], NEG))
    masked = jnp.concatenate(masked, axis=0)

    eidx = lax.broadcasted_iota(I32, (N_EXPERTS, tn), 0)
    rank = jnp.zeros((N_EXPERTS, tn), F32)
    for e2 in range(N_EXPERTS):
        row = masked[e2:e2 + 1]
        beats = (row > masked) | ((row == masked) & (e2 < eidx))
        rank = rank + beats.astype(F32)
    gw = jnp.where(rank < float(MOE_TOPK), s, 0.0)
    g_ref[...] = gw / jnp.sum(gw, axis=0, keepdims=True) * ROUTED_SCALE


def _router(x2d, wr_t, br_col):
    n = x2d.shape[0]
    tn = min(512, n)
    return pl.pallas_call(
        _router_kernel,
        grid=(n // tn,),
        in_specs=[pl.BlockSpec((tn, D_MODEL), lambda i: (i, 0)),
                  pl.BlockSpec(wr_t.shape, lambda i: (0, 0)),
                  pl.BlockSpec(br_col.shape, lambda i: (0, 0))],
        out_specs=pl.BlockSpec((N_EXPERTS, tn), lambda i: (0, i)),
        out_shape=jax.ShapeDtypeStruct((N_EXPERTS, n), F32),
        compiler_params=_cparams(("parallel",)),
    )(x2d, wr_t, br_col)


def _silu(x):
    return x / (1.0 + jnp.exp(-x))


def _moe_kernel(x_ref, gt_ref, wg_ref, wu_ref, wd_ref, sg_ref, su_ref, sd_ref, lg_ref, lb_ref, o_ref,
                xb_ref, acc_ref, *, alpha):
    e = pl.program_id(1)

    @pl.when(e == 0)
    def _():
        xb = x_ref[...].astype(BF16)
        xb_ref[...] = xb
        h = _silu(_dot(xb, sg_ref[...])) * _dot(xb, su_ref[...])
        acc_ref[...] = _dot(h.astype(BF16), sd_ref[...])

    xb = xb_ref[...]
    h = (_silu(_dot(xb, wg_ref[0])) * _dot(xb, wu_ref[0])).astype(BF16)
    gates = gt_ref[...]
    lane = lax.broadcasted_iota(I32, gates.shape, 1)
    gcol = jnp.sum(jnp.where(lane == e, gates, 0.0), axis=1, keepdims=True)
    chunk = 256
    for c in range(D_MODEL // chunk):
        cs = slice(c * chunk, (c + 1) * chunk)
        acc_ref[:, cs] += gcol * _dot(h, wd_ref[0, :, cs])

    @pl.when(e == pl.num_programs(1) - 1)
    def _():
        o_ref[...] = _layer_norm(alpha * x_ref[...] + acc_ref[...], lg_ref[...], lb_ref[...])


def _moe_ln(x2d, gates, w_gate, w_up, w_down, ws_gate, ws_up, ws_down, g, b, alpha):
    n = x2d.shape[0]
    tn = min(1024, n)
    full = lambda a: pl.BlockSpec(a.shape, lambda i, e: (0,) * a.ndim)
    return pl.pallas_call(
        functools.partial(_moe_kernel, alpha=alpha),
        grid=(n // tn, N_EXPERTS),
        in_specs=[pl.BlockSpec((tn, D_MODEL), lambda i, e: (i, 0)),
                  pl.BlockSpec((tn, N_EXPERTS), lambda i, e: (i, 0)),
                  pl.BlockSpec((1, D_MODEL, EXPERT_DIM), lambda i, e: (e, 0, 0)),
                  pl.BlockSpec((1, D_MODEL, EXPERT_DIM), lambda i, e: (e, 0, 0)),
                  pl.BlockSpec((1, EXPERT_DIM, D_MODEL), lambda i, e: (e, 0, 0)),
                  full(ws_gate), full(ws_up), full(ws_down), full(g), full(b)],
        out_specs=pl.BlockSpec((tn, D_MODEL), lambda i, e: (i, 0)),
        out_shape=jax.ShapeDtypeStruct((n, D_MODEL), F32),
        scratch_shapes=[pltpu.VMEM((tn, D_MODEL), BF16), pltpu.VMEM((tn, D_MODEL), F32)],
        compiler_params=_cparams(("parallel", "arbitrary")),
    )(x2d, gates, w_gate, w_up, w_down, ws_gate, ws_up, ws_down, g, b)


def _gather_cols(w, order):
    parts = []
    for item in order:
        if isinstance(item, tuple):
            parts.append(jnp.zeros(w.shape[:-1] + (item[1],), w.dtype))
        else:
            off, width = SEG_OFF[item]
            parts.append(w[..., off:off + width])
    return jnp.concatenate(parts, axis=-1)


def _rotate_half_cols(w):
    lead = w.shape[:-1]
    w4 = w.reshape(lead + (-1, 2, HEAD_DIM // 2))
    return jnp.concatenate([-w4[..., 1:2, :], w4[..., 0:1, :]], axis=-2).reshape(w.shape)


def _rope_tables(seq, width):
    half = HEAD_DIM // 2
    inv_freq = ROPE_THETA ** (-jnp.arange(half, dtype=F32) * (2.0 / HEAD_DIM))
    ang = jnp.arange(seq).astype(F32)[:, None] * inv_freq[None, :]
    reps = width // half
    return jnp.tile(jnp.cos(ang), (1, reps)), jnp.tile(jnp.sin(ang), (1, reps))


def _token_rows(tok, b, seq):
    t4 = tok.reshape(b, seq, NSA_KV_HEADS, HEAD_DIM).transpose(0, 2, 1, 3)
    return t4.reshape(b * NSA_KV_HEADS, seq // CMP_STRIDE, CMP_STRIDE * HEAD_DIM)


def kernel(x, w_in, fox_forget_bias, cmp_pos_k, cmp_w1_k, cmp_w2_k, cmp_pos_v, cmp_w1_v, cmp_w2_v, w_out,
           ln1_g, ln1_b, w_router, b_router, w_gate, w_up, w_down, ws_gate, ws_up, ws_down, ln2_g, ln2_b):
    b, seq, dm = x.shape
    depth = w_in.shape[0]
    n = b * seq
    alpha = float((2 * depth) ** 0.25)

    w_plain = _gather_cols(w_in, PLAIN_ORDER).astype(BF16)
    w_rope = _gather_cols(w_in, ROPE_ORDER)
    w_rope_a = w_rope.astype(BF16)
    w_rope_b = _rotate_half_cols(w_rope).astype(BF16)
    cos_t, sin_t = _rope_tables(seq, 256)
    fw, nw = FOX_HEADS * HEAD_DIM, NSA_HEADS * HEAD_DIM
    w_out_b = w_out.astype(BF16)

    x2d = x.reshape(n, dm)
    for l in range(depth):
        hp, hr = _project(x2d, w_plain[l], w_rope_a[l], w_rope_b[l], cos_t, sin_t, seq)
        hp3 = hp.reshape(b, seq, PLAIN_W)
        hr3 = hr.reshape(b, seq, ROPE_W)

        f_logit = hp3[:, :, SMALL_COL + SMALL_F:SMALL_COL + SMALL_F + FOX_HEADS]
        c_rows = _fox_cumsum(f_logit.transpose(0, 2, 1), fox_forget_bias[l].reshape(FOX_HEADS, 1))
        o_fox = _fox_attention(hp3, c_rows.transpose(0, 2, 1), c_rows)

        kvw = NSA_KV_HEADS * HEAD_DIM
        rk = _token_rows(hr3[:, :, ROPE_POS['nsa_kc']:ROPE_POS['nsa_kc'] + kvw], b, seq)
        rv = _token_rows(hp3[:, :, PLAIN_POS['nsa_vc']:PLAIN_POS['nsa_vc'] + kvw], b, seq)
        k_cmp, v_cmp = _nsa_compress(
            rk, rv, cmp_pos_k[l].reshape(1, -1), cmp_pos_v[l].reshape(1, -1),
            cmp_w1_k[l].astype(BF16), cmp_w2_k[l].astype(BF16), cmp_w1_v[l].astype(BF16), cmp_w2_v[l].astype(BF16))
        n_rows = seq // CMP_STRIDE
        o_nsa = _nsa_attention(hr3, hp3, k_cmp.reshape(b, NSA_KV_HEADS, n_rows, HEAD_DIM),
                               v_cmp.reshape(b, NSA_KV_HEADS, n_rows, HEAD_DIM))

        o_dsa = _dsa_attention(hr3, hp3)

        x2d = _outproj_ln(o_fox.reshape(n, fw), o_nsa.reshape(n, nw), o_dsa.reshape(n, nw), x2d,
                          w_out_b[l, :fw], w_out_b[l, fw:fw + nw], w_out_b[l, fw + nw:],
                          ln1_g[l].reshape(1, dm), ln1_b[l].reshape(1, dm), alpha)

        gates_t = _router(x2d, w_router[l].T.astype(BF16), b_router[l].reshape(N_EXPERTS, 1))
        x2d = _moe_ln(x2d, gates_t.T, w_gate[l].astype(BF16), w_up[l].astype(BF16), w_down[l].astype(BF16),
                      ws_gate[l].astype(BF16), ws_up[l].astype(BF16), ws_down[l].astype(BF16),
                      ln2_g[l].reshape(1, dm), ln2_b[l].reshape(1, dm), alpha)
    return x2d.reshape(b, seq, dm)
```

```python
import functools

import numpy as np
import jax
import jax.numpy as jnp
from jax import lax
from jax.experimental import pallas as pl
from jax.experimental.pallas import tpu as pltpu

D_MODEL = 1024
HEAD_DIM = 64
FOX_HEADS = 4
NSA_HEADS = 6
NSA_KV_HEADS = 2
NSA_REP = NSA_HEADS // NSA_KV_HEADS
DSA_HEADS = 6
ROPE_THETA = 10000.0
CMP_LEN = 32
CMP_STRIDE = 16
CMP_HIDDEN = 2 * HEAD_DIM
SEL_BLOCK = 64
SEL_TOPN = 16
WINDOW = 512
IDX_HEADS = 4
IDX_DIM = 64
DSA_TOPK = 256
N_EXPERTS = 64
N_GROUPS = 8
GROUP_SIZE = N_EXPERTS // N_GROUPS
TOPK_GROUPS = 4
MOE_TOPK = 8
EXPERT_DIM = 256
SHARED_DIM = 256
ROUTED_SCALE = 2.5
LN_EPS = 1e-5
NEG = -1e30
FORCE = 1e6
SCALE = HEAD_DIM ** -0.5

F32 = jnp.float32
BF16 = jnp.bfloat16
I32 = jnp.int32

VMEM_LIMIT_BYTES = 52 * 1024 * 1024
LANES = 128
KEY_STEP = 512

SEGMENTS = (
    ('fox_q', FOX_HEADS * HEAD_DIM), ('fox_k', FOX_HEADS * HEAD_DIM),
    ('fox_v', FOX_HEADS * HEAD_DIM), ('fox_f', FOX_HEADS),
    ('nsa_q', NSA_HEADS * HEAD_DIM),
    ('nsa_kc', NSA_KV_HEADS * HEAD_DIM), ('nsa_vc', NSA_KV_HEADS * HEAD_DIM),
    ('nsa_ks', NSA_KV_HEADS * HEAD_DIM), ('nsa_vs', NSA_KV_HEADS * HEAD_DIM),
    ('nsa_kw', NSA_KV_HEADS * HEAD_DIM), ('nsa_vw', NSA_KV_HEADS * HEAD_DIM),
    ('nsa_g', 3 * NSA_HEADS),
    ('dsa_q', DSA_HEADS * HEAD_DIM), ('dsa_k', HEAD_DIM), ('dsa_v', HEAD_DIM),
    ('idx_q', IDX_HEADS * IDX_DIM), ('idx_k', IDX_DIM), ('idx_w', IDX_HEADS),
)
SEG_OFF = {}
_off = 0
for _name, _width in SEGMENTS:
    SEG_OFF[_name] = (_off, _width)
    _off += _width
N_IN = _off

PLAIN_ORDER = ('fox_q', 'fox_k', 'fox_v', 'nsa_vc', 'nsa_vs', 'nsa_vw', 'dsa_v', (None, 64),
               'fox_f', 'nsa_g', 'idx_w', (None, 2 * LANES - FOX_HEADS - 3 * NSA_HEADS - IDX_HEADS))
ROPE_ORDER = ('nsa_q', 'nsa_kc', 'nsa_ks', 'nsa_kw', 'dsa_q', 'dsa_k', 'idx_k', 'idx_q')


def _layout(order):
    pos, off = {}, 0
    for item in order:
        if isinstance(item, tuple):
            off += item[1]
        else:
            pos[item] = off
            off += SEG_OFF[item][1]
    return pos, off


PLAIN_POS, PLAIN_W = _layout(PLAIN_ORDER)
ROPE_POS, ROPE_W = _layout(ROPE_ORDER)
SMALL_COL = PLAIN_POS['fox_f']
SMALL_F = 0
SMALL_G = FOX_HEADS
SMALL_W = FOX_HEADS + 3 * NSA_HEADS


def _cparams(sem):
    return pltpu.CompilerParams(dimension_semantics=sem, vmem_limit_bytes=VMEM_LIMIT_BYTES)


def _nt_dot(a, b):
    return lax.dot_general(a, b, (((1,), (1,)), ((), ())), preferred_element_type=F32)


def _dot(a, b):
    return jnp.dot(a, b, preferred_element_type=F32)


def _softmax_rows(s):
    m = jnp.max(s, axis=-1, keepdims=True)
    p = jnp.exp(s - m)
    return p, jnp.sum(p, axis=-1, keepdims=True)


def _for_key_extent(q_last, seq, body):
    n = seq // KEY_STEP
    if n <= 1:
        body(seq)
        return
    c = q_last // KEY_STEP
    for i in range(n):
        pl.when(c == i)(functools.partial(body, (i + 1) * KEY_STEP))


def _layer_norm(z, g, b):
    mu = jnp.mean(z, axis=-1, keepdims=True)
    zc = z - mu
    var = jnp.mean(zc * zc, axis=-1, keepdims=True)
    return zc * lax.rsqrt(var + LN_EPS) * g + b


def _proj_kernel(x_ref, w_ref, o_ref, xb_ref):
    @pl.when(pl.program_id(1) == 0)
    def _():
        xb_ref[...] = x_ref[...].astype(BF16)

    o_ref[...] = _dot(xb_ref[...], w_ref[...])


def _proj_rope_kernel(x_ref, wa_ref, wb_ref, cos_ref, sin_ref, o_ref, xb_ref):
    @pl.when(pl.program_id(1) == 0)
    def _():
        xb_ref[...] = x_ref[...].astype(BF16)

    xb = xb_ref[...]
    o_ref[...] = _dot(xb, wa_ref[...]) * cos_ref[...] + _dot(xb, wb_ref[...]) * sin_ref[...]


def _project(x2d, w_plain, w_a, w_b, cos_t, sin_t, seq):
    n = x2d.shape[0]
    tm = min(1024, seq)
    tc = 256
    plain = pl.pallas_call(
        _proj_kernel,
        grid=(n // tm, PLAIN_W // tc),
        in_specs=[pl.BlockSpec((tm, D_MODEL), lambda i, j: (i, 0)),
                  pl.BlockSpec((D_MODEL, tc), lambda i, j: (0, j))],
        out_specs=pl.BlockSpec((tm, tc), lambda i, j: (i, j)),
        out_shape=jax.ShapeDtypeStruct((n, PLAIN_W), F32),
        scratch_shapes=[pltpu.VMEM((tm, D_MODEL), BF16)],
        compiler_params=_cparams(("parallel", "arbitrary")),
    )(x2d, w_plain)
    nt = seq // tm
    roped = pl.pallas_call(
        _proj_rope_kernel,
        grid=(n // tm, ROPE_W // tc),
        in_specs=[pl.BlockSpec((tm, D_MODEL), lambda i, j: (i, 0)),
                  pl.BlockSpec((D_MODEL, tc), lambda i, j: (0, j)),
                  pl.BlockSpec((D_MODEL, tc), lambda i, j: (0, j)),
                  pl.BlockSpec((tm, tc), lambda i, j: (i % nt, 0)),
                  pl.BlockSpec((tm, tc), lambda i, j: (i % nt, 0))],
        out_specs=pl.BlockSpec((tm, tc), lambda i, j: (i, j)),
        out_shape=jax.ShapeDtypeStruct((n, ROPE_W), F32),
        scratch_shapes=[pltpu.VMEM((tm, D_MODEL), BF16)],
        compiler_params=_cparams(("parallel", "arbitrary")),
    )(x2d, w_a, w_b, cos_t, sin_t)
    return plain, roped


def _fox_cum_kernel(f_ref, fb_ref, o_ref):
    x = f_ref[0] + fb_ref[...]
    c = jnp.minimum(x, 0.0) - jnp.log1p(jnp.exp(-jnp.abs(x)))
    seq = c.shape[-1]
    lane = lax.broadcasted_iota(I32, c.shape, 1)
    sh = 1
    while sh < seq:
        c = c + jnp.where(lane >= sh, pltpu.roll(c, sh, 1), 0.0)
        sh *= 2
    o_ref[0] = c


def _fox_cumsum(f_rows, f_bias):
    b, h, seq = f_rows.shape
    return pl.pallas_call(
        _fox_cum_kernel,
        grid=(b,),
        in_specs=[pl.BlockSpec((1, h, seq), lambda i: (i, 0, 0)),
                  pl.BlockSpec((h, 1), lambda i: (0, 0))],
        out_specs=pl.BlockSpec((1, h, seq), lambda i: (i, 0, 0)),
        out_shape=jax.ShapeDtypeStruct((b, h, seq), F32),
        compiler_params=_cparams(("parallel",)),
    )(f_rows, f_bias)


def _fox_kernel(q_ref, k_ref, v_ref, cc_ref, cr_ref, o_ref, *, tq, seq):
    q0 = pl.program_id(1) * tq

    def body(klen):
        qpos = q0 + lax.broadcasted_iota(I32, (tq, klen), 0)
        causal = lax.broadcasted_iota(I32, (tq, klen), 1) <= qpos
        for h in range(FOX_HEADS):
            sl = slice(h * HEAD_DIM, (h + 1) * HEAD_DIM)
            qh = (q_ref[0, :, sl] * SCALE).astype(BF16)
            kh = k_ref[0, :klen, sl].astype(BF16)
            vh = v_ref[0, :klen, sl].astype(BF16)
            s = _nt_dot(qh, kh) + (cc_ref[0, :, h:h + 1] - cr_ref[0, h:h + 1, :klen])
            p, l = _softmax_rows(jnp.where(causal, s, NEG))
            o_ref[0, :, sl] = (_dot(p.astype(BF16), vh) / l).astype(o_ref.dtype)

    _for_key_extent(q0 + tq - 1, seq, body)


def _fox_attention(hp3, c_cols, c_rows):
    b, seq, _ = hp3.shape
    tq = min(256, seq)
    w = FOX_HEADS * HEAD_DIM
    return pl.pallas_call(
        functools.partial(_fox_kernel, tq=tq, seq=seq),
        grid=(b, seq // tq),
        in_specs=[pl.BlockSpec((1, tq, w), lambda i, j: (i, j, PLAIN_POS['fox_q'] // w)),
                  pl.BlockSpec((1, seq, w), lambda i, j: (i, 0, PLAIN_POS['fox_k'] // w)),
                  pl.BlockSpec((1, seq, w), lambda i, j: (i, 0, PLAIN_POS['fox_v'] // w)),
                  pl.BlockSpec((1, tq, FOX_HEADS), lambda i, j: (i, j, 0)),
                  pl.BlockSpec((1, FOX_HEADS, seq), lambda i, j: (i, 0, 0))],
        out_specs=pl.BlockSpec((1, tq, w), lambda i, j: (i, j, 0)),
        out_shape=jax.ShapeDtypeStruct((b, seq, w), BF16),
        compiler_params=_cparams(("parallel", "arbitrary")),
    )(hp3, hp3, hp3, c_cols, c_rows)


def _gelu_tanh(x):
    return 0.5 * x * (1.0 + jnp.tanh(np.float32(np.sqrt(2.0 / np.pi)) * (x + 0.044715 * (x * x * x))))


def _compress_kernel(rk_ref, rv_ref, pek_ref, pev_ref, w1k_ref, w2k_ref, w1v_ref, w2v_ref, ok_ref, ov_ref):
    half = CMP_STRIDE * HEAD_DIM

    def one(r_ref, pe_ref, w1_ref, w2_ref, o_ref):
        r = r_ref[0]
        n_rows = r.shape[0]
        lo = _dot((r + pe_ref[:, :half]).astype(BF16), w1_ref[:half, :])
        hi = _dot((r + pe_ref[:, half:]).astype(BF16), w1_ref[half:, :])
        hid = _gelu_tanh(lo + pltpu.roll(hi, n_rows - 1, 0))
        o_ref[0] = _dot(hid.astype(BF16), w2_ref[...])

    one(rk_ref, pek_ref, w1k_ref, w2k_ref, ok_ref)
    one(rv_ref, pev_ref, w1v_ref, w2v_ref, ov_ref)


def _nsa_compress(rk, rv, pek, pev, w1k, w2k, w1v, w2v):
    bg, rows, width = rk.shape
    tok = pl.BlockSpec((1, rows, width), lambda i: (i, 0, 0))
    full = lambda a: pl.BlockSpec(a.shape, lambda i: (0,) * a.ndim)
    out = pl.BlockSpec((1, rows, HEAD_DIM), lambda i: (i, 0, 0))
    return pl.pallas_call(
        _compress_kernel,
        grid=(bg,),
        in_specs=[tok, tok, full(pek), full(pev), full(w1k), full(w2k), full(w1v), full(w2v)],
        out_specs=[out, out],
        out_shape=[jax.ShapeDtypeStruct((bg, rows, HEAD_DIM), F32)] * 2,
        compiler_params=_cparams(("parallel",)),
    )(rk, rv, pek, pev, w1k, w2k, w1v, w2v)


def _split3_dot(a, b01):
    a1 = a.astype(BF16)
    r1 = a - a1.astype(F32)
    a2 = r1.astype(BF16)
    a3 = (r1 - a2.astype(F32)).astype(BF16)
    return _dot(a1, b01) + _dot(a2, b01) + _dot(a3, b01)


def _nsa_kernel(q_ref, kc_ref, vc_ref, ks_ref, kw_ref, vs_ref, vw_ref, sm_ref, o_ref, oslc_ref, *, tq, seq, wlen):
    q0 = pl.program_id(1) * tq
    n_c = kc_ref.shape[2]
    n_s = seq // SEL_BLOCK
    rows = NSA_REP * tq
    tcol = q0 + lax.broadcasted_iota(I32, (tq, 1), 0)
    tcol_r = jnp.concatenate([tcol] * NSA_REP, axis=0)

    cidx = lax.broadcasted_iota(I32, (rows, n_c), 1)
    vis_r = (cidx * CMP_STRIDE + (CMP_LEN - 1)) <= tcol_r
    any_vis_r = (tcol_r >= (CMP_LEN - 1)).astype(F32)
    oc = lax.broadcasted_iota(I32, (n_c, n_s), 0) * CMP_STRIDE
    ob = lax.broadcasted_iota(I32, (n_c, n_s), 1) * SEL_BLOCK
    overlap = ((oc < ob + SEL_BLOCK) & (oc + CMP_LEN > ob)).astype(BF16)
    jj = lax.broadcasted_iota(I32, (tq, n_s), 1)
    blk_t = tcol // SEL_BLOCK
    w0 = pl.multiple_of(jnp.maximum(q0 + tq - wlen, 0), 8)
    dist = tcol_r - (w0 + lax.broadcasted_iota(I32, (rows, wlen), 1))
    band_r = (dist >= 0) & (dist < WINDOW)
    gates = sm_ref[0]

    for g in range(NSA_KV_HEADS):
        ksl = slice(g * HEAD_DIM, (g + 1) * HEAD_DIM)
        qs = (jnp.concatenate(
            [q_ref[0, :, (g * NSA_REP + r) * HEAD_DIM:(g * NSA_REP + r + 1) * HEAD_DIM] for r in range(NSA_REP)],
            axis=0) * SCALE).astype(BF16)

        s_c = _nt_dot(qs, kc_ref[0, g].astype(BF16))
        p_c, l_c = _softmax_rows(jnp.where(vis_r, s_c, NEG))
        p_c = p_c / l_c * any_vis_r
        o_cmp = _dot(p_c.astype(BF16), vc_ref[0, g].astype(BF16))

        p_sum = p_c[0:tq]
        for r in range(1, NSA_REP):
            p_sum = p_sum + p_c[r * tq:(r + 1) * tq]
        imp = _split3_dot(p_sum, overlap)
        forced = (jj == 0) | (jj == blk_t) | (jj == blk_t - 1)
        imp = jnp.where(jj <= blk_t, jnp.where(forced, FORCE, imp), -1.0)
        rank = jnp.zeros((tq, n_s), F32)
        for j2 in range(n_s):
            col = imp[:, j2:j2 + 1]
            beats = (col > imp) | ((col == imp) & (j2 < jj))
            rank = rank + beats.astype(F32)
        sel = (rank < float(min(SEL_TOPN, n_s))) & (jj <= blk_t)
        sel_bias = jnp.concatenate([jnp.where(sel, 0.0, NEG), jnp.zeros((tq, LANES - HEAD_DIM - n_s), F32)], axis=1)
        qs_aug = jnp.concatenate([qs, jnp.concatenate([sel_bias] * NSA_REP, axis=0).astype(BF16)], axis=1)

        def slc_body(klen, qs_aug=qs_aug, ksl=ksl):
            block_onehot = (lax.broadcasted_iota(I32, (klen, LANES - HEAD_DIM), 0) // SEL_BLOCK
                            == lax.broadcasted_iota(I32, (klen, LANES - HEAD_DIM), 1)).astype(BF16)
            k_aug = jnp.concatenate([ks_ref[0, :klen, ksl].astype(BF16), block_onehot], axis=1)
            causal_r = lax.broadcasted_iota(I32, (rows, klen), 1) <= tcol_r
            p_s, l_s = _softmax_rows(jnp.where(causal_r, _nt_dot(qs_aug, k_aug), NEG))
            oslc_ref[...] = _dot(p_s.astype(BF16), vs_ref[0, :klen, ksl].astype(BF16)) / l_s

        _for_key_extent(q0 + tq - 1, seq, slc_body)
        o_slc = oslc_ref[...]

        kw = kw_ref[0, pl.ds(w0, wlen), ksl].astype(BF16)
        vw = vw_ref[0, pl.ds(w0, wlen), ksl].astype(BF16)
        p_w, l_w = _softmax_rows(jnp.where(band_r, _nt_dot(qs, kw), NEG))
        o_win = _dot(p_w.astype(BF16), vw) / l_w

        for r in range(NSA_REP):
            head = g * NSA_REP + r
            rs = slice(r * tq, (r + 1) * tq)

            def gate(branch, head=head):
                col = SMALL_G + branch * NSA_HEADS + head
                return 1.0 / (1.0 + jnp.exp(-gates[:, col:col + 1]))

            out = gate(0) * o_cmp[rs] + gate(1) * o_slc[rs] + gate(2) * o_win[rs]
            o_ref[0, :, head * HEAD_DIM:(head + 1) * HEAD_DIM] = out.astype(o_ref.dtype)


def _nsa_attention(hr3, hp3, k_cmp, v_cmp):
    b, seq, _ = hr3.shape
    tq = min(128, seq)
    wlen = min(WINDOW + tq, seq)
    qw = NSA_HEADS * HEAD_DIM
    kvw = NSA_KV_HEADS * HEAD_DIM
    n_c = k_cmp.shape[2]
    cmp_spec = pl.BlockSpec((1, NSA_KV_HEADS, n_c, HEAD_DIM), lambda i, j: (i, 0, 0, 0))

    def seq_spec(col):
        return pl.BlockSpec((1, seq, kvw), lambda i, j: (i, 0, col // kvw))

    return pl.pallas_call(
        functools.partial(_nsa_kernel, tq=tq, seq=seq, wlen=wlen),
        grid=(b, seq // tq),
        in_specs=[pl.BlockSpec((1, tq, qw), lambda i, j: (i, j, ROPE_POS['nsa_q'] // qw)),
                  cmp_spec, cmp_spec,
                  seq_spec(ROPE_POS['nsa_ks']), seq_spec(ROPE_POS['nsa_kw']),
                  seq_spec(PLAIN_POS['nsa_vs']), seq_spec(PLAIN_POS['nsa_vw']),
                  pl.BlockSpec((1, tq, LANES), lambda i, j: (i, j, SMALL_COL // LANES))],
        out_specs=pl.BlockSpec((1, tq, qw), lambda i, j: (i, j, 0)),
        out_shape=jax.ShapeDtypeStruct((b, seq, qw), BF16),
        scratch_shapes=[pltpu.VMEM((NSA_REP * tq, HEAD_DIM), F32)],
        compiler_params=_cparams(("parallel", "arbitrary")),
    )(hr3, k_cmp, v_cmp, hr3, hr3, hp3, hp3, hp3)


def _row_count(mask):
    return jnp.sum(mask.astype(F32), axis=-1, keepdims=True)


def _dsa_kernel(q_ref, kk_ref, v_ref, iq_ref, sm_ref, o_ref, key_ref, *, tq, seq, topk):
    q0 = pl.program_id(1) * tq
    tcol = q0 + lax.broadcasted_iota(I32, (tq, 1), 0)
    w_idx = sm_ref[0, :, SMALL_W:SMALL_W + IDX_HEADS] * (IDX_HEADS ** -0.5) * (IDX_DIM ** -0.5)
    kf = float(topk)

    def body(klen):
        kpos = lax.broadcasted_iota(I32, (tq, klen), 1)
        causal = kpos <= tcol
        ik = kk_ref[0, :klen, HEAD_DIM:2 * HEAD_DIM].astype(BF16)
        score = jnp.zeros((tq, klen), F32)
        for h in range(IDX_HEADS):
            d = _nt_dot(iq_ref[0, :, h * IDX_DIM:(h + 1) * IDX_DIM].astype(BF16), ik)
            score = score + w_idx[:, h:h + 1] * jnp.maximum(d, 0.0)
        score = jnp.where(causal, score, NEG)

        bits = lax.bitcast_convert_type(score, I32)
        key_ref[:, :klen] = jnp.where(bits < 0, bits ^ 0x7FFFFFFF, bits)

        def tau_step(i, tau):
            cand = tau + jnp.left_shift(jnp.int32(1), 31 - i)
            cnt = _row_count(key_ref[:, :klen] >= cand)
            return jnp.where(cnt >= kf, cand, tau)

        tau = lax.fori_loop(0, 32, tau_step, jnp.full((tq, 1), -2 ** 31, I32))
        need = kf - _row_count(key_ref[:, :klen] > tau)

        n_bits = (klen - 1).bit_length()

        def cut_step(i, cut):
            cand = cut + jnp.left_shift(jnp.int32(1), n_bits - 1 - i)
            idx = lax.broadcasted_iota(I32, (tq, klen), 1)
            cnt = _row_count((key_ref[:, :klen] == tau) & (idx < cand))
            return jnp.where(cnt < need, cand, cut)

        cut = lax.fori_loop(0, n_bits, cut_step, jnp.zeros((tq, 1), I32))
        key = key_ref[:, :klen]
        chosen = ((key > tau) | ((key == tau) & (kpos <= cut))) & causal

        k = kk_ref[0, :klen, 0:HEAD_DIM].astype(BF16)
        v = v_ref[0, :klen, 0:HEAD_DIM].astype(BF16)
        for h in range(DSA_HEADS):
            sl = slice(h * HEAD_DIM, (h + 1) * HEAD_DIM)
            s = _nt_dot((q_ref[0, :, sl] * SCALE).astype(BF16), k)
            p, l = _softmax_rows(jnp.where(chosen, s, NEG))
            o_ref[0, :, sl] = (_dot(p.astype(BF16), v) / l).astype(o_ref.dtype)

    _for_key_extent(q0 + tq - 1, seq, body)


def _dsa_attention(hr3, hp3):
    b, seq, _ = hr3.shape
    tq = min(512, seq)
    topk = min(DSA_TOPK, seq // 4)
    qw = DSA_HEADS * HEAD_DIM
    iqw = IDX_HEADS * IDX_DIM
    return pl.pallas_call(
        functools.partial(_dsa_kernel, tq=tq, seq=seq, topk=topk),
        grid=(b, seq // tq),
        in_specs=[pl.BlockSpec((1, tq, qw), lambda i, j: (i, j, ROPE_POS['dsa_q'] // qw)),
                  pl.BlockSpec((1, seq, LANES), lambda i, j: (i, 0, ROPE_POS['dsa_k'] // LANES)),
                  pl.BlockSpec((1, seq, LANES), lambda i, j: (i, 0, PLAIN_POS['dsa_v'] // LANES)),
                  pl.BlockSpec((1, tq, iqw), lambda i, j: (i, j, ROPE_POS['idx_q'] // iqw)),
                  pl.BlockSpec((1, tq, LANES), lambda i, j: (i, j, SMALL_COL // LANES))],
        out_specs=pl.BlockSpec((1, tq, qw), lambda i, j: (i, j, 0)),
        out_shape=jax.ShapeDtypeStruct((b, seq, qw), BF16),
        scratch_shapes=[pltpu.VMEM((tq, seq), I32)],
        compiler_params=_cparams(("parallel", "arbitrary")),
    )(hr3, hr3, hp3, hr3, hp3)


def _outproj_kernel(of_ref, on_ref, od_ref, x_ref, wf_ref, wn_ref, wd_ref, g_ref, b_ref, o_ref, *, alpha):
    mix = _dot(of_ref[...], wf_ref[...]) + _dot(on_ref[...], wn_ref[...]) + _dot(od_ref[...], wd_ref[...])
    o_ref[...] = _layer_norm(alpha * x_ref[...] + mix, g_ref[...], b_ref[...])


def _outproj_ln(o_fox, o_nsa, o_dsa, x2d, w_f, w_n, w_d, g, b, alpha):
    n = x2d.shape[0]
    tm = min(512, n)
    row = lambda a: pl.BlockSpec((tm, a.shape[1]), lambda i: (i, 0))
    full = lambda a: pl.BlockSpec(a.shape, lambda i: (0, 0))
    return pl.pallas_call(
        functools.partial(_outproj_kernel, alpha=alpha),
        grid=(n // tm,),
        in_specs=[row(o_fox), row(o_nsa), row(o_dsa), row(x2d), full(w_f), full(w_n), full(w_d), full(g), full(b)],
        out_specs=pl.BlockSpec((tm, D_MODEL), lambda i: (i, 0)),
        out_shape=jax.ShapeDtypeStruct((n, D_MODEL), F32),
        compiler_params=_cparams(("parallel",)),
    )(o_fox, o_nsa, o_dsa, x2d, w_f, w_n, w_d, g, b)


def _router_kernel(x_ref, wr_ref, br_ref, g_ref):
    tn = x_ref.shape[0]
    logits = _nt_dot(wr_ref[...], x_ref[...].astype(BF16))
    s = 1.0 / (1.0 + jnp.exp(-logits))
    sb = s + br_ref[...]
    low = jnp.float32(-3e38)

    grp = []
    for gi in range(N_GROUPS):
        blk = sb[gi * GROUP_SIZE:(gi + 1) * GROUP_SIZE]
        m1 = jnp.max(blk, axis=0, keepdims=True)
        is_max = blk == m1
        n_max = jnp.sum(is_max.astype(F32), axis=0, keepdims=True)
        m2 = jnp.max(jnp.where(is_max, low, blk), axis=0, keepdims=True)
        grp.append(m1 + jnp.where(n_max >= 2.0, m1, m2))
    masked = []
    for gi in range(N_GROUPS):
        rank = jnp.zeros((1, tn), F32)
        for g2 in range(N_GROUPS):
            if g2 == gi:
                continue
            beats = (grp[g2] > grp[gi]) | ((grp[g2] == grp[gi]) if g2 < gi else False)
            rank = rank + beats.astype(F32)
        keep = rank < float(TOPK_GROUPS)
        masked.append(jnp.where(keep, sb[gi * GROUP_SIZE:(gi + 1) * GROUP_SIZE], NEG))
    masked = jnp.concatenate(masked, axis=0)

    eidx = lax.broadcasted_iota(I32, (N_EXPERTS, tn), 0)
    rank = jnp.zeros((N_EXPERTS, tn), F32)
    for e2 in range(N_EXPERTS):
        row = masked[e2:e2 + 1]
        beats = (row > masked) | ((row == masked) & (e2 < eidx))
        rank = rank + beats.astype(F32)
    gw = jnp.where(rank < float(MOE_TOPK), s, 0.0)
    g_ref[...] = gw / jnp.sum(gw, axis=0, keepdims=True) * ROUTED_SCALE


def _router(x2d, wr_t, br_col):
    n = x2d.shape[0]
    tn = min(512, n)
    return pl.pallas_call(
        _router_kernel,
        grid=(n // tn,),
        in_specs=[pl.BlockSpec((tn, D_MODEL), lambda i: (i, 0)),
                  pl.BlockSpec(wr_t.shape, lambda i: (0, 0)),
                  pl.BlockSpec(br_col.shape, lambda i: (0, 0))],
        out_specs=pl.BlockSpec((N_EXPERTS, tn), lambda i: (0, i)),
        out_shape=jax.ShapeDtypeStruct((N_EXPERTS, n), F32),
        compiler_params=_cparams(("parallel",)),
    )(x2d, wr_t, br_col)


def _silu(x):
    return x / (1.0 + jnp.exp(-x))


def _moe_kernel(x_ref, gt_ref, wg_ref, wu_ref, wd_ref, sg_ref, su_ref, sd_ref, lg_ref, lb_ref, o_ref,
                xb_ref, acc_ref, *, alpha):
    e = pl.program_id(1)

    @pl.when(e == 0)
    def _():
        xb = x_ref[...].astype(BF16)
        xb_ref[...] = xb
        h = _silu(_dot(xb, sg_ref[...])) * _dot(xb, su_ref[...])
        acc_ref[...] = _dot(h.astype(BF16), sd_ref[...])

    xb = xb_ref[...]
    h = (_silu(_dot(xb, wg_ref[0])) * _dot(xb, wu_ref[0])).astype(BF16)
    gates = gt_ref[...]
    lane = lax.broadcasted_iota(I32, gates.shape, 1)
    gcol = jnp.sum(jnp.where(lane == e, gates, 0.0), axis=1, keepdims=True)
    chunk = 256
    for c in range(D_MODEL // chunk):
        cs = slice(c * chunk, (c + 1) * chunk)
        acc_ref[:, cs] += gcol * _dot(h, wd_ref[0, :, cs])

    @pl.when(e == pl.num_programs(1) - 1)
    def _():
        o_ref[...] = _layer_norm(alpha * x_ref[...] + acc_ref[...], lg_ref[...], lb_ref[...])


def _moe_ln(x2d, gates, w_gate, w_up, w_down, ws_gate, ws_up, ws_down, g, b, alpha):
    n = x2d.shape[0]
    tn = min(1024, n)
    full = lambda a: pl.BlockSpec(a.shape, lambda i, e: (0,) * a.ndim)
    return pl.pallas_call(
        functools.partial(_moe_kernel, alpha=alpha),
        grid=(n // tn, N_EXPERTS),
        in_specs=[pl.BlockSpec((tn, D_MODEL), lambda i, e: (i, 0)),
                  pl.BlockSpec((tn, N_EXPERTS), lambda i, e: (i, 0)),
                  pl.BlockSpec((1, D_MODEL, EXPERT_DIM), lambda i, e: (e, 0, 0)),
                  pl.BlockSpec((1, D_MODEL, EXPERT_DIM), lambda i, e: (e, 0, 0)),
                  pl.BlockSpec((1, EXPERT_DIM, D_MODEL), lambda i, e: (e, 0, 0)),
                  full(ws_gate), full(ws_up), full(ws_down), full(g), full(b)],
        out_specs=pl.BlockSpec((tn, D_MODEL), lambda i, e: (i, 0)),
        out_shape=jax.ShapeDtypeStruct((n, D_MODEL), F32),
        scratch_shapes=[pltpu.VMEM((tn, D_MODEL), BF16), pltpu.VMEM((tn, D_MODEL), F32)],
        compiler_params=_cparams(("parallel", "arbitrary")),
    )(x2d, gates, w_gate, w_up, w_down, ws_gate, ws_up, ws_down, g, b)


def _gather_cols(w, order):
    parts = []
    for item in order:
        if isinstance(item, tuple):
            parts.append(jnp.zeros(w.shape[:-1] + (item[1],), w.dtype))
        else:
            off, width = SEG_OFF[item]
            parts.append(w[..., off:off + width])
    return jnp.concatenate(parts, axis=-1)


def _rotate_half_cols(w):
    lead = w.shape[:-1]
    w4 = w.reshape(lead + (-1, 2, HEAD_DIM // 2))
    return jnp.concatenate([-w4[..., 1:2, :], w4[..., 0:1, :]], axis=-2).reshape(w.shape)


def _rope_tables(seq, width):
    half = HEAD_DIM // 2
    inv_freq = ROPE_THETA ** (-jnp.arange(half, dtype=F32) * (2.0 / HEAD_DIM))
    ang = jnp.arange(seq).astype(F32)[:, None] * inv_freq[None, :]
    reps = width // half
    return jnp.tile(jnp.cos(ang), (1, reps)), jnp.tile(jnp.sin(ang), (1, reps))


def _token_rows(tok, b, seq):
    t4 = tok.reshape(b, seq, NSA_KV_HEADS, HEAD_DIM).transpose(0, 2, 1, 3)
    return t4.reshape(b * NSA_KV_HEADS, seq // CMP_STRIDE, CMP_STRIDE * HEAD_DIM)


def kernel(x, w_in, fox_forget_bias, cmp_pos_k, cmp_w1_k, cmp_w2_k, cmp_pos_v, cmp_w1_v, cmp_w2_v, w_out,
           ln1_g, ln1_b, w_router, b_router, w_gate, w_up, w_down, ws_gate, ws_up, ws_down, ln2_g, ln2_b):
    b, seq, dm = x.shape
    depth = w_in.shape[0]
    n = b * seq
    alpha = float((2 * depth) ** 0.25)

    w_plain = _gather_cols(w_in, PLAIN_ORDER).astype(BF16)
    w_rope = _gather_cols(w_in, ROPE_ORDER)
    w_rope_a = w_rope.astype(BF16)
    w_rope_b = _rotate_half_cols(w_rope).astype(BF16)
    cos_t, sin_t = _rope_tables(seq, 256)
    fw, nw = FOX_HEADS * HEAD_DIM, NSA_HEADS * HEAD_DIM
    w_out_b = w_out.astype(BF16)

    x2d = x.reshape(n, dm)
    for l in range(depth):
        hp, hr = _project(x2d, w_plain[l], w_rope_a[l], w_rope_b[l], cos_t, sin_t, seq)
        hp3 = hp.reshape(b, seq, PLAIN_W)
        hr3 = hr.reshape(b, seq, ROPE_W)

        f_logit = hp3[:, :, SMALL_COL + SMALL_F:SMALL_COL + SMALL_F + FOX_HEADS]
        c_rows = _fox_cumsum(f_logit.transpose(0, 2, 1), fox_forget_bias[l].reshape(FOX_HEADS, 1))
        o_fox = _fox_attention(hp3, c_rows.transpose(0, 2, 1), c_rows)

        kvw = NSA_KV_HEADS * HEAD_DIM
        rk = _token_rows(hr3[:, :, ROPE_POS['nsa_kc']:ROPE_POS['nsa_kc'] + kvw], b, seq)
        rv = _token_rows(hp3[:, :, PLAIN_POS['nsa_vc']:PLAIN_POS['nsa_vc'] + kvw], b, seq)
        k_cmp, v_cmp = _nsa_compress(
            rk, rv, cmp_pos_k[l].reshape(1, -1), cmp_pos_v[l].reshape(1, -1),
            cmp_w1_k[l].astype(BF16), cmp_w2_k[l].astype(BF16), cmp_w1_v[l].astype(BF16), cmp_w2_v[l].astype(BF16))
        n_rows = seq // CMP_STRIDE
        o_nsa = _nsa_attention(hr3, hp3, k_cmp.reshape(b, NSA_KV_HEADS, n_rows, HEAD_DIM),
                               v_cmp.reshape(b, NSA_KV_HEADS, n_rows, HEAD_DIM))

        o_dsa = _dsa_attention(hr3, hp3)

        x2d = _outproj_ln(o_fox.reshape(n, fw), o_nsa.reshape(n, nw), o_dsa.reshape(n, nw), x2d,
                          w_out_b[l, :fw], w_out_b[l, fw:fw + nw], w_out_b[l, fw + nw:],
                          ln1_g[l].reshape(1, dm), ln1_b[l].reshape(1, dm), alpha)

        gates_t = _router(x2d, w_router[l].T.astype(BF16), b_router[l].reshape(N_EXPERTS, 1))
        x2d = _moe_ln(x2d, gates_t.T, w_gate[l].astype(BF16), w_up[l].astype(BF16), w_down[l].astype(BF16),
                      ws_gate[l].astype(BF16), ws_up[l].astype(BF16), ws_down[l].astype(BF16),
                      ln2_g[l].reshape(1, dm), ln2_b[l].reshape(1, dm), alpha)
    return x2d.reshape(b, seq, dm)
```

```python
import functools

import numpy as np
import jax
import jax.numpy as jnp
from jax import lax
from jax.experimental import pallas as pl
from jax.experimental.pallas import tpu as pltpu

D_MODEL = 1024
HEAD_DIM = 64
FOX_HEADS = 4
NSA_HEADS = 6
NSA_KV_HEADS = 2
NSA_REP = NSA_HEADS // NSA_KV_HEADS
DSA_HEADS = 6
ROPE_THETA = 10000.0
CMP_LEN = 32
CMP_STRIDE = 16
CMP_HIDDEN = 2 * HEAD_DIM
SEL_BLOCK = 64
SEL_TOPN = 16
WINDOW = 512
IDX_HEADS = 4
IDX_DIM = 64
DSA_TOPK = 256
N_EXPERTS = 64
N_GROUPS = 8
GROUP_SIZE = N_EXPERTS // N_GROUPS
TOPK_GROUPS = 4
MOE_TOPK = 8
EXPERT_DIM = 256
SHARED_DIM = 256
ROUTED_SCALE = 2.5
LN_EPS = 1e-5
NEG = -1e30
FORCE = 1e6
SCALE = HEAD_DIM ** -0.5

F32 = jnp.float32
BF16 = jnp.bfloat16
I32 = jnp.int32

VMEM_LIMIT_BYTES = 52 * 1024 * 1024
LANES = 128
KEY_STEP = 512

SEGMENTS = (
    ('fox_q', FOX_HEADS * HEAD_DIM), ('fox_k', FOX_HEADS * HEAD_DIM),
    ('fox_v', FOX_HEADS * HEAD_DIM), ('fox_f', FOX_HEADS),
    ('nsa_q', NSA_HEADS * HEAD_DIM),
    ('nsa_kc', NSA_KV_HEADS * HEAD_DIM), ('nsa_vc', NSA_KV_HEADS * HEAD_DIM),
    ('nsa_ks', NSA_KV_HEADS * HEAD_DIM), ('nsa_vs', NSA_KV_HEADS * HEAD_DIM),
    ('nsa_kw', NSA_KV_HEADS * HEAD_DIM), ('nsa_vw', NSA_KV_HEADS * HEAD_DIM),
    ('nsa_g', 3 * NSA_HEADS),
    ('dsa_q', DSA_HEADS * HEAD_DIM), ('dsa_k', HEAD_DIM), ('dsa_v', HEAD_DIM),
    ('idx_q', IDX_HEADS * IDX_DIM), ('idx_k', IDX_DIM), ('idx_w', IDX_HEADS),
)
SEG_OFF = {}
_off = 0
for _name, _width in SEGMENTS:
    SEG_OFF[_name] = (_off, _width)
    _off += _width
N_IN = _off

PLAIN_ORDER = ('fox_q', 'fox_k', 'fox_v', 'nsa_vc', 'nsa_vs', 'nsa_vw', 'dsa_v', (None, 64),
               'fox_f', 'nsa_g', 'idx_w', (None, 2 * LANES - FOX_HEADS - 3 * NSA_HEADS - IDX_HEADS))
ROPE_ORDER = ('nsa_q', 'nsa_kc', 'nsa_ks', 'nsa_kw', 'dsa_q', 'dsa_k', 'idx_k', 'idx_q')


def _layout(order):
    pos, off = {}, 0
    for item in order:
        if isinstance(item, tuple):
            off += item[1]
        else:
            pos[item] = off
            off += SEG_OFF[item][1]
    return pos, off


PLAIN_POS, PLAIN_W = _layout(PLAIN_ORDER)
ROPE_POS, ROPE_W = _layout(ROPE_ORDER)
SMALL_COL = PLAIN_POS['fox_f']
SMALL_F = 0
SMALL_G = FOX_HEADS
SMALL_W = FOX_HEADS + 3 * NSA_HEADS


def _cparams(sem):
    return pltpu.CompilerParams(dimension_semantics=sem, vmem_limit_bytes=VMEM_LIMIT_BYTES)


def _nt_dot(a, b):
    return lax.dot_general(a, b, (((1,), (1,)), ((), ())), preferred_element_type=F32)


def _dot(a, b):
    return jnp.dot(a, b, preferred_element_type=F32)


def _softmax_rows(s):
    m = jnp.max(s, axis=-1, keepdims=True)
    p = jnp.exp(s - m)
    return p, jnp.sum(p, axis=-1, keepdims=True)


def _for_key_extent(q_last, seq, body, step=KEY_STEP):
    n = seq // step
    if n <= 1:
        body(seq)
        return
    c = q_last // step
    for i in range(n):
        pl.when(c == i)(functools.partial(body, (i + 1) * step))


def _layer_norm(z, g, b):
    mu = jnp.mean(z, axis=-1, keepdims=True)
    zc = z - mu
    var = jnp.mean(zc * zc, axis=-1, keepdims=True)
    return zc * lax.rsqrt(var + LN_EPS) * g + b


def _proj_kernel(x_ref, w_ref, o_ref, xb_ref):
    @pl.when(pl.program_id(1) == 0)
    def _():
        xb_ref[...] = x_ref[...].astype(BF16)

    o_ref[...] = _dot(xb_ref[...], w_ref[...])


def _proj_rope_kernel(x_ref, wa_ref, wb_ref, cos_ref, sin_ref, o_ref, xb_ref):
    @pl.when(pl.program_id(1) == 0)
    def _():
        xb_ref[...] = x_ref[...].astype(BF16)

    xb = xb_ref[...]
    o_ref[...] = _dot(xb, wa_ref[...]) * cos_ref[...] + _dot(xb, wb_ref[...]) * sin_ref[...]


def _project(x2d, w_plain, w_a, w_b, cos_t, sin_t, seq):
    n = x2d.shape[0]
    tm = min(1024, seq)
    tc = 256
    plain = pl.pallas_call(
        _proj_kernel,
        grid=(n // tm, PLAIN_W // tc),
        in_specs=[pl.BlockSpec((tm, D_MODEL), lambda i, j: (i, 0)),
                  pl.BlockSpec((D_MODEL, tc), lambda i, j: (0, j))],
        out_specs=pl.BlockSpec((tm, tc), lambda i, j: (i, j)),
        out_shape=jax.ShapeDtypeStruct((n, PLAIN_W), F32),
        scratch_shapes=[pltpu.VMEM((tm, D_MODEL), BF16)],
        compiler_params=_cparams(("parallel", "arbitrary")),
    )(x2d, w_plain)
    nt = seq // tm
    roped = pl.pallas_call(
        _proj_rope_kernel,
        grid=(n // tm, ROPE_W // tc),
        in_specs=[pl.BlockSpec((tm, D_MODEL), lambda i, j: (i, 0)),
                  pl.BlockSpec((D_MODEL, tc), lambda i, j: (0, j)),
                  pl.BlockSpec((D_MODEL, tc), lambda i, j: (0, j)),
                  pl.BlockSpec((tm, tc), lambda i, j: (i % nt, 0)),
                  pl.BlockSpec((tm, tc), lambda i, j: (i % nt, 0))],
        out_specs=pl.BlockSpec((tm, tc), lambda i, j: (i, j)),
        out_shape=jax.ShapeDtypeStruct((n, ROPE_W), F32),
        scratch_shapes=[pltpu.VMEM((tm, D_MODEL), BF16)],
        compiler_params=_cparams(("parallel", "arbitrary")),
    )(x2d, w_a, w_b, cos_t, sin_t)
    return plain, roped


def _fox_cum_kernel(f_ref, fb_ref, o_ref):
    x = f_ref[0] + fb_ref[...]
    c = jnp.minimum(x, 0.0) - jnp.log1p(jnp.exp(-jnp.abs(x)))
    seq = c.shape[-1]
    lane = lax.broadcasted_iota(I32, c.shape, 1)
    sh = 1
    while sh < seq:
        c = c + jnp.where(lane >= sh, pltpu.roll(c, sh, 1), 0.0)
        sh *= 2
    o_ref[0] = c


def _fox_cumsum(f_rows, f_bias):
    b, h, seq = f_rows.shape
    return pl.pallas_call(
        _fox_cum_kernel,
        grid=(b,),
        in_specs=[pl.BlockSpec((1, h, seq), lambda i: (i, 0, 0)),
                  pl.BlockSpec((h, 1), lambda i: (0, 0))],
        out_specs=pl.BlockSpec((1, h, seq), lambda i: (i, 0, 0)),
        out_shape=jax.ShapeDtypeStruct((b, h, seq), F32),
        compiler_params=_cparams(("parallel",)),
    )(f_rows, f_bias)


def _fox_kernel(q_ref, k_ref, v_ref, cc_ref, cr_ref, o_ref, *, tq, seq):
    q0 = pl.program_id(1) * tq

    def body(klen):
        qpos = q0 + lax.broadcasted_iota(I32, (tq, klen), 0)
        causal = lax.broadcasted_iota(I32, (tq, klen), 1) <= qpos
        for h in range(FOX_HEADS):
            sl = slice(h * HEAD_DIM, (h + 1) * HEAD_DIM)
            qh = (q_ref[0, :, sl] * SCALE).astype(BF16)
            kh = k_ref[0, :klen, sl].astype(BF16)
            vh = v_ref[0, :klen, sl].astype(BF16)
            s = _nt_dot(qh, kh) + (cc_ref[0, :, h:h + 1] - cr_ref[0, h:h + 1, :klen])
            p, l = _softmax_rows(jnp.where(causal, s, NEG))
            o_ref[0, :, sl] = (_dot(p.astype(BF16), vh) / l).astype(o_ref.dtype)

    _for_key_extent(q0 + tq - 1, seq, body, step=tq)


def _fox_attention(hp3, c_cols, c_rows):
    b, seq, _ = hp3.shape
    tq = min(256, seq)
    w = FOX_HEADS * HEAD_DIM
    return pl.pallas_call(
        functools.partial(_fox_kernel, tq=tq, seq=seq),
        grid=(b, seq // tq),
        in_specs=[pl.BlockSpec((1, tq, w), lambda i, j: (i, j, PLAIN_POS['fox_q'] // w)),
                  pl.BlockSpec((1, seq, w), lambda i, j: (i, 0, PLAIN_POS['fox_k'] // w)),
                  pl.BlockSpec((1, seq, w), lambda i, j: (i, 0, PLAIN_POS['fox_v'] // w)),
                  pl.BlockSpec((1, tq, FOX_HEADS), lambda i, j: (i, j, 0)),
                  pl.BlockSpec((1, FOX_HEADS, seq), lambda i, j: (i, 0, 0))],
        out_specs=pl.BlockSpec((1, tq, w), lambda i, j: (i, j, 0)),
        out_shape=jax.ShapeDtypeStruct((b, seq, w), BF16),
        compiler_params=_cparams(("parallel", "arbitrary")),
    )(hp3, hp3, hp3, c_cols, c_rows)


def _gelu_tanh(x):
    return 0.5 * x * (1.0 + jnp.tanh(np.float32(np.sqrt(2.0 / np.pi)) * (x + 0.044715 * (x * x * x))))


def _compress_kernel(rk_ref, rv_ref, pek_ref, pev_ref, w1k_ref, w2k_ref, w1v_ref, w2v_ref, ok_ref, ov_ref):
    half = CMP_STRIDE * HEAD_DIM

    def one(r_ref, pe_ref, w1_ref, w2_ref, o_ref):
        r = r_ref[0]
        n_rows = r.shape[0]
        lo = _dot((r + pe_ref[:, :half]).astype(BF16), w1_ref[:half, :])
        hi = _dot((r + pe_ref[:, half:]).astype(BF16), w1_ref[half:, :])
        hid = _gelu_tanh(lo + pltpu.roll(hi, n_rows - 1, 0))
        o_ref[0] = _dot(hid.astype(BF16), w2_ref[...])

    one(rk_ref, pek_ref, w1k_ref, w2k_ref, ok_ref)
    one(rv_ref, pev_ref, w1v_ref, w2v_ref, ov_ref)


def _nsa_compress(rk, rv, pek, pev, w1k, w2k, w1v, w2v):
    bg, rows, width = rk.shape
    tok = pl.BlockSpec((1, rows, width), lambda i: (i, 0, 0))
    full = lambda a: pl.BlockSpec(a.shape, lambda i: (0,) * a.ndim)
    out = pl.BlockSpec((1, rows, HEAD_DIM), lambda i: (i, 0, 0))
    return pl.pallas_call(
        _compress_kernel,
        grid=(bg,),
        in_specs=[tok, tok, full(pek), full(pev), full(w1k), full(w2k), full(w1v), full(w2v)],
        out_specs=[out, out],
        out_shape=[jax.ShapeDtypeStruct((bg, rows, HEAD_DIM), F32)] * 2,
        compiler_params=_cparams(("parallel",)),
    )(rk, rv, pek, pev, w1k, w2k, w1v, w2v)


def _split3_nt_dot(b01, a):
    a1 = a.astype(BF16)
    r1 = a - a1.astype(F32)
    a2 = r1.astype(BF16)
    a3 = (r1 - a2.astype(F32)).astype(BF16)
    return _nt_dot(b01, a1) + _nt_dot(b01, a2) + _nt_dot(b01, a3)


def _nsa_kernel(q_ref, kc_ref, vc_ref, ks_ref, kw_ref, vs_ref, vw_ref, sm_ref, o_ref, oslc_ref, *, tq, seq, wlen):
    q0 = pl.program_id(1) * tq
    n_c = kc_ref.shape[2]
    n_s = seq // SEL_BLOCK
    rows = NSA_REP * tq
    tcol = q0 + lax.broadcasted_iota(I32, (tq, 1), 0)
    tcol_r = jnp.concatenate([tcol] * NSA_REP, axis=0)

    cidx = lax.broadcasted_iota(I32, (rows, n_c), 1)
    vis_r = (cidx * CMP_STRIDE + (CMP_LEN - 1)) <= tcol_r
    any_vis_r = (tcol_r >= (CMP_LEN - 1)).astype(F32)
    oc = lax.broadcasted_iota(I32, (n_s, n_c), 1) * CMP_STRIDE
    ob = lax.broadcasted_iota(I32, (n_s, n_c), 0) * SEL_BLOCK
    overlap_t = ((oc < ob + SEL_BLOCK) & (oc + CMP_LEN > ob)).astype(BF16)
    jj_t = lax.broadcasted_iota(I32, (n_s, tq), 0)
    blk_t = (q0 + lax.broadcasted_iota(I32, (1, tq), 1)) // SEL_BLOCK
    w0 = pl.multiple_of(jnp.maximum(q0 + tq - wlen, 0), 8)
    dist = tcol_r - (w0 + lax.broadcasted_iota(I32, (rows, wlen), 1))
    band_r = (dist >= 0) & (dist < WINDOW)
    gates = sm_ref[0]

    for g in range(NSA_KV_HEADS):
        ksl = slice(g * HEAD_DIM, (g + 1) * HEAD_DIM)
        qs = (jnp.concatenate(
            [q_ref[0, :, (g * NSA_REP + r) * HEAD_DIM:(g * NSA_REP + r + 1) * HEAD_DIM] for r in range(NSA_REP)],
            axis=0) * SCALE).astype(BF16)

        s_c = _nt_dot(qs, kc_ref[0, g].astype(BF16))
        p_c, l_c = _softmax_rows(jnp.where(vis_r, s_c, NEG))
        p_c = p_c / l_c * any_vis_r
        o_cmp = _dot(p_c.astype(BF16), vc_ref[0, g].astype(BF16))

        p_sum = p_c[0:tq]
        for r in range(1, NSA_REP):
            p_sum = p_sum + p_c[r * tq:(r + 1) * tq]
        imp = _split3_nt_dot(overlap_t, p_sum)
        forced = (jj_t == 0) | (jj_t == blk_t) | (jj_t == blk_t - 1)
        imp = jnp.where(jj_t <= blk_t, jnp.where(forced, FORCE, imp), -1.0)
        rank = jnp.zeros((n_s, tq), F32)
        for j2 in range(n_s):
            row = imp[j2:j2 + 1, :]
            beats = (row > imp) | ((row == imp) & (j2 < jj_t))
            rank = rank + beats.astype(F32)
        sel = (rank < float(min(SEL_TOPN, n_s))) & (jj_t <= blk_t)
        bias_t = jnp.concatenate([jnp.where(sel, 0.0, NEG), jnp.zeros((tq - n_s, tq), F32)], axis=0)
        sel_bias = bias_t.T[:, :LANES - HEAD_DIM]
        qs_aug = jnp.concatenate([qs, jnp.concatenate([sel_bias] * NSA_REP, axis=0).astype(BF16)], axis=1)

        def slc_body(klen, qs_aug=qs_aug, ksl=ksl):
            block_onehot = (lax.broadcasted_iota(I32, (klen, LANES - HEAD_DIM), 0) // SEL_BLOCK
                            == lax.broadcasted_iota(I32, (klen, LANES - HEAD_DIM), 1)).astype(BF16)
            k_aug = jnp.concatenate([ks_ref[0, :klen, ksl].astype(BF16), block_onehot], axis=1)
            causal_r = lax.broadcasted_iota(I32, (rows, klen), 1) <= tcol_r
            p_s, l_s = _softmax_rows(jnp.where(causal_r, _nt_dot(qs_aug, k_aug), NEG))
            oslc_ref[...] = _dot(p_s.astype(BF16), vs_ref[0, :klen, ksl].astype(BF16)) / l_s

        _for_key_extent(q0 + tq - 1, seq, slc_body)
        o_slc = oslc_ref[...]

        kw = kw_ref[0, pl.ds(w0, wlen), ksl].astype(BF16)
        vw = vw_ref[0, pl.ds(w0, wlen), ksl].astype(BF16)
        p_w, l_w = _softmax_rows(jnp.where(band_r, _nt_dot(qs, kw), NEG))
        o_win = _dot(p_w.astype(BF16), vw) / l_w

        for r in range(NSA_REP):
            head = g * NSA_REP + r
            rs = slice(r * tq, (r + 1) * tq)

            def gate(branch, head=head):
                col = SMALL_G + branch * NSA_HEADS + head
                return 1.0 / (1.0 + jnp.exp(-gates[:, col:col + 1]))

            out = gate(0) * o_cmp[rs] + gate(1) * o_slc[rs] + gate(2) * o_win[rs]
            o_ref[0, :, head * HEAD_DIM:(head + 1) * HEAD_DIM] = out.astype(o_ref.dtype)


def _nsa_attention(hr3, hp3, k_cmp, v_cmp):
    b, seq, _ = hr3.shape
    tq = min(128, seq)
    wlen = min(WINDOW + tq, seq)
    qw = NSA_HEADS * HEAD_DIM
    kvw = NSA_KV_HEADS * HEAD_DIM
    n_c = k_cmp.shape[2]
    cmp_spec = pl.BlockSpec((1, NSA_KV_HEADS, n_c, HEAD_DIM), lambda i, j: (i, 0, 0, 0))

    def seq_spec(col):
        return pl.BlockSpec((1, seq, kvw), lambda i, j: (i, 0, col // kvw))

    return pl.pallas_call(
        functools.partial(_nsa_kernel, tq=tq, seq=seq, wlen=wlen),
        grid=(b, seq // tq),
        in_specs=[pl.BlockSpec((1, tq, qw), lambda i, j: (i, j, ROPE_POS['nsa_q'] // qw)),
                  cmp_spec, cmp_spec,
                  seq_spec(ROPE_POS['nsa_ks']), seq_spec(ROPE_POS['nsa_kw']),
                  seq_spec(PLAIN_POS['nsa_vs']), seq_spec(PLAIN_POS['nsa_vw']),
                  pl.BlockSpec((1, tq, LANES), lambda i, j: (i, j, SMALL_COL // LANES))],
        out_specs=pl.BlockSpec((1, tq, qw), lambda i, j: (i, j, 0)),
        out_shape=jax.ShapeDtypeStruct((b, seq, qw), BF16),
        scratch_shapes=[pltpu.VMEM((NSA_REP * tq, HEAD_DIM), F32)],
        compiler_params=_cparams(("parallel", "arbitrary")),
    )(hr3, k_cmp, v_cmp, hr3, hr3, hp3, hp3, hp3)


def _row_count(mask):
    return jnp.sum(mask.astype(F32), axis=-1, keepdims=True)


NEG_KEY = int(np.float32(NEG).view(np.int32)) ^ 0x7FFFFFFF


def _dsa_kernel(q_ref, kk_ref, v_ref, iq_ref, sm_ref, o_ref, key_ref, *, tq, seq, topk):
    q0 = pl.program_id(1) * tq
    tcol = q0 + lax.broadcasted_iota(I32, (tq, 1), 0)
    w_idx = sm_ref[0, :, SMALL_W:SMALL_W + IDX_HEADS] * (IDX_HEADS ** -0.5) * (IDX_DIM ** -0.5)
    kf = float(topk)

    def body(klen):
        kpos = lax.broadcasted_iota(I32, (tq, klen), 1)
        causal = kpos <= tcol
        ik = kk_ref[0, :klen, HEAD_DIM:2 * HEAD_DIM].astype(BF16)
        score = jnp.zeros((tq, klen), F32)
        for h in range(IDX_HEADS):
            d = _nt_dot(iq_ref[0, :, h * IDX_DIM:(h + 1) * IDX_DIM].astype(BF16), ik)
            score = score + w_idx[:, h:h + 1] * jnp.maximum(d, 0.0)
        score = jnp.where(causal, score, NEG)

        bits = lax.bitcast_convert_type(score, I32)
        key_ref[:, :klen] = jnp.where(bits < 0, bits ^ 0x7FFFFFFF, bits)

        def tau_step(i, tau):
            cand = tau + jnp.left_shift(jnp.int32(1), 31 - i)
            cnt = _row_count(key_ref[:, :klen] >= cand)
            return jnp.where(cnt >= kf, cand, tau)

        tau = lax.fori_loop(0, 32, tau_step, jnp.full((tq, 1), -2 ** 31, I32))
        key = key_ref[:, :klen]
        n_ge = _row_count(key >= tau)
        need = kf - _row_count(key > tau)

        n_bits = (klen - 1).bit_length()

        def cut_step(i, cut):
            cand = cut + jnp.left_shift(jnp.int32(1), n_bits - 1 - i)
            idx = lax.broadcasted_iota(I32, (tq, klen), 1)
            cnt = _row_count((key_ref[:, :klen] == tau) & (idx < cand))
            return jnp.where(cnt < need, cand, cut)

        surplus = jnp.max(jnp.where(tau > NEG_KEY, n_ge - kf, 0.0))
        cut = lax.cond(surplus > 0.0,
                       lambda: lax.fori_loop(0, n_bits, cut_step, jnp.zeros((tq, 1), I32)),
                       lambda: jnp.full((tq, 1), klen, I32))
        chosen = ((key > tau) | ((key == tau) & (kpos <= cut))) & causal

        k = kk_ref[0, :klen, 0:HEAD_DIM].astype(BF16)
        v = v_ref[0, :klen, 0:HEAD_DIM].astype(BF16)
        for h in range(DSA_HEADS):
            sl = slice(h * HEAD_DIM, (h + 1) * HEAD_DIM)
            s = _nt_dot((q_ref[0, :, sl] * SCALE).astype(BF16), k)
            p, l = _softmax_rows(jnp.where(chosen, s, NEG))
            o_ref[0, :, sl] = (_dot(p.astype(BF16), v) / l).astype(o_ref.dtype)

    _for_key_extent(q0 + tq - 1, seq, body, step=max(topk, 2 * tq))


def _dsa_attention(hr3, hp3):
    b, seq, _ = hr3.shape
    tq = min(128, seq)
    topk = min(DSA_TOPK, seq // 4)
    qw = DSA_HEADS * HEAD_DIM
    iqw = IDX_HEADS * IDX_DIM
    return pl.pallas_call(
        functools.partial(_dsa_kernel, tq=tq, seq=seq, topk=topk),
        grid=(b, seq // tq),
        in_specs=[pl.BlockSpec((1, tq, qw), lambda i, j: (i, j, ROPE_POS['dsa_q'] // qw)),
                  pl.BlockSpec((1, seq, LANES), lambda i, j: (i, 0, ROPE_POS['dsa_k'] // LANES)),
                  pl.BlockSpec((1, seq, LANES), lambda i, j: (i, 0, PLAIN_POS['dsa_v'] // LANES)),
                  pl.BlockSpec((1, tq, iqw), lambda i, j: (i, j, ROPE_POS['idx_q'] // iqw)),
                  pl.BlockSpec((1, tq, LANES), lambda i, j: (i, j, SMALL_COL // LANES))],
        out_specs=pl.BlockSpec((1, tq, qw), lambda i, j: (i, j, 0)),
        out_shape=jax.ShapeDtypeStruct((b, seq, qw), BF16),
        scratch_shapes=[pltpu.VMEM((tq, seq), I32)],
        compiler_params=_cparams(("parallel", "arbitrary")),
    )(hr3, hr3, hp3, hr3, hp3)


def _outproj_kernel(of_ref, on_ref, od_ref, x_ref, wf_ref, wn_ref, wd_ref, g_ref, b_ref, o_ref, *, alpha):
    mix = _dot(of_ref[...], wf_ref[...]) + _dot(on_ref[...], wn_ref[...]) + _dot(od_ref[...], wd_ref[...])
    o_ref[...] = _layer_norm(alpha * x_ref[...] + mix, g_ref[...], b_ref[...])


def _outproj_ln(o_fox, o_nsa, o_dsa, x2d, w_f, w_n, w_d, g, b, alpha):
    n = x2d.shape[0]
    tm = min(512, n)
    row = lambda a: pl.BlockSpec((tm, a.shape[1]), lambda i: (i, 0))
    full = lambda a: pl.BlockSpec(a.shape, lambda i: (0, 0))
    return pl.pallas_call(
        functools.partial(_outproj_kernel, alpha=alpha),
        grid=(n // tm,),
        in_specs=[row(o_fox), row(o_nsa), row(o_dsa), row(x2d), full(w_f), full(w_n), full(w_d), full(g), full(b)],
        out_specs=pl.BlockSpec((tm, D_MODEL), lambda i: (i, 0)),
        out_shape=jax.ShapeDtypeStruct((n, D_MODEL), F32),
        compiler_params=_cparams(("parallel",)),
    )(o_fox, o_nsa, o_dsa, x2d, w_f, w_n, w_d, g, b)


def _router_kernel(x_ref, wr_ref, br_ref, g_ref):
    tn = x_ref.shape[0]
    logits = _nt_dot(wr_ref[...], x_ref[...].astype(BF16))
    s = 1.0 / (1.0 + jnp.exp(-logits))
    sb = s + br_ref[...]
    low = jnp.float32(-3e38)

    grp = []
    for gi in range(N_GROUPS):
        blk = sb[gi * GROUP_SIZE:(gi + 1) * GROUP_SIZE]
        m1 = jnp.max(blk, axis=0, keepdims=True)
        is_max = blk == m1
        n_max = jnp.sum(is_max.astype(F32), axis=0, keepdims=True)
        m2 = jnp.max(jnp.where(is_max, low, blk), axis=0, keepdims=True)
        grp.append(m1 + jnp.where(n_max >= 2.0, m1, m2))
    masked = []
    for gi in range(N_GROUPS):
        rank = jnp.zeros((1, tn), F32)
        for g2 in range(N_GROUPS):
            if g2 == gi:
                continue
            beats = (grp[g2] > grp[gi]) | ((grp[g2] == grp[gi]) if g2 < gi else False)
            rank = rank + beats.astype(F32)
        keep = rank < float(TOPK_GROUPS)
        masked.append(jnp.where(keep, sb[gi * GROUP_SIZE:(gi + 1) * GROUP_SIZE], NEG))
    masked = jnp.concatenate(masked, axis=0)

    eidx = lax.broadcasted_iota(I32, (N_EXPERTS, tn), 0)
    rank = jnp.zeros((N_EXPERTS, tn), F32)
    for e2 in range(N_EXPERTS):
        row = masked[e2:e2 + 1]
        beats = (row > masked) | ((row == masked) & (e2 < eidx))
        rank = rank + beats.astype(F32)
    gw = jnp.where(rank < float(MOE_TOPK), s, 0.0)
    g_ref[...] = gw / jnp.sum(gw, axis=0, keepdims=True) * ROUTED_SCALE


def _router(x2d, wr_t, br_col):
    n = x2d.shape[0]
    tn = min(512, n)
    return pl.pallas_call(
        _router_kernel,
        grid=(n // tn,),
        in_specs=[pl.BlockSpec((tn, D_MODEL), lambda i: (i, 0)),
                  pl.BlockSpec(wr_t.shape, lambda i: (0, 0)),
                  pl.BlockSpec(br_col.shape, lambda i: (0, 0))],
        out_specs=pl.BlockSpec((N_EXPERTS, tn), lambda i: (0, i)),
        out_shape=jax.ShapeDtypeStruct((N_EXPERTS, n), F32),
        compiler_params=_cparams(("parallel",)),
    )(x2d, wr_t, br_col)


def _silu(x):
    return x / (1.0 + jnp.exp(-x))


def _moe_kernel(x_ref, gt_ref, wg_ref, wu_ref, wd_ref, sg_ref, su_ref, sd_ref, lg_ref, lb_ref, o_ref,
                xb_ref, acc_ref, *, alpha):
    e = pl.program_id(1)

    @pl.when(e == 0)
    def _():
        xb = x_ref[...].astype(BF16)
        xb_ref[...] = xb
        h = _silu(_dot(xb, sg_ref[...])) * _dot(xb, su_ref[...])
        acc_ref[...] = _dot(h.astype(BF16), sd_ref[...])

    xb = xb_ref[...]
    h = (_silu(_dot(xb, wg_ref[0])) * _dot(xb, wu_ref[0])).astype(BF16)
    gates = gt_ref[...]
    lane = lax.broadcasted_iota(I32, gates.shape, 1)
    gcol = jnp.sum(jnp.where(lane == e, gates, 0.0), axis=1, keepdims=True)
    chunk = 256
    for c in range(D_MODEL // chunk):
        cs = slice(c * chunk, (c + 1) * chunk)
        acc_ref[:, cs] += gcol * _dot(h, wd_ref[0, :, cs])

    @pl.when(e == pl.num_programs(1) - 1)
    def _():
        o_ref[...] = _layer_norm(alpha * x_ref[...] + acc_ref[...], lg_ref[...], lb_ref[...])


def _moe_ln(x2d, gates, w_gate, w_up, w_down, ws_gate, ws_up, ws_down, g, b, alpha):
    n = x2d.shape[0]
    tn = min(1024, n)
    full = lambda a: pl.BlockSpec(a.shape, lambda i, e: (0,) * a.ndim)
    return pl.pallas_call(
        functools.partial(_moe_kernel, alpha=alpha),
        grid=(n // tn, N_EXPERTS),
        in_specs=[pl.BlockSpec((tn, D_MODEL), lambda i, e: (i, 0)),
                  pl.BlockSpec((tn, N_EXPERTS), lambda i, e: (i, 0)),
                  pl.BlockSpec((1, D_MODEL, EXPERT_DIM), lambda i, e: (e, 0, 0)),
                  pl.BlockSpec((1, D_MODEL, EXPERT_DIM), lambda i, e: (e, 0, 0)),
                  pl.BlockSpec((1, EXPERT_DIM, D_MODEL), lambda i, e: (e, 0, 0)),
                  full(ws_gate), full(ws_up), full(ws_down), full(g), full(b)],
        out_specs=pl.BlockSpec((tn, D_MODEL), lambda i, e: (i, 0)),
        out_shape=jax.ShapeDtypeStruct((n, D_MODEL), F32),
        scratch_shapes=[pltpu.VMEM((tn, D_MODEL), BF16), pltpu.VMEM((tn, D_MODEL), F32)],
        compiler_params=_cparams(("parallel", "arbitrary")),
    )(x2d, gates, w_gate, w_up, w_down, ws_gate, ws_up, ws_down, g, b)


def _gather_cols(w, order):
    parts = []
    for item in order:
        if isinstance(item, tuple):
            parts.append(jnp.zeros(w.shape[:-1] + (item[1],), w.dtype))
        else:
            off, width = SEG_OFF[item]
            parts.append(w[..., off:off + width])
    return jnp.concatenate(parts, axis=-1)


def _rotate_half_cols(w):
    lead = w.shape[:-1]
    w4 = w.reshape(lead + (-1, 2, HEAD_DIM // 2))
    return jnp.concatenate([-w4[..., 1:2, :], w4[..., 0:1, :]], axis=-2).reshape(w.shape)


def _rope_tables(seq, width):
    half = HEAD_DIM // 2
    inv_freq = ROPE_THETA ** (-jnp.arange(half, dtype=F32) * (2.0 / HEAD_DIM))
    ang = jnp.arange(seq).astype(F32)[:, None] * inv_freq[None, :]
    reps = width // half
    return jnp.tile(jnp.cos(ang), (1, reps)), jnp.tile(jnp.sin(ang), (1, reps))


def _token_rows(tok, b, seq):
    t4 = tok.reshape(b, seq, NSA_KV_HEADS, HEAD_DIM).transpose(0, 2, 1, 3)
    return t4.reshape(b * NSA_KV_HEADS, seq // CMP_STRIDE, CMP_STRIDE * HEAD_DIM)


def kernel(x, w_in, fox_forget_bias, cmp_pos_k, cmp_w1_k, cmp_w2_k, cmp_pos_v, cmp_w1_v, cmp_w2_v, w_out,
           ln1_g, ln1_b, w_router, b_router, w_gate, w_up, w_down, ws_gate, ws_up, ws_down, ln2_g, ln2_b):
    b, seq, dm = x.shape
    depth = w_in.shape[0]
    n = b * seq
    alpha = float((2 * depth) ** 0.25)

    w_plain = _gather_cols(w_in, PLAIN_ORDER).astype(BF16)
    w_rope = _gather_cols(w_in, ROPE_ORDER)
    w_rope_a = w_rope.astype(BF16)
    w_rope_b = _rotate_half_cols(w_rope).astype(BF16)
    cos_t, sin_t = _rope_tables(seq, 256)
    fw, nw = FOX_HEADS * HEAD_DIM, NSA_HEADS * HEAD_DIM
    w_out_b = w_out.astype(BF16)

    x2d = x.reshape(n, dm)
    for l in range(depth):
        hp, hr = _project(x2d, w_plain[l], w_rope_a[l], w_rope_b[l], cos_t, sin_t, seq)
        hp3 = hp.reshape(b, seq, PLAIN_W)
        hr3 = hr.reshape(b, seq, ROPE_W)

        f_logit = hp3[:, :, SMALL_COL + SMALL_F:SMALL_COL + SMALL_F + FOX_HEADS]
        c_rows = _fox_cumsum(f_logit.transpose(0, 2, 1), fox_forget_bias[l].reshape(FOX_HEADS, 1))
        o_fox = _fox_attention(hp3, c_rows.transpose(0, 2, 1), c_rows)

        kvw = NSA_KV_HEADS * HEAD_DIM
        rk = _token_rows(hr3[:, :, ROPE_POS['nsa_kc']:ROPE_POS['nsa_kc'] + kvw], b, seq)
        rv = _token_rows(hp3[:, :, PLAIN_POS['nsa_vc']:PLAIN_POS['nsa_vc'] + kvw], b, seq)
        k_cmp, v_cmp = _nsa_compress(
            rk, rv, cmp_pos_k[l].reshape(1, -1), cmp_pos_v[l].reshape(1, -1),
            cmp_w1_k[l].astype(BF16), cmp_w2_k[l].astype(BF16), cmp_w1_v[l].astype(BF16), cmp_w2_v[l].astype(BF16))
        n_rows = seq // CMP_STRIDE
        o_nsa = _nsa_attention(hr3, hp3, k_cmp.reshape(b, NSA_KV_HEADS, n_rows, HEAD_DIM),
                               v_cmp.reshape(b, NSA_KV_HEADS, n_rows, HEAD_DIM))

        o_dsa = _dsa_attention(hr3, hp3)

        x2d = _outproj_ln(o_fox.reshape(n, fw), o_nsa.reshape(n, nw), o_dsa.reshape(n, nw), x2d,
                          w_out_b[l, :fw], w_out_b[l, fw:fw + nw], w_out_b[l, fw + nw:],
                          ln1_g[l].reshape(1, dm), ln1_b[l].reshape(1, dm), alpha)

        gates_t = _router(x2d, w_router[l].T.astype(BF16), b_router[l].reshape(N_EXPERTS, 1))
        x2d = _moe_ln(x2d, gates_t.T, w_gate[l].astype(BF16), w_up[l].astype(BF16), w_down[l].astype(BF16),
                      ws_gate[l].astype(BF16), ws_up[l].astype(BF16), ws_down[l].astype(BF16),
                      ln2_g[l].reshape(1, dm), ln2_b[l].reshape(1, dm), alpha)
    return x2d.reshape(b, seq, dm)
```

```python
import functools

import numpy as np
import jax
import jax.numpy as jnp
from jax import lax
from jax.experimental import pallas as pl
from jax.experimental.pallas import tpu as pltpu

D_MODEL = 1024
HEAD_DIM = 64
FOX_HEADS = 4
NSA_HEADS = 6
NSA_KV_HEADS = 2
NSA_REP = NSA_HEADS // NSA_KV_HEADS
DSA_HEADS = 6
ROPE_THETA = 10000.0
CMP_LEN = 32
CMP_STRIDE = 16
CMP_HIDDEN = 2 * HEAD_DIM
SEL_BLOCK = 64
SEL_TOPN = 16
WINDOW = 512
IDX_HEADS = 4
IDX_DIM = 64
DSA_TOPK = 256
N_EXPERTS = 64
N_GROUPS = 8
GROUP_SIZE = N_EXPERTS // N_GROUPS
TOPK_GROUPS = 4
MOE_TOPK = 8
EXPERT_DIM = 256
SHARED_DIM = 256
ROUTED_SCALE = 2.5
LN_EPS = 1e-5
NEG = -1e30
FORCE = 1e6
SCALE = HEAD_DIM ** -0.5

F32 = jnp.float32
BF16 = jnp.bfloat16
I32 = jnp.int32

VMEM_LIMIT_BYTES = 52 * 1024 * 1024
LANES = 128
KEY_STEP = 512

SEGMENTS = (
    ('fox_q', FOX_HEADS * HEAD_DIM), ('fox_k', FOX_HEADS * HEAD_DIM),
    ('fox_v', FOX_HEADS * HEAD_DIM), ('fox_f', FOX_HEADS),
    ('nsa_q', NSA_HEADS * HEAD_DIM),
    ('nsa_kc', NSA_KV_HEADS * HEAD_DIM), ('nsa_vc', NSA_KV_HEADS * HEAD_DIM),
    ('nsa_ks', NSA_KV_HEADS * HEAD_DIM), ('nsa_vs', NSA_KV_HEADS * HEAD_DIM),
    ('nsa_kw', NSA_KV_HEADS * HEAD_DIM), ('nsa_vw', NSA_KV_HEADS * HEAD_DIM),
    ('nsa_g', 3 * NSA_HEADS),
    ('dsa_q', DSA_HEADS * HEAD_DIM), ('dsa_k', HEAD_DIM), ('dsa_v', HEAD_DIM),
    ('idx_q', IDX_HEADS * IDX_DIM), ('idx_k', IDX_DIM), ('idx_w', IDX_HEADS),
)
SEG_OFF = {}
_off = 0
for _name, _width in SEGMENTS:
    SEG_OFF[_name] = (_off, _width)
    _off += _width
N_IN = _off

PLAIN_ORDER = ('fox_q', 'fox_k', 'fox_v', 'nsa_vc', 'nsa_vs', 'nsa_vw', 'dsa_v', (None, 64),
               'fox_f', 'nsa_g', 'idx_w', (None, 2 * LANES - FOX_HEADS - 3 * NSA_HEADS - IDX_HEADS))
ROPE_ORDER = ('nsa_q', 'nsa_kc', 'nsa_ks', 'nsa_kw', 'dsa_q', 'dsa_k', 'idx_k', 'idx_q')


def _layout(order):
    pos, off = {}, 0
    for item in order:
        if isinstance(item, tuple):
            off += item[1]
        else:
            pos[item] = off
            off += SEG_OFF[item][1]
    return pos, off


PLAIN_POS, PLAIN_W = _layout(PLAIN_ORDER)
ROPE_POS, ROPE_W = _layout(ROPE_ORDER)
SMALL_COL = PLAIN_POS['fox_f']
SMALL_F = 0
SMALL_G = FOX_HEADS
SMALL_W = FOX_HEADS + 3 * NSA_HEADS


def _cparams(sem):
    return pltpu.CompilerParams(dimension_semantics=sem, vmem_limit_bytes=VMEM_LIMIT_BYTES)


def _nt_dot(a, b):
    return lax.dot_general(a, b, (((1,), (1,)), ((), ())), preferred_element_type=F32)


def _dot(a, b):
    return jnp.dot(a, b, preferred_element_type=F32)


def _softmax_rows(s):
    m = jnp.max(s, axis=-1, keepdims=True)
    p = jnp.exp(s - m)
    return p, jnp.sum(p, axis=-1, keepdims=True)


def _for_key_extent(q_last, seq, body, step=KEY_STEP):
    n = seq // step
    if n <= 1:
        body(seq)
        return
    c = q_last // step
    for i in range(n):
        pl.when(c == i)(functools.partial(body, (i + 1) * step))


def _layer_norm(z, g, b):
    mu = jnp.mean(z, axis=-1, keepdims=True)
    zc = z - mu
    var = jnp.mean(zc * zc, axis=-1, keepdims=True)
    return zc * lax.rsqrt(var + LN_EPS) * g + b


def _proj_kernel(x_ref, w_ref, o_ref, xb_ref):
    @pl.when(pl.program_id(1) == 0)
    def _():
        xb_ref[...] = x_ref[...].astype(BF16)

    o_ref[...] = _dot(xb_ref[...], w_ref[...])


def _proj_rope_kernel(x_ref, wa_ref, wb_ref, cos_ref, sin_ref, o_ref, xb_ref):
    @pl.when(pl.program_id(1) == 0)
    def _():
        xb_ref[...] = x_ref[...].astype(BF16)

    xb = xb_ref[...]
    o_ref[...] = _dot(xb, wa_ref[...]) * cos_ref[...] + _dot(xb, wb_ref[...]) * sin_ref[...]


def _project(x2d, w_plain, w_a, w_b, cos_t, sin_t, seq):
    n = x2d.shape[0]
    tm = min(1024, seq)
    tc = 256
    plain = pl.pallas_call(
        _proj_kernel,
        grid=(n // tm, PLAIN_W // tc),
        in_specs=[pl.BlockSpec((tm, D_MODEL), lambda i, j: (i, 0)),
                  pl.BlockSpec((D_MODEL, tc), lambda i, j: (0, j))],
        out_specs=pl.BlockSpec((tm, tc), lambda i, j: (i, j)),
        out_shape=jax.ShapeDtypeStruct((n, PLAIN_W), F32),
        scratch_shapes=[pltpu.VMEM((tm, D_MODEL), BF16)],
        compiler_params=_cparams(("parallel", "arbitrary")),
    )(x2d, w_plain)
    nt = seq // tm
    roped = pl.pallas_call(
        _proj_rope_kernel,
        grid=(n // tm, ROPE_W // tc),
        in_specs=[pl.BlockSpec((tm, D_MODEL), lambda i, j: (i, 0)),
                  pl.BlockSpec((D_MODEL, tc), lambda i, j: (0, j)),
                  pl.BlockSpec((D_MODEL, tc), lambda i, j: (0, j)),
                  pl.BlockSpec((tm, tc), lambda i, j: (i % nt, 0)),
                  pl.BlockSpec((tm, tc), lambda i, j: (i % nt, 0))],
        out_specs=pl.BlockSpec((tm, tc), lambda i, j: (i, j)),
        out_shape=jax.ShapeDtypeStruct((n, ROPE_W), F32),
        scratch_shapes=[pltpu.VMEM((tm, D_MODEL), BF16)],
        compiler_params=_cparams(("parallel", "arbitrary")),
    )(x2d, w_a, w_b, cos_t, sin_t)
    return plain, roped


def _fox_cum_kernel(f_ref, fb_ref, o_ref):
    x = f_ref[0] + fb_ref[...]
    c = jnp.minimum(x, 0.0) - jnp.log1p(jnp.exp(-jnp.abs(x)))
    seq = c.shape[-1]
    lane = lax.broadcasted_iota(I32, c.shape, 1)
    sh = 1
    while sh < seq:
        c = c + jnp.where(lane >= sh, pltpu.roll(c, sh, 1), 0.0)
        sh *= 2
    o_ref[0] = c


def _fox_cumsum(f_rows, f_bias):
    b, h, seq = f_rows.shape
    return pl.pallas_call(
        _fox_cum_kernel,
        grid=(b,),
        in_specs=[pl.BlockSpec((1, h, seq), lambda i: (i, 0, 0)),
                  pl.BlockSpec((h, 1), lambda i: (0, 0))],
        out_specs=pl.BlockSpec((1, h, seq), lambda i: (i, 0, 0)),
        out_shape=jax.ShapeDtypeStruct((b, h, seq), F32),
        compiler_params=_cparams(("parallel",)),
    )(f_rows, f_bias)


def _fox_kernel(q_ref, k_ref, v_ref, cc_ref, cr_ref, o_ref, *, tq, seq):
    q0 = pl.program_id(1) * tq

    def body(klen):
        qpos = q0 + lax.broadcasted_iota(I32, (tq, klen), 0)
        causal = lax.broadcasted_iota(I32, (tq, klen), 1) <= qpos
        for h in range(FOX_HEADS):
            sl = slice(h * HEAD_DIM, (h + 1) * HEAD_DIM)
            qh = (q_ref[0, :, sl] * SCALE).astype(BF16)
            kh = k_ref[0, :klen, sl].astype(BF16)
            vh = v_ref[0, :klen, sl].astype(BF16)
            s = _nt_dot(qh, kh) + (cc_ref[0, :, h:h + 1] - cr_ref[0, h:h + 1, :klen])
            p, l = _softmax_rows(jnp.where(causal, s, NEG))
            o_ref[0, :, sl] = (_dot(p.astype(BF16), vh) / l).astype(o_ref.dtype)

    _for_key_extent(q0 + tq - 1, seq, body, step=tq)


def _fox_attention(hp3, c_cols, c_rows):
    b, seq, _ = hp3.shape
    tq = min(256, seq)
    w = FOX_HEADS * HEAD_DIM
    return pl.pallas_call(
        functools.partial(_fox_kernel, tq=tq, seq=seq),
        grid=(b, seq // tq),
        in_specs=[pl.BlockSpec((1, tq, w), lambda i, j: (i, j, PLAIN_POS['fox_q'] // w)),
                  pl.BlockSpec((1, seq, w), lambda i, j: (i, 0, PLAIN_POS['fox_k'] // w)),
                  pl.BlockSpec((1, seq, w), lambda i, j: (i, 0, PLAIN_POS['fox_v'] // w)),
                  pl.BlockSpec((1, tq, FOX_HEADS), lambda i, j: (i, j, 0)),
                  pl.BlockSpec((1, FOX_HEADS, seq), lambda i, j: (i, 0, 0))],
        out_specs=pl.BlockSpec((1, tq, w), lambda i, j: (i, j, 0)),
        out_shape=jax.ShapeDtypeStruct((b, seq, w), BF16),
        compiler_params=_cparams(("parallel", "arbitrary")),
    )(hp3, hp3, hp3, c_cols, c_rows)


def _gelu_tanh(x):
    return 0.5 * x * (1.0 + jnp.tanh(np.float32(np.sqrt(2.0 / np.pi)) * (x + 0.044715 * (x * x * x))))


def _compress_kernel(rk_ref, rv_ref, pek_ref, pev_ref, w1k_ref, w2k_ref, w1v_ref, w2v_ref, ok_ref, ov_ref):
    half = CMP_STRIDE * HEAD_DIM

    def one(r_ref, pe_ref, w1_ref, w2_ref, o_ref):
        r = r_ref[0]
        n_rows = r.shape[0]
        lo = _dot((r + pe_ref[:, :half]).astype(BF16), w1_ref[:half, :])
        hi = _dot((r + pe_ref[:, half:]).astype(BF16), w1_ref[half:, :])
        hid = _gelu_tanh(lo + pltpu.roll(hi, n_rows - 1, 0))
        o_ref[0] = _dot(hid.astype(BF16), w2_ref[...])

    one(rk_ref, pek_ref, w1k_ref, w2k_ref, ok_ref)
    one(rv_ref, pev_ref, w1v_ref, w2v_ref, ov_ref)


def _nsa_compress(rk, rv, pek, pev, w1k, w2k, w1v, w2v):
    bg, rows, width = rk.shape
    tok = pl.BlockSpec((1, rows, width), lambda i: (i, 0, 0))
    full = lambda a: pl.BlockSpec(a.shape, lambda i: (0,) * a.ndim)
    out = pl.BlockSpec((1, rows, HEAD_DIM), lambda i: (i, 0, 0))
    return pl.pallas_call(
        _compress_kernel,
        grid=(bg,),
        in_specs=[tok, tok, full(pek), full(pev), full(w1k), full(w2k), full(w1v), full(w2v)],
        out_specs=[out, out],
        out_shape=[jax.ShapeDtypeStruct((bg, rows, HEAD_DIM), F32)] * 2,
        compiler_params=_cparams(("parallel",)),
    )(rk, rv, pek, pev, w1k, w2k, w1v, w2v)


def _split3_nt_dot(b01, a):
    a1 = a.astype(BF16)
    r1 = a - a1.astype(F32)
    a2 = r1.astype(BF16)
    a3 = (r1 - a2.astype(F32)).astype(BF16)
    return _nt_dot(b01, a1) + _nt_dot(b01, a2) + _nt_dot(b01, a3)


def _nsa_kernel(q_ref, kc_ref, vc_ref, ks_ref, kw_ref, vs_ref, vw_ref, sm_ref, o_ref, oslc_ref, *, tq, seq, wlen):
    q0 = pl.program_id(1) * tq
    n_c = kc_ref.shape[2]
    n_s = seq // SEL_BLOCK
    rows = NSA_REP * tq
    tcol = q0 + lax.broadcasted_iota(I32, (tq, 1), 0)
    tcol_r = jnp.concatenate([tcol] * NSA_REP, axis=0)

    cidx = lax.broadcasted_iota(I32, (rows, n_c), 1)
    vis_r = (cidx * CMP_STRIDE + (CMP_LEN - 1)) <= tcol_r
    any_vis_r = (tcol_r >= (CMP_LEN - 1)).astype(F32)
    oc = lax.broadcasted_iota(I32, (n_s, n_c), 1) * CMP_STRIDE
    ob = lax.broadcasted_iota(I32, (n_s, n_c), 0) * SEL_BLOCK
    overlap_t = ((oc < ob + SEL_BLOCK) & (oc + CMP_LEN > ob)).astype(BF16)
    jj_t = lax.broadcasted_iota(I32, (n_s, tq), 0)
    blk_t = (q0 + lax.broadcasted_iota(I32, (1, tq), 1)) // SEL_BLOCK
    w0 = pl.multiple_of(jnp.maximum(q0 + tq - wlen, 0), 8)
    dist = tcol_r - (w0 + lax.broadcasted_iota(I32, (rows, wlen), 1))
    band_r = (dist >= 0) & (dist < WINDOW)
    gates = sm_ref[0]

    qs_all, qs_aug_all, o_cmp_all = [], [], []
    for g in range(NSA_KV_HEADS):
        qs = (jnp.concatenate(
            [q_ref[0, :, (g * NSA_REP + r) * HEAD_DIM:(g * NSA_REP + r + 1) * HEAD_DIM] for r in range(NSA_REP)],
            axis=0) * SCALE).astype(BF16)

        s_c = _nt_dot(qs, kc_ref[0, g].astype(BF16))
        p_c, l_c = _softmax_rows(jnp.where(vis_r, s_c, NEG))
        p_c = p_c / l_c * any_vis_r
        o_cmp_all.append(_dot(p_c.astype(BF16), vc_ref[0, g].astype(BF16)))

        p_sum = p_c[0:tq]
        for r in range(1, NSA_REP):
            p_sum = p_sum + p_c[r * tq:(r + 1) * tq]
        imp = _split3_nt_dot(overlap_t, p_sum)
        forced = (jj_t == 0) | (jj_t == blk_t) | (jj_t == blk_t - 1)
        imp = jnp.where(jj_t <= blk_t, jnp.where(forced, FORCE, imp), -1.0)
        rank = jnp.zeros((n_s, tq), F32)
        for j2 in range(n_s):
            row = imp[j2:j2 + 1, :]
            beats = (row > imp) | ((row == imp) & (j2 < jj_t))
            rank = rank + beats.astype(F32)
        sel = (rank < float(min(SEL_TOPN, n_s))) & (jj_t <= blk_t)
        bias_t = jnp.concatenate([jnp.where(sel, 0.0, NEG), jnp.zeros((tq - n_s, tq), F32)], axis=0)
        sel_bias = bias_t.T[:, :LANES - HEAD_DIM]
        qs_all.append(qs)
        qs_aug_all.append(jnp.concatenate([qs, jnp.concatenate([sel_bias] * NSA_REP, axis=0).astype(BF16)], axis=1))

    def slc_body(klen):
        block_onehot = (lax.broadcasted_iota(I32, (klen, LANES - HEAD_DIM), 0) // SEL_BLOCK
                        == lax.broadcasted_iota(I32, (klen, LANES - HEAD_DIM), 1)).astype(BF16)
        causal_r = lax.broadcasted_iota(I32, (rows, klen), 1) <= tcol_r
        for g in range(NSA_KV_HEADS):
            ksl = slice(g * HEAD_DIM, (g + 1) * HEAD_DIM)
            k_aug = jnp.concatenate([ks_ref[0, :klen, ksl].astype(BF16), block_onehot], axis=1)
            p_s, l_s = _softmax_rows(jnp.where(causal_r, _nt_dot(qs_aug_all[g], k_aug), NEG))
            oslc_ref[g] = _dot(p_s.astype(BF16), vs_ref[0, :klen, ksl].astype(BF16)) / l_s

    _for_key_extent(q0 + tq - 1, seq, slc_body)

    for g in range(NSA_KV_HEADS):
        ksl = slice(g * HEAD_DIM, (g + 1) * HEAD_DIM)
        o_cmp, o_slc = o_cmp_all[g], oslc_ref[g]

        kw = kw_ref[0, pl.ds(w0, wlen), ksl].astype(BF16)
        vw = vw_ref[0, pl.ds(w0, wlen), ksl].astype(BF16)
        p_w, l_w = _softmax_rows(jnp.where(band_r, _nt_dot(qs_all[g], kw), NEG))
        o_win = _dot(p_w.astype(BF16), vw) / l_w

        for r in range(NSA_REP):
            head = g * NSA_REP + r
            rs = slice(r * tq, (r + 1) * tq)

            def gate(branch, head=head):
                col = SMALL_G + branch * NSA_HEADS + head
                return 1.0 / (1.0 + jnp.exp(-gates[:, col:col + 1]))

            out = gate(0) * o_cmp[rs] + gate(1) * o_slc[rs] + gate(2) * o_win[rs]
            o_ref[0, :, head * HEAD_DIM:(head + 1) * HEAD_DIM] = out.astype(o_ref.dtype)


def _nsa_attention(hr3, hp3, k_cmp, v_cmp):
    b, seq, _ = hr3.shape
    tq = min(128, seq)
    wlen = min(WINDOW + tq, seq)
    qw = NSA_HEADS * HEAD_DIM
    kvw = NSA_KV_HEADS * HEAD_DIM
    n_c = k_cmp.shape[2]
    cmp_spec = pl.BlockSpec((1, NSA_KV_HEADS, n_c, HEAD_DIM), lambda i, j: (i, 0, 0, 0))

    def seq_spec(col):
        return pl.BlockSpec((1, seq, kvw), lambda i, j: (i, 0, col // kvw))

    return pl.pallas_call(
        functools.partial(_nsa_kernel, tq=tq, seq=seq, wlen=wlen),
        grid=(b, seq // tq),
        in_specs=[pl.BlockSpec((1, tq, qw), lambda i, j: (i, j, ROPE_POS['nsa_q'] // qw)),
                  cmp_spec, cmp_spec,
                  seq_spec(ROPE_POS['nsa_ks']), seq_spec(ROPE_POS['nsa_kw']),
                  seq_spec(PLAIN_POS['nsa_vs']), seq_spec(PLAIN_POS['nsa_vw']),
                  pl.BlockSpec((1, tq, LANES), lambda i, j: (i, j, SMALL_COL // LANES))],
        out_specs=pl.BlockSpec((1, tq, qw), lambda i, j: (i, j, 0)),
        out_shape=jax.ShapeDtypeStruct((b, seq, qw), BF16),
        scratch_shapes=[pltpu.VMEM((NSA_KV_HEADS, NSA_REP * tq, HEAD_DIM), F32)],
        compiler_params=_cparams(("parallel", "arbitrary")),
    )(hr3, k_cmp, v_cmp, hr3, hr3, hp3, hp3, hp3)


def _row_count(mask):
    return jnp.sum(mask.astype(F32), axis=-1, keepdims=True)


NEG_KEY = int(np.float32(NEG).view(np.int32)) ^ 0x7FFFFFFF


def _dsa_kernel(q_ref, kk_ref, v_ref, iq_ref, sm_ref, o_ref, key_ref, bias_ref, *, tq, seq, topk):
    q0 = pl.program_id(1) * tq
    tcol = q0 + lax.broadcasted_iota(I32, (tq, 1), 0)
    w_idx = sm_ref[0, :, SMALL_W:SMALL_W + IDX_HEADS] * (IDX_HEADS ** -0.5) * (IDX_DIM ** -0.5)
    kf = float(topk)
    sub = min(tq, LANES)

    def body(klen):
        ik = kk_ref[0, :klen, HEAD_DIM:2 * HEAD_DIM].astype(BF16)
        score = jnp.zeros((tq, klen), F32)
        for h in range(IDX_HEADS):
            d = _nt_dot(iq_ref[0, :, h * IDX_DIM:(h + 1) * IDX_DIM].astype(BF16), ik)
            score = score + w_idx[:, h:h + 1] * jnp.maximum(d, 0.0)
        score = jnp.where(lax.broadcasted_iota(I32, (tq, klen), 1) <= tcol, score, NEG)

        bits = lax.bitcast_convert_type(score, I32)
        key_ref[:, :klen] = jnp.where(bits < 0, bits ^ 0x7FFFFFFF, bits)

        def tau_step(i, tau):
            cand = tau + jnp.left_shift(jnp.int32(1), 31 - i)
            cnt = _row_count(key_ref[:, :klen] >= cand)
            return jnp.where(cnt >= kf, cand, tau)

        tau = lax.fori_loop(0, 32, tau_step, jnp.full((tq, 1), -2 ** 31, I32))
        need = kf - _row_count(key_ref[:, :klen] > tau)

        tri = (lax.broadcasted_iota(I32, (LANES, LANES), 0) <= lax.broadcasted_iota(I32, (LANES, LANES), 1)).astype(BF16)
        seen = jnp.zeros((tq, 1), F32)
        for blk in range(klen // LANES):
            ks = slice(blk * LANES, (blk + 1) * LANES)
            key = key_ref[:, ks]
            tie = key == tau
            count = _dot(jnp.where(tie, 1.0, 0.0).astype(BF16), tri) + seen
            seen = count[:, LANES - 1:LANES]
            kpos = blk * LANES + lax.broadcasted_iota(I32, (tq, LANES), 1)
            chosen = ((key > tau) | (tie & (count <= need))) & (kpos <= tcol)
            bias_ref[:, ks] = jnp.where(chosen, 0.0, NEG)

        k = kk_ref[0, :klen, 0:HEAD_DIM].astype(BF16)
        v = v_ref[0, :klen, 0:HEAD_DIM].astype(BF16)
        for r0 in range(0, tq, sub):
            bias = bias_ref[r0:r0 + sub, :klen]
            for h in range(DSA_HEADS):
                sl = slice(h * HEAD_DIM, (h + 1) * HEAD_DIM)
                s = _nt_dot((q_ref[0, r0:r0 + sub, sl] * SCALE).astype(BF16), k)
                p, l = _softmax_rows(s + bias)
                o_ref[0, r0:r0 + sub, sl] = (_dot(p.astype(BF16), v) / l).astype(o_ref.dtype)

    _for_key_extent(q0 + tq - 1, seq, body, step=max(topk, tq))


def _dsa_attention(hr3, hp3):
    b, seq, _ = hr3.shape
    tq = min(256, seq)
    topk = min(DSA_TOPK, seq // 4)
    qw = DSA_HEADS * HEAD_DIM
    iqw = IDX_HEADS * IDX_DIM
    return pl.pallas_call(
        functools.partial(_dsa_kernel, tq=tq, seq=seq, topk=topk),
        grid=(b, seq // tq),
        in_specs=[pl.BlockSpec((1, tq, qw), lambda i, j: (i, j, ROPE_POS['dsa_q'] // qw)),
                  pl.BlockSpec((1, seq, LANES), lambda i, j: (i, 0, ROPE_POS['dsa_k'] // LANES)),
                  pl.BlockSpec((1, seq, LANES), lambda i, j: (i, 0, PLAIN_POS['dsa_v'] // LANES)),
                  pl.BlockSpec((1, tq, iqw), lambda i, j: (i, j, ROPE_POS['idx_q'] // iqw)),
                  pl.BlockSpec((1, tq, LANES), lambda i, j: (i, j, SMALL_COL // LANES))],
        out_specs=pl.BlockSpec((1, tq, qw), lambda i, j: (i, j, 0)),
        out_shape=jax.ShapeDtypeStruct((b, seq, qw), BF16),
        scratch_shapes=[pltpu.VMEM((tq, seq), I32), pltpu.VMEM((tq, seq), F32)],
        compiler_params=_cparams(("parallel", "arbitrary")),
    )(hr3, hr3, hp3, hr3, hp3)


def _outproj_kernel(of_ref, on_ref, od_ref, x_ref, wf_ref, wn_ref, wd_ref, g_ref, b_ref, o_ref, *, alpha):
    mix = _dot(of_ref[...], wf_ref[...]) + _dot(on_ref[...], wn_ref[...]) + _dot(od_ref[...], wd_ref[...])
    o_ref[...] = _layer_norm(alpha * x_ref[...] + mix, g_ref[...], b_ref[...])


def _outproj_ln(o_fox, o_nsa, o_dsa, x2d, w_f, w_n, w_d, g, b, alpha):
    n = x2d.shape[0]
    tm = min(512, n)
    row = lambda a: pl.BlockSpec((tm, a.shape[1]), lambda i: (i, 0))
    full = lambda a: pl.BlockSpec(a.shape, lambda i: (0, 0))
    return pl.pallas_call(
        functools.partial(_outproj_kernel, alpha=alpha),
        grid=(n // tm,),
        in_specs=[row(o_fox), row(o_nsa), row(o_dsa), row(x2d), full(w_f), full(w_n), full(w_d), full(g), full(b)],
        out_specs=pl.BlockSpec((tm, D_MODEL), lambda i: (i, 0)),
        out_shape=jax.ShapeDtypeStruct((n, D_MODEL), F32),
        compiler_params=_cparams(("parallel",)),
    )(o_fox, o_nsa, o_dsa, x2d, w_f, w_n, w_d, g, b)


def _router_kernel(x_ref, wr_ref, br_ref, g_ref):
    tn = x_ref.shape[0]
    logits = _nt_dot(wr_ref[...], x_ref[...].astype(BF16))
    s = 1.0 / (1.0 + jnp.exp(-logits))
    sb = s + br_ref[...]
    low = jnp.float32(-3e38)

    grp = []
    for gi in range(N_GROUPS):
        blk = sb[gi * GROUP_SIZE:(gi + 1) * GROUP_SIZE]
        m1 = jnp.max(blk, axis=0, keepdims=True)
        is_max = blk == m1
        n_max = jnp.sum(is_max.astype(F32), axis=0, keepdims=True)
        m2 = jnp.max(jnp.where(is_max, low, blk), axis=0, keepdims=True)
        grp.append(m1 + jnp.where(n_max >= 2.0, m1, m2))
    masked = []
    for gi in range(N_GROUPS):
        rank = jnp.zeros((1, tn), F32)
        for g2 in range(N_GROUPS):
            if g2 == gi:
                continue
            beats = (grp[g2] > grp[gi]) | ((grp[g2] == grp[gi]) if g2 < gi else False)
            rank = rank + beats.astype(F32)
        keep = rank < float(TOPK_GROUPS)
        masked.append(jnp.where(keep, sb[gi * GROUP_SIZE:(gi + 1) * GROUP_SIZE], NEG))
    masked = jnp.concatenate(masked, axis=0)

    eidx = lax.broadcasted_iota(I32, (N_EXPERTS, tn), 0)
    rank = jnp.zeros((N_EXPERTS, tn), F32)
    for e2 in range(N_EXPERTS):
        row = masked[e2:e2 + 1]
        beats = (row > masked) | ((row == masked) & (e2 < eidx))
        rank = rank + beats.astype(F32)
    gw = jnp.where(rank < float(MOE_TOPK), s, 0.0)
    g_ref[...] = gw / jnp.sum(gw, axis=0, keepdims=True) * ROUTED_SCALE


def _router(x2d, wr_t, br_col):
    n = x2d.shape[0]
    tn = min(512, n)
    return pl.pallas_call(
        _router_kernel,
        grid=(n // tn,),
        in_specs=[pl.BlockSpec((tn, D_MODEL), lambda i: (i, 0)),
                  pl.BlockSpec(wr_t.shape, lambda i: (0, 0)),
                  pl.BlockSpec(br_col.shape, lambda i: (0, 0))],
        out_specs=pl.BlockSpec((N_EXPERTS, tn), lambda i: (0, i)),
        out_shape=jax.ShapeDtypeStruct((N_EXPERTS, n), F32),
        compiler_params=_cparams(("parallel",)),
    )(x2d, wr_t, br_col)


def _silu(x):
    return x / (1.0 + jnp.exp(-x))


def _moe_kernel(x_ref, gt_ref, wg_ref, wu_ref, wd_ref, sg_ref, su_ref, sd_ref, lg_ref, lb_ref, o_ref,
                xb_ref, acc_ref, *, alpha):
    e = pl.program_id(1)

    @pl.when(e == 0)
    def _():
        xb = x_ref[...].astype(BF16)
        xb_ref[...] = xb
        h = _silu(_dot(xb, sg_ref[...])) * _dot(xb, su_ref[...])
        acc_ref[...] = _dot(h.astype(BF16), sd_ref[...])

    xb = xb_ref[...]
    h = (_silu(_dot(xb, wg_ref[0])) * _dot(xb, wu_ref[0])).astype(BF16)
    gates = gt_ref[...]
    lane = lax.broadcasted_iota(I32, gates.shape, 1)
    gcol = jnp.sum(jnp.where(lane == e, gates, 0.0), axis=1, keepdims=True)
    chunk = 256
    for c in range(D_MODEL // chunk):
        cs = slice(c * chunk, (c + 1) * chunk)
        acc_ref[:, cs] += gcol * _dot(h, wd_ref[0, :, cs])

    @pl.when(e == pl.num_programs(1) - 1)
    def _():
        o_ref[...] = _layer_norm(alpha * x_ref[...] + acc_ref[...], lg_ref[...], lb_ref[...])


def _moe_ln(x2d, gates, w_gate, w_up, w_down, ws_gate, ws_up, ws_down, g, b, alpha):
    n = x2d.shape[0]
    tn = min(1024, n)
    full = lambda a: pl.BlockSpec(a.shape, lambda i, e: (0,) * a.ndim)
    return pl.pallas_call(
        functools.partial(_moe_kernel, alpha=alpha),
        grid=(n // tn, N_EXPERTS),
        in_specs=[pl.BlockSpec((tn, D_MODEL), lambda i, e: (i, 0)),
                  pl.BlockSpec((tn, N_EXPERTS), lambda i, e: (i, 0)),
                  pl.BlockSpec((1, D_MODEL, EXPERT_DIM), lambda i, e: (e, 0, 0)),
                  pl.BlockSpec((1, D_MODEL, EXPERT_DIM), lambda i, e: (e, 0, 0)),
                  pl.BlockSpec((1, EXPERT_DIM, D_MODEL), lambda i, e: (e, 0, 0)),
                  full(ws_gate), full(ws_up), full(ws_down), full(g), full(b)],
        out_specs=pl.BlockSpec((tn, D_MODEL), lambda i, e: (i, 0)),
        out_shape=jax.ShapeDtypeStruct((n, D_MODEL), F32),
        scratch_shapes=[pltpu.VMEM((tn, D_MODEL), BF16), pltpu.VMEM((tn, D_MODEL), F32)],
        compiler_params=_cparams(("parallel", "arbitrary")),
    )(x2d, gates, w_gate, w_up, w_down, ws_gate, ws_up, ws_down, g, b)


def _gather_cols(w, order):
    parts = []
    for item in order:
        if isinstance(item, tuple):
            parts.append(jnp.zeros(w.shape[:-1] + (item[1],), w.dtype))
        else:
            off, width = SEG_OFF[item]
            parts.append(w[..., off:off + width])
    return jnp.concatenate(parts, axis=-1)


def _rotate_half_cols(w):
    lead = w.shape[:-1]
    w4 = w.reshape(lead + (-1, 2, HEAD_DIM // 2))
    return jnp.concatenate([-w4[..., 1:2, :], w4[..., 0:1, :]], axis=-2).reshape(w.shape)


def _rope_tables(seq, width):
    half = HEAD_DIM // 2
    inv_freq = ROPE_THETA ** (-jnp.arange(half, dtype=F32) * (2.0 / HEAD_DIM))
    ang = jnp.arange(seq).astype(F32)[:, None] * inv_freq[None, :]
    reps = width // half
    return jnp.tile(jnp.cos(ang), (1, reps)), jnp.tile(jnp.sin(ang), (1, reps))


def _token_rows(tok, b, seq):
    t4 = tok.reshape(b, seq, NSA_KV_HEADS, HEAD_DIM).transpose(0, 2, 1, 3)
    return t4.reshape(b * NSA_KV_HEADS, seq // CMP_STRIDE, CMP_STRIDE * HEAD_DIM)


def kernel(x, w_in, fox_forget_bias, cmp_pos_k, cmp_w1_k, cmp_w2_k, cmp_pos_v, cmp_w1_v, cmp_w2_v, w_out,
           ln1_g, ln1_b, w_router, b_router, w_gate, w_up, w_down, ws_gate, ws_up, ws_down, ln2_g, ln2_b):
    b, seq, dm = x.shape
    depth = w_in.shape[0]
    n = b * seq
    alpha = float((2 * depth) ** 0.25)

    w_plain = _gather_cols(w_in, PLAIN_ORDER).astype(BF16)
    w_rope = _gather_cols(w_in, ROPE_ORDER)
    w_rope_a = w_rope.astype(BF16)
    w_rope_b = _rotate_half_cols(w_rope).astype(BF16)
    cos_t, sin_t = _rope_tables(seq, 256)
    fw, nw = FOX_HEADS * HEAD_DIM, NSA_HEADS * HEAD_DIM
    w_out_b = w_out.astype(BF16)

    x2d = x.reshape(n, dm)
    for l in range(depth):
        hp, hr = _project(x2d, w_plain[l], w_rope_a[l], w_rope_b[l], cos_t, sin_t, seq)
        hp3 = hp.reshape(b, seq, PLAIN_W)
        hr3 = hr.reshape(b, seq, ROPE_W)

        f_logit = hp3[:, :, SMALL_COL + SMALL_F:SMALL_COL + SMALL_F + FOX_HEADS]
        c_rows = _fox_cumsum(f_logit.transpose(0, 2, 1), fox_forget_bias[l].reshape(FOX_HEADS, 1))
        o_fox = _fox_attention(hp3, c_rows.transpose(0, 2, 1), c_rows)

        kvw = NSA_KV_HEADS * HEAD_DIM
        rk = _token_rows(hr3[:, :, ROPE_POS['nsa_kc']:ROPE_POS['nsa_kc'] + kvw], b, seq)
        rv = _token_rows(hp3[:, :, PLAIN_POS['nsa_vc']:PLAIN_POS['nsa_vc'] + kvw], b, seq)
        k_cmp, v_cmp = _nsa_compress(
            rk, rv, cmp_pos_k[l].reshape(1, -1), cmp_pos_v[l].reshape(1, -1),
            cmp_w1_k[l].astype(BF16), cmp_w2_k[l].astype(BF16), cmp_w1_v[l].astype(BF16), cmp_w2_v[l].astype(BF16))
        n_rows = seq // CMP_STRIDE
        o_nsa = _nsa_attention(hr3, hp3, k_cmp.reshape(b, NSA_KV_HEADS, n_rows, HEAD_DIM),
                               v_cmp.reshape(b, NSA_KV_HEADS, n_rows, HEAD_DIM))

        o_dsa = _dsa_attention(hr3, hp3)

        x2d = _outproj_ln(o_fox.reshape(n, fw), o_nsa.reshape(n, nw), o_dsa.reshape(n, nw), x2d,
                          w_out_b[l, :fw], w_out_b[l, fw:fw + nw], w_out_b[l, fw + nw:],
                          ln1_g[l].reshape(1, dm), ln1_b[l].reshape(1, dm), alpha)

        gates_t = _router(x2d, w_router[l].T.astype(BF16), b_router[l].reshape(N_EXPERTS, 1))
        x2d = _moe_ln(x2d, gates_t.T, w_gate[l].astype(BF16), w_up[l].astype(BF16), w_down[l].astype(BF16),
                      ws_gate[l].astype(BF16), ws_up[l].astype(BF16), ws_down[l].astype(BF16),
                      ln2_g[l].reshape(1, dm), ln2_b[l].reshape(1, dm), alpha)
    return x2d.reshape(b, seq, dm)
```

```python
import functools

import numpy as np
import jax
import jax.numpy as jnp
from jax import lax
from jax.experimental import pallas as pl
from jax.experimental.pallas import tpu as pltpu

D_MODEL = 1024
HEAD_DIM = 64
FOX_HEADS = 4
NSA_HEADS = 6
NSA_KV_HEADS = 2
NSA_REP = NSA_HEADS // NSA_KV_HEADS
DSA_HEADS = 6
ROPE_THETA = 10000.0
CMP_LEN = 32
CMP_STRIDE = 16
CMP_HIDDEN = 2 * HEAD_DIM
SEL_BLOCK = 64
SEL_TOPN = 16
WINDOW = 512
IDX_HEADS = 4
IDX_DIM = 64
DSA_TOPK = 256
N_EXPERTS = 64
N_GROUPS = 8
GROUP_SIZE = N_EXPERTS // N_GROUPS
TOPK_GROUPS = 4
MOE_TOPK = 8
EXPERT_DIM = 256
SHARED_DIM = 256
ROUTED_SCALE = 2.5
LN_EPS = 1e-5
NEG = -1e30
FORCE = 1e6
SCALE = HEAD_DIM ** -0.5

F32 = jnp.float32
BF16 = jnp.bfloat16
I32 = jnp.int32

VMEM_LIMIT_BYTES = 52 * 1024 * 1024
LANES = 128
KEY_STEP = 512

SEGMENTS = (
    ('fox_q', FOX_HEADS * HEAD_DIM), ('fox_k', FOX_HEADS * HEAD_DIM),
    ('fox_v', FOX_HEADS * HEAD_DIM), ('fox_f', FOX_HEADS),
    ('nsa_q', NSA_HEADS * HEAD_DIM),
    ('nsa_kc', NSA_KV_HEADS * HEAD_DIM), ('nsa_vc', NSA_KV_HEADS * HEAD_DIM),
    ('nsa_ks', NSA_KV_HEADS * HEAD_DIM), ('nsa_vs', NSA_KV_HEADS * HEAD_DIM),
    ('nsa_kw', NSA_KV_HEADS * HEAD_DIM), ('nsa_vw', NSA_KV_HEADS * HEAD_DIM),
    ('nsa_g', 3 * NSA_HEADS),
    ('dsa_q', DSA_HEADS * HEAD_DIM), ('dsa_k', HEAD_DIM), ('dsa_v', HEAD_DIM),
    ('idx_q', IDX_HEADS * IDX_DIM), ('idx_k', IDX_DIM), ('idx_w', IDX_HEADS),
)
SEG_OFF = {}
_off = 0
for _name, _width in SEGMENTS:
    SEG_OFF[_name] = (_off, _width)
    _off += _width
N_IN = _off

PLAIN_ORDER = ('fox_q', 'fox_k', 'fox_v', 'nsa_vc', 'nsa_vs', 'nsa_vw', 'dsa_v', (None, 64),
               'fox_f', 'nsa_g', 'idx_w', (None, 2 * LANES - FOX_HEADS - 3 * NSA_HEADS - IDX_HEADS))
ROPE_ORDER = ('nsa_q', 'nsa_kc', 'nsa_ks', 'nsa_kw', 'dsa_q', 'dsa_k', 'idx_k', 'idx_q')


def _layout(order):
    pos, off = {}, 0
    for item in order:
        if isinstance(item, tuple):
            off += item[1]
        else:
            pos[item] = off
            off += SEG_OFF[item][1]
    return pos, off


PLAIN_POS, PLAIN_W = _layout(PLAIN_ORDER)
ROPE_POS, ROPE_W = _layout(ROPE_ORDER)
SMALL_COL = PLAIN_POS['fox_f']
SMALL_F = 0
SMALL_G = FOX_HEADS
SMALL_W = FOX_HEADS + 3 * NSA_HEADS


def _cparams(sem):
    return pltpu.CompilerParams(dimension_semantics=sem, vmem_limit_bytes=VMEM_LIMIT_BYTES)


def _nt_dot(a, b):
    return lax.dot_general(a, b, (((1,), (1,)), ((), ())), preferred_element_type=F32)


def _dot(a, b):
    return jnp.dot(a, b, preferred_element_type=F32)


def _softmax_rows(s):
    m = jnp.max(s, axis=-1, keepdims=True)
    p = jnp.exp(s - m)
    return p, jnp.sum(p, axis=-1, keepdims=True)


def _for_key_extent(q_last, seq, body, step=KEY_STEP):
    n = seq // step
    if n <= 1:
        body(seq)
        return
    c = q_last // step
    for i in range(n):
        pl.when(c == i)(functools.partial(body, (i + 1) * step))


def _layer_norm(z, g, b):
    mu = jnp.mean(z, axis=-1, keepdims=True)
    zc = z - mu
    var = jnp.mean(zc * zc, axis=-1, keepdims=True)
    return zc * lax.rsqrt(var + LN_EPS) * g + b


def _proj_kernel(x_ref, w_ref, o_ref, xb_ref):
    @pl.when(pl.program_id(1) == 0)
    def _():
        xb_ref[...] = x_ref[...].astype(BF16)

    o_ref[...] = _dot(xb_ref[...], w_ref[...])


def _proj_rope_kernel(x_ref, wa_ref, wb_ref, cos_ref, sin_ref, o_ref, xb_ref):
    @pl.when(pl.program_id(1) == 0)
    def _():
        xb_ref[...] = x_ref[...].astype(BF16)

    xb = xb_ref[...]
    o_ref[...] = _dot(xb, wa_ref[...]) * cos_ref[...] + _dot(xb, wb_ref[...]) * sin_ref[...]


def _project(x2d, w_plain, w_a, w_b, cos_t, sin_t, seq):
    n = x2d.shape[0]
    tm = min(1024, seq)
    tc = 256
    plain = pl.pallas_call(
        _proj_kernel,
        grid=(n // tm, PLAIN_W // tc),
        in_specs=[pl.BlockSpec((tm, D_MODEL), lambda i, j: (i, 0)),
                  pl.BlockSpec((D_MODEL, tc), lambda i, j: (0, j))],
        out_specs=pl.BlockSpec((tm, tc), lambda i, j: (i, j)),
        out_shape=jax.ShapeDtypeStruct((n, PLAIN_W), F32),
        scratch_shapes=[pltpu.VMEM((tm, D_MODEL), BF16)],
        compiler_params=_cparams(("parallel", "arbitrary")),
    )(x2d, w_plain)
    nt = seq // tm
    roped = pl.pallas_call(
        _proj_rope_kernel,
        grid=(n // tm, ROPE_W // tc),
        in_specs=[pl.BlockSpec((tm, D_MODEL), lambda i, j: (i, 0)),
                  pl.BlockSpec((D_MODEL, tc), lambda i, j: (0, j)),
                  pl.BlockSpec((D_MODEL, tc), lambda i, j: (0, j)),
                  pl.BlockSpec((tm, tc), lambda i, j: (i % nt, 0)),
                  pl.BlockSpec((tm, tc), lambda i, j: (i % nt, 0))],
        out_specs=pl.BlockSpec((tm, tc), lambda i, j: (i, j)),
        out_shape=jax.ShapeDtypeStruct((n, ROPE_W), F32),
        scratch_shapes=[pltpu.VMEM((tm, D_MODEL), BF16)],
        compiler_params=_cparams(("parallel", "arbitrary")),
    )(x2d, w_a, w_b, cos_t, sin_t)
    return plain, roped


def _fox_cum_kernel(f_ref, fb_ref, o_ref):
    x = f_ref[0] + fb_ref[...]
    c = jnp.minimum(x, 0.0) - jnp.log1p(jnp.exp(-jnp.abs(x)))
    seq = c.shape[-1]
    lane = lax.broadcasted_iota(I32, c.shape, 1)
    sh = 1
    while sh < seq:
        c = c + jnp.where(lane >= sh, pltpu.roll(c, sh, 1), 0.0)
        sh *= 2
    o_ref[0] = c


def _fox_cumsum(f_rows, f_bias):
    b, h, seq = f_rows.shape
    return pl.pallas_call(
        _fox_cum_kernel,
        grid=(b,),
        in_specs=[pl.BlockSpec((1, h, seq), lambda i: (i, 0, 0)),
                  pl.BlockSpec((h, 1), lambda i: (0, 0))],
        out_specs=pl.BlockSpec((1, h, seq), lambda i: (i, 0, 0)),
        out_shape=jax.ShapeDtypeStruct((b, h, seq), F32),
        compiler_params=_cparams(("parallel",)),
    )(f_rows, f_bias)


def _fox_kernel(q_ref, k_ref, v_ref, cc_ref, cr_ref, o_ref, *, tq, seq):
    q0 = pl.program_id(1) * tq

    def body(klen):
        qpos = q0 + lax.broadcasted_iota(I32, (tq, klen), 0)
        causal = lax.broadcasted_iota(I32, (tq, klen), 1) <= qpos
        for h in range(FOX_HEADS):
            sl = slice(h * HEAD_DIM, (h + 1) * HEAD_DIM)
            qh = (q_ref[0, :, sl] * SCALE).astype(BF16)
            kh = k_ref[0, :klen, sl].astype(BF16)
            vh = v_ref[0, :klen, sl].astype(BF16)
            s = _nt_dot(qh, kh) + (cc_ref[0, :, h:h + 1] - cr_ref[0, h:h + 1, :klen])
            p, l = _softmax_rows(jnp.where(causal, s, NEG))
            o_ref[0, :, sl] = (_dot(p.astype(BF16), vh) / l).astype(o_ref.dtype)

    _for_key_extent(q0 + tq - 1, seq, body, step=tq)


def _fox_attention(hp3, c_cols, c_rows):
    b, seq, _ = hp3.shape
    tq = min(256, seq)
    w = FOX_HEADS * HEAD_DIM
    return pl.pallas_call(
        functools.partial(_fox_kernel, tq=tq, seq=seq),
        grid=(b, seq // tq),
        in_specs=[pl.BlockSpec((1, tq, w), lambda i, j: (i, j, PLAIN_POS['fox_q'] // w)),
                  pl.BlockSpec((1, seq, w), lambda i, j: (i, 0, PLAIN_POS['fox_k'] // w)),
                  pl.BlockSpec((1, seq, w), lambda i, j: (i, 0, PLAIN_POS['fox_v'] // w)),
                  pl.BlockSpec((1, tq, FOX_HEADS), lambda i, j: (i, j, 0)),
                  pl.BlockSpec((1, FOX_HEADS, seq), lambda i, j: (i, 0, 0))],
        out_specs=pl.BlockSpec((1, tq, w), lambda i, j: (i, j, 0)),
        out_shape=jax.ShapeDtypeStruct((b, seq, w), BF16),
        compiler_params=_cparams(("parallel", "arbitrary")),
    )(hp3, hp3, hp3, c_cols, c_rows)


def _gelu_tanh(x):
    return 0.5 * x * (1.0 + jnp.tanh(np.float32(np.sqrt(2.0 / np.pi)) * (x + 0.044715 * (x * x * x))))


def _compress_kernel(rk_ref, rv_ref, pek_ref, pev_ref, w1k_ref, w2k_ref, w1v_ref, w2v_ref, ok_ref, ov_ref):
    half = CMP_STRIDE * HEAD_DIM

    def one(r_ref, pe_ref, w1_ref, w2_ref, o_ref):
        r = r_ref[0]
        n_rows = r.shape[0]
        lo = _dot((r + pe_ref[:, :half]).astype(BF16), w1_ref[:half, :])
        hi = _dot((r + pe_ref[:, half:]).astype(BF16), w1_ref[half:, :])
        hid = _gelu_tanh(lo + pltpu.roll(hi, n_rows - 1, 0))
        o_ref[0] = _dot(hid.astype(BF16), w2_ref[...])

    one(rk_ref, pek_ref, w1k_ref, w2k_ref, ok_ref)
    one(rv_ref, pev_ref, w1v_ref, w2v_ref, ov_ref)


def _nsa_compress(rk, rv, pek, pev, w1k, w2k, w1v, w2v):
    bg, rows, width = rk.shape
    tok = pl.BlockSpec((1, rows, width), lambda i: (i, 0, 0))
    full = lambda a: pl.BlockSpec(a.shape, lambda i: (0,) * a.ndim)
    out = pl.BlockSpec((1, rows, HEAD_DIM), lambda i: (i, 0, 0))
    return pl.pallas_call(
        _compress_kernel,
        grid=(bg,),
        in_specs=[tok, tok, full(pek), full(pev), full(w1k), full(w2k), full(w1v), full(w2v)],
        out_specs=[out, out],
        out_shape=[jax.ShapeDtypeStruct((bg, rows, HEAD_DIM), F32)] * 2,
        compiler_params=_cparams(("parallel",)),
    )(rk, rv, pek, pev, w1k, w2k, w1v, w2v)


def _split3_nt_dot(b01, a):
    a1 = a.astype(BF16)
    r1 = a - a1.astype(F32)
    a2 = r1.astype(BF16)
    a3 = (r1 - a2.astype(F32)).astype(BF16)
    return _nt_dot(b01, a1) + _nt_dot(b01, a2) + _nt_dot(b01, a3)


def _nsa_kernel(q_ref, kc_ref, vc_ref, ks_ref, kw_ref, vs_ref, vw_ref, sm_ref, o_ref, oslc_ref, *, tq, seq, wlen):
    q0 = pl.program_id(1) * tq
    n_c = kc_ref.shape[2]
    n_s = seq // SEL_BLOCK
    rows = NSA_REP * tq
    tcol = q0 + lax.broadcasted_iota(I32, (tq, 1), 0)
    tcol_r = jnp.concatenate([tcol] * NSA_REP, axis=0)

    cidx = lax.broadcasted_iota(I32, (rows, n_c), 1)
    vis_r = (cidx * CMP_STRIDE + (CMP_LEN - 1)) <= tcol_r
    any_vis_r = (tcol_r >= (CMP_LEN - 1)).astype(F32)
    oc = lax.broadcasted_iota(I32, (n_s, n_c), 1) * CMP_STRIDE
    ob = lax.broadcasted_iota(I32, (n_s, n_c), 0) * SEL_BLOCK
    overlap_t = ((oc < ob + SEL_BLOCK) & (oc + CMP_LEN > ob)).astype(BF16)
    jj_t = lax.broadcasted_iota(I32, (n_s, tq), 0)
    blk_t = (q0 + lax.broadcasted_iota(I32, (1, tq), 1)) // SEL_BLOCK
    w0 = pl.multiple_of(jnp.maximum(q0 + tq - wlen, 0), 8)
    dist = tcol_r - (w0 + lax.broadcasted_iota(I32, (rows, wlen), 1))
    band_r = (dist >= 0) & (dist < WINDOW)
    gates = sm_ref[0]

    qs_all, qs_aug_all, o_cmp_all = [], [], []
    for g in range(NSA_KV_HEADS):
        qs = (jnp.concatenate(
            [q_ref[0, :, (g * NSA_REP + r) * HEAD_DIM:(g * NSA_REP + r + 1) * HEAD_DIM] for r in range(NSA_REP)],
            axis=0) * SCALE).astype(BF16)

        s_c = _nt_dot(qs, kc_ref[0, g].astype(BF16))
        p_c, l_c = _softmax_rows(jnp.where(vis_r, s_c, NEG))
        p_c = p_c / l_c * any_vis_r
        o_cmp_all.append(_dot(p_c.astype(BF16), vc_ref[0, g].astype(BF16)))

        p_sum = p_c[0:tq]
        for r in range(1, NSA_REP):
            p_sum = p_sum + p_c[r * tq:(r + 1) * tq]
        imp = _split3_nt_dot(overlap_t, p_sum)
        forced = (jj_t == 0) | (jj_t == blk_t) | (jj_t == blk_t - 1)
        imp = jnp.where(jj_t <= blk_t, jnp.where(forced, FORCE, imp), -1.0)
        rank = jnp.zeros((n_s, tq), F32)
        for j2 in range(n_s):
            row = imp[j2:j2 + 1, :]
            beats = (row > imp) | ((row == imp) & (j2 < jj_t))
            rank = rank + beats.astype(F32)
        sel = (rank < float(min(SEL_TOPN, n_s))) & (jj_t <= blk_t)
        bias_t = jnp.concatenate([jnp.where(sel, 0.0, NEG), jnp.zeros((tq - n_s, tq), F32)], axis=0)
        sel_bias = bias_t.T[:, :LANES - HEAD_DIM]
        qs_all.append(qs)
        qs_aug_all.append(jnp.concatenate([qs, jnp.concatenate([sel_bias] * NSA_REP, axis=0).astype(BF16)], axis=1))

    def slc_body(klen):
        block_onehot = (lax.broadcasted_iota(I32, (klen, LANES - HEAD_DIM), 0) // SEL_BLOCK
                        == lax.broadcasted_iota(I32, (klen, LANES - HEAD_DIM), 1)).astype(BF16)
        causal_r = lax.broadcasted_iota(I32, (rows, klen), 1) <= tcol_r
        for g in range(NSA_KV_HEADS):
            ksl = slice(g * HEAD_DIM, (g + 1) * HEAD_DIM)
            k_aug = jnp.concatenate([ks_ref[0, :klen, ksl].astype(BF16), block_onehot], axis=1)
            p_s, l_s = _softmax_rows(jnp.where(causal_r, _nt_dot(qs_aug_all[g], k_aug), NEG))
            oslc_ref[g] = _dot(p_s.astype(BF16), vs_ref[0, :klen, ksl].astype(BF16)) / l_s

    _for_key_extent(q0 + tq - 1, seq, slc_body)

    for g in range(NSA_KV_HEADS):
        ksl = slice(g * HEAD_DIM, (g + 1) * HEAD_DIM)
        o_cmp, o_slc = o_cmp_all[g], oslc_ref[g]

        kw = kw_ref[0, pl.ds(w0, wlen), ksl].astype(BF16)
        vw = vw_ref[0, pl.ds(w0, wlen), ksl].astype(BF16)
        p_w, l_w = _softmax_rows(jnp.where(band_r, _nt_dot(qs_all[g], kw), NEG))
        o_win = _dot(p_w.astype(BF16), vw) / l_w

        for r in range(NSA_REP):
            head = g * NSA_REP + r
            rs = slice(r * tq, (r + 1) * tq)

            def gate(branch, head=head):
                col = SMALL_G + branch * NSA_HEADS + head
                return 1.0 / (1.0 + jnp.exp(-gates[:, col:col + 1]))

            out = gate(0) * o_cmp[rs] + gate(1) * o_slc[rs] + gate(2) * o_win[rs]
            o_ref[0, :, head * HEAD_DIM:(head + 1) * HEAD_DIM] = out.astype(o_ref.dtype)


def _nsa_attention(hr3, hp3, k_cmp, v_cmp):
    b, seq, _ = hr3.shape
    tq = min(128, seq)
    wlen = min(WINDOW + tq, seq)
    qw = NSA_HEADS * HEAD_DIM
    kvw = NSA_KV_HEADS * HEAD_DIM
    n_c = k_cmp.shape[2]
    cmp_spec = pl.BlockSpec((1, NSA_KV_HEADS, n_c, HEAD_DIM), lambda i, j: (i, 0, 0, 0))

    def seq_spec(col):
        return pl.BlockSpec((1, seq, kvw), lambda i, j: (i, 0, col // kvw))

    return pl.pallas_call(
        functools.partial(_nsa_kernel, tq=tq, seq=seq, wlen=wlen),
        grid=(b, seq // tq),
        in_specs=[pl.BlockSpec((1, tq, qw), lambda i, j: (i, j, ROPE_POS['nsa_q'] // qw)),
                  cmp_spec, cmp_spec,
                  seq_spec(ROPE_POS['nsa_ks']), seq_spec(ROPE_POS['nsa_kw']),
                  seq_spec(PLAIN_POS['nsa_vs']), seq_spec(PLAIN_POS['nsa_vw']),
                  pl.BlockSpec((1, tq, LANES), lambda i, j: (i, j, SMALL_COL // LANES))],
        out_specs=pl.BlockSpec((1, tq, qw), lambda i, j: (i, j, 0)),
        out_shape=jax.ShapeDtypeStruct((b, seq, qw), BF16),
        scratch_shapes=[pltpu.VMEM((NSA_KV_HEADS, NSA_REP * tq, HEAD_DIM), F32)],
        compiler_params=_cparams(("parallel", "arbitrary")),
    )(hr3, k_cmp, v_cmp, hr3, hr3, hp3, hp3, hp3)


def _row_count(mask):
    return jnp.sum(mask.astype(F32), axis=-1, keepdims=True)


NEG_KEY = int(np.float32(NEG).view(np.int32)) ^ 0x7FFFFFFF


def _dsa_kernel(q_ref, kk_ref, v_ref, iq_ref, sm_ref, o_ref, key_ref, bias_ref, *, tq, seq, topk):
    q0 = pl.program_id(1) * tq
    tcol = q0 + lax.broadcasted_iota(I32, (tq, 1), 0)
    w_idx = sm_ref[0, :, SMALL_W:SMALL_W + IDX_HEADS] * (IDX_HEADS ** -0.5) * (IDX_DIM ** -0.5)
    kf = float(topk)
    sub = min(tq, LANES)

    def body(klen):
        ik = kk_ref[0, :klen, HEAD_DIM:2 * HEAD_DIM].astype(BF16)
        score = jnp.zeros((tq, klen), F32)
        for h in range(IDX_HEADS):
            d = _nt_dot(iq_ref[0, :, h * IDX_DIM:(h + 1) * IDX_DIM].astype(BF16), ik)
            score = score + w_idx[:, h:h + 1] * jnp.maximum(d, 0.0)
        score = jnp.where(lax.broadcasted_iota(I32, (tq, klen), 1) <= tcol, score, NEG)

        bits = lax.bitcast_convert_type(score, I32)
        key_ref[:, :klen] = jnp.where(bits < 0, bits ^ 0x7FFFFFFF, bits)

        def tau_step(i, tau):
            cand = tau + jnp.left_shift(jnp.int32(1), 31 - i)
            cnt = _row_count(key_ref[:, :klen] >= cand)
            return jnp.where(cnt >= kf, cand, tau)

        tau = lax.fori_loop(0, 32, tau_step, jnp.full((tq, 1), -2 ** 31, I32))
        need = kf - _row_count(key_ref[:, :klen] > tau)

        tri = (lax.broadcasted_iota(I32, (LANES, LANES), 0) <= lax.broadcasted_iota(I32, (LANES, LANES), 1)).astype(BF16)
        seen = jnp.zeros((tq, 1), F32)
        for blk in range(klen // LANES):
            ks = slice(blk * LANES, (blk + 1) * LANES)
            key = key_ref[:, ks]
            tie = key == tau
            count = _dot(jnp.where(tie, 1.0, 0.0).astype(BF16), tri) + seen
            seen = count[:, LANES - 1:LANES]
            kpos = blk * LANES + lax.broadcasted_iota(I32, (tq, LANES), 1)
            chosen = ((key > tau) | (tie & (count <= need))) & (kpos <= tcol)
            bias_ref[:, ks] = jnp.where(chosen, 0.0, NEG)

        k = kk_ref[0, :klen, 0:HEAD_DIM].astype(BF16)
        v = v_ref[0, :klen, 0:HEAD_DIM].astype(BF16)
        for r0 in range(0, tq, sub):
            bias = bias_ref[r0:r0 + sub, :klen]
            for h in range(DSA_HEADS):
                sl = slice(h * HEAD_DIM, (h + 1) * HEAD_DIM)
                s = _nt_dot((q_ref[0, r0:r0 + sub, sl] * SCALE).astype(BF16), k)
                p, l = _softmax_rows(s + bias)
                o_ref[0, r0:r0 + sub, sl] = (_dot(p.astype(BF16), v) / l).astype(o_ref.dtype)

    _for_key_extent(q0 + tq - 1, seq, body)


def _dsa_attention(hr3, hp3):
    b, seq, _ = hr3.shape
    tq = min(256, seq)
    topk = min(DSA_TOPK, seq // 4)
    qw = DSA_HEADS * HEAD_DIM
    iqw = IDX_HEADS * IDX_DIM
    return pl.pallas_call(
        functools.partial(_dsa_kernel, tq=tq, seq=seq, topk=topk),
        grid=(b, seq // tq),
        in_specs=[pl.BlockSpec((1, tq, qw), lambda i, j: (i, j, ROPE_POS['dsa_q'] // qw)),
                  pl.BlockSpec((1, seq, LANES), lambda i, j: (i, 0, ROPE_POS['dsa_k'] // LANES)),
                  pl.BlockSpec((1, seq, LANES), lambda i, j: (i, 0, PLAIN_POS['dsa_v'] // LANES)),
                  pl.BlockSpec((1, tq, iqw), lambda i, j: (i, j, ROPE_POS['idx_q'] // iqw)),
                  pl.BlockSpec((1, tq, LANES), lambda i, j: (i, j, SMALL_COL // LANES))],
        out_specs=pl.BlockSpec((1, tq, qw), lambda i, j: (i, j, 0)),
        out_shape=jax.ShapeDtypeStruct((b, seq, qw), BF16),
        scratch_shapes=[pltpu.VMEM((tq, seq), I32), pltpu.VMEM((tq, seq), F32)],
        compiler_params=_cparams(("parallel", "arbitrary")),
    )(hr3, hr3, hp3, hr3, hp3)


def _outproj_kernel(of_ref, on_ref, od_ref, x_ref, wf_ref, wn_ref, wd_ref, g_ref, b_ref, o_ref, *, alpha):
    mix = _dot(of_ref[...], wf_ref[...]) + _dot(on_ref[...], wn_ref[...]) + _dot(od_ref[...], wd_ref[...])
    o_ref[...] = _layer_norm(alpha * x_ref[...] + mix, g_ref[...], b_ref[...])


def _outproj_ln(o_fox, o_nsa, o_dsa, x2d, w_f, w_n, w_d, g, b, alpha):
    n = x2d.shape[0]
    tm = min(512, n)
    row = lambda a: pl.BlockSpec((tm, a.shape[1]), lambda i: (i, 0))
    full = lambda a: pl.BlockSpec(a.shape, lambda i: (0, 0))
    return pl.pallas_call(
        functools.partial(_outproj_kernel, alpha=alpha),
        grid=(n // tm,),
        in_specs=[row(o_fox), row(o_nsa), row(o_dsa), row(x2d), full(w_f), full(w_n), full(w_d), full(g), full(b)],
        out_specs=pl.BlockSpec((tm, D_MODEL), lambda i: (i, 0)),
        out_shape=jax.ShapeDtypeStruct((n, D_MODEL), F32),
        compiler_params=_cparams(("parallel",)),
    )(o_fox, o_nsa, o_dsa, x2d, w_f, w_n, w_d, g, b)


def _router_kernel(x_ref, wr_ref, br_ref, g_ref):
    tn = x_ref.shape[0]
    logits = _nt_dot(wr_ref[...], x_ref[...].astype(BF16))
    s = 1.0 / (1.0 + jnp.exp(-logits))
    sb = s + br_ref[...]
    low = jnp.float32(-3e38)

    grp = []
    for gi in range(N_GROUPS):
        blk = sb[gi * GROUP_SIZE:(gi + 1) * GROUP_SIZE]
        m1 = jnp.max(blk, axis=0, keepdims=True)
        is_max = blk == m1
        n_max = jnp.sum(is_max.astype(F32), axis=0, keepdims=True)
        m2 = jnp.max(jnp.where(is_max, low, blk), axis=0, keepdims=True)
        grp.append(m1 + jnp.where(n_max >= 2.0, m1, m2))
    masked = []
    for gi in range(N_GROUPS):
        rank = jnp.zeros((1, tn), F32)
        for g2 in range(N_GROUPS):
            if g2 == gi:
                continue
            beats = (grp[g2] > grp[gi]) | ((grp[g2] == grp[gi]) if g2 < gi else False)
            rank = rank + beats.astype(F32)
        keep = rank < float(TOPK_GROUPS)
        masked.append(jnp.where(keep, sb[gi * GROUP_SIZE:(gi + 1) * GROUP_SIZE], NEG))
    masked = jnp.concatenate(masked, axis=0)

    eidx = lax.broadcasted_iota(I32, (N_EXPERTS, tn), 0)
    rank = jnp.zeros((N_EXPERTS, tn), F32)
    for e2 in range(N_EXPERTS):
        row = masked[e2:e2 + 1]
        beats = (row > masked) | ((row == masked) & (e2 < eidx))
        rank = rank + beats.astype(F32)
    gw = jnp.where(rank < float(MOE_TOPK), s, 0.0)
    g_ref[...] = gw / jnp.sum(gw, axis=0, keepdims=True) * ROUTED_SCALE


def _router(x2d, wr_t, br_col):
    n = x2d.shape[0]
    tn = min(512, n)
    return pl.pallas_call(
        _router_kernel,
        grid=(n // tn,),
        in_specs=[pl.BlockSpec((tn, D_MODEL), lambda i: (i, 0)),
                  pl.BlockSpec(wr_t.shape, lambda i: (0, 0)),
                  pl.BlockSpec(br_col.shape, lambda i: (0, 0))],
        out_specs=pl.BlockSpec((N_EXPERTS, tn), lambda i: (0, i)),
        out_shape=jax.ShapeDtypeStruct((N_EXPERTS, n), F32),
        compiler_params=_cparams(("parallel",)),
    )(x2d, wr_t, br_col)


def _silu(x):
    return x / (1.0 + jnp.exp(-x))


def _moe_kernel(x_ref, gt_ref, wg_ref, wu_ref, wd_ref, sg_ref, su_ref, sd_ref, lg_ref, lb_ref, o_ref,
                xb_ref, acc_ref, *, alpha):
    e = pl.program_id(1)

    @pl.when(e == 0)
    def _():
        xb = x_ref[...].astype(BF16)
        xb_ref[...] = xb
        h = _silu(_dot(xb, sg_ref[...])) * _dot(xb, su_ref[...])
        acc_ref[...] = _dot(h.astype(BF16), sd_ref[...])

    xb = xb_ref[...]
    h = (_silu(_dot(xb, wg_ref[0])) * _dot(xb, wu_ref[0])).astype(BF16)
    gates = gt_ref[...]
    lane = lax.broadcasted_iota(I32, gates.shape, 1)
    gcol = jnp.sum(jnp.where(lane == e, gates, 0.0), axis=1, keepdims=True)
    chunk = 256
    for c in range(D_MODEL // chunk):
        cs = slice(c * chunk, (c + 1) * chunk)
        acc_ref[:, cs] += gcol * _dot(h, wd_ref[0, :, cs])

    @pl.when(e == pl.num_programs(1) - 1)
    def _():
        o_ref[...] = _layer_norm(alpha * x_ref[...] + acc_ref[...], lg_ref[...], lb_ref[...])


def _moe_ln(x2d, gates, w_gate, w_up, w_down, ws_gate, ws_up, ws_down, g, b, alpha):
    n = x2d.shape[0]
    tn = min(1024, n)
    full = lambda a: pl.BlockSpec(a.shape, lambda i, e: (0,) * a.ndim)
    return pl.pallas_call(
        functools.partial(_moe_kernel, alpha=alpha),
        grid=(n // tn, N_EXPERTS),
        in_specs=[pl.BlockSpec((tn, D_MODEL), lambda i, e: (i, 0)),
                  pl.BlockSpec((tn, N_EXPERTS), lambda i, e: (i, 0)),
                  pl.BlockSpec((1, D_MODEL, EXPERT_DIM), lambda i, e: (e, 0, 0)),
                  pl.BlockSpec((1, D_MODEL, EXPERT_DIM), lambda i, e: (e, 0, 0)),
                  pl.BlockSpec((1, EXPERT_DIM, D_MODEL), lambda i, e: (e, 0, 0)),
                  full(ws_gate), full(ws_up), full(ws_down), full(g), full(b)],
        out_specs=pl.BlockSpec((tn, D_MODEL), lambda i, e: (i, 0)),
        out_shape=jax.ShapeDtypeStruct((n, D_MODEL), F32),
        scratch_shapes=[pltpu.VMEM((tn, D_MODEL), BF16), pltpu.VMEM((tn, D_MODEL), F32)],
        compiler_params=_cparams(("parallel", "arbitrary")),
    )(x2d, gates, w_gate, w_up, w_down, ws_gate, ws_up, ws_down, g, b)


def _gather_cols(w, order):
    parts = []
    for item in order:
        if isinstance(item, tuple):
            parts.append(jnp.zeros(w.shape[:-1] + (item[1],), w.dtype))
        else:
            off, width = SEG_OFF[item]
            parts.append(w[..., off:off + width])
    return jnp.concatenate(parts, axis=-1)


def _rotate_half_cols(w):
    lead = w.shape[:-1]
    w4 = w.reshape(lead + (-1, 2, HEAD_DIM // 2))
    return jnp.concatenate([-w4[..., 1:2, :], w4[..., 0:1, :]], axis=-2).reshape(w.shape)


def _rope_tables(seq, width):
    half = HEAD_DIM // 2
    inv_freq = ROPE_THETA ** (-jnp.arange(half, dtype=F32) * (2.0 / HEAD_DIM))
    ang = jnp.arange(seq).astype(F32)[:, None] * inv_freq[None, :]
    reps = width // half
    return jnp.tile(jnp.cos(ang), (1, reps)), jnp.tile(jnp.sin(ang), (1, reps))


def _token_rows(tok, b, seq):
    t4 = tok.reshape(b, seq, NSA_KV_HEADS, HEAD_DIM).transpose(0, 2, 1, 3)
    return t4.reshape(b * NSA_KV_HEADS, seq // CMP_STRIDE, CMP_STRIDE * HEAD_DIM)


def kernel(x, w_in, fox_forget_bias, cmp_pos_k, cmp_w1_k, cmp_w2_k, cmp_pos_v, cmp_w1_v, cmp_w2_v, w_out,
           ln1_g, ln1_b, w_router, b_router, w_gate, w_up, w_down, ws_gate, ws_up, ws_down, ln2_g, ln2_b):
    b, seq, dm = x.shape
    depth = w_in.shape[0]
    n = b * seq
    alpha = float((2 * depth) ** 0.25)

    w_plain = _gather_cols(w_in, PLAIN_ORDER).astype(BF16)
    w_rope = _gather_cols(w_in, ROPE_ORDER)
    w_rope_a = w_rope.astype(BF16)
    w_rope_b = _rotate_half_cols(w_rope).astype(BF16)
    cos_t, sin_t = _rope_tables(seq, 256)
    fw, nw = FOX_HEADS * HEAD_DIM, NSA_HEADS * HEAD_DIM
    w_out_b = w_out.astype(BF16)

    x2d = x.reshape(n, dm)
    for l in range(depth):
        hp, hr = _project(x2d, w_plain[l], w_rope_a[l], w_rope_b[l], cos_t, sin_t, seq)
        hp3 = hp.reshape(b, seq, PLAIN_W)
        hr3 = hr.reshape(b, seq, ROPE_W)

        f_logit = hp3[:, :, SMALL_COL + SMALL_F:SMALL_COL + SMALL_F + FOX_HEADS]
        c_rows = _fox_cumsum(f_logit.transpose(0, 2, 1), fox_forget_bias[l].reshape(FOX_HEADS, 1))
        o_fox = _fox_attention(hp3, c_rows.transpose(0, 2, 1), c_rows)

        kvw = NSA_KV_HEADS * HEAD_DIM
        rk = _token_rows(hr3[:, :, ROPE_POS['nsa_kc']:ROPE_POS['nsa_kc'] + kvw], b, seq)
        rv = _token_rows(hp3[:, :, PLAIN_POS['nsa_vc']:PLAIN_POS['nsa_vc'] + kvw], b, seq)
        k_cmp, v_cmp = _nsa_compress(
            rk, rv, cmp_pos_k[l].reshape(1, -1), cmp_pos_v[l].reshape(1, -1),
            cmp_w1_k[l].astype(BF16), cmp_w2_k[l].astype(BF16), cmp_w1_v[l].astype(BF16), cmp_w2_v[l].astype(BF16))
        n_rows = seq // CMP_STRIDE
        o_nsa = _nsa_attention(hr3, hp3, k_cmp.reshape(b, NSA_KV_HEADS, n_rows, HEAD_DIM),
                               v_cmp.reshape(b, NSA_KV_HEADS, n_rows, HEAD_DIM))

        o_dsa = _dsa_attention(hr3, hp3)

        x2d = _outproj_ln(o_fox.reshape(n, fw), o_nsa.reshape(n, nw), o_dsa.reshape(n, nw), x2d,
                          w_out_b[l, :fw], w_out_b[l, fw:fw + nw], w_out_b[l, fw + nw:],
                          ln1_g[l].reshape(1, dm), ln1_b[l].reshape(1, dm), alpha)

        gates_t = _router(x2d, w_router[l].T.astype(BF16), b_router[l].reshape(N_EXPERTS, 1))
        x2d = _moe_ln(x2d, gates_t.T, w_gate[l].astype(BF16), w_up[l].astype(BF16), w_down[l].astype(BF16),
                      ws_gate[l].astype(BF16), ws_up[l].astype(BF16), ws_down[l].astype(BF16),
                      ln2_g[l].reshape(1, dm), ln2_b[l].reshape(1, dm), alpha)
    return x2d.reshape(b, seq, dm)
```

```python
import functools

import numpy as np
import jax
import jax.numpy as jnp
from jax import lax
from jax.experimental import pallas as pl
from jax.experimental.pallas import tpu as pltpu

D_MODEL = 1024
HEAD_DIM = 64
FOX_HEADS = 4
NSA_HEADS = 6
NSA_KV_HEADS = 2
NSA_REP = NSA_HEADS // NSA_KV_HEADS
DSA_HEADS = 6
ROPE_THETA = 10000.0
CMP_LEN = 32
CMP_STRIDE = 16
CMP_HIDDEN = 2 * HEAD_DIM
SEL_BLOCK = 64
SEL_TOPN = 16
WINDOW = 512
IDX_HEADS = 4
IDX_DIM = 64
DSA_TOPK = 256
N_EXPERTS = 64
N_GROUPS = 8
GROUP_SIZE = N_EXPERTS // N_GROUPS
TOPK_GROUPS = 4
MOE_TOPK = 8
EXPERT_DIM = 256
SHARED_DIM = 256
ROUTED_SCALE = 2.5
LN_EPS = 1e-5
NEG = -1e30
FORCE = 1e6
SCALE = HEAD_DIM ** -0.5

F32 = jnp.float32
BF16 = jnp.bfloat16
I32 = jnp.int32

VMEM_LIMIT_BYTES = 52 * 1024 * 1024
LANES = 128
KEY_STEP = 512
PROJ_TILE_COLS = 512

SEGMENTS = (
    ('fox_q', FOX_HEADS * HEAD_DIM), ('fox_k', FOX_HEADS * HEAD_DIM),
    ('fox_v', FOX_HEADS * HEAD_DIM), ('fox_f', FOX_HEADS),
    ('nsa_q', NSA_HEADS * HEAD_DIM),
    ('nsa_kc', NSA_KV_HEADS * HEAD_DIM), ('nsa_vc', NSA_KV_HEADS * HEAD_DIM),
    ('nsa_ks', NSA_KV_HEADS * HEAD_DIM), ('nsa_vs', NSA_KV_HEADS * HEAD_DIM),
    ('nsa_kw', NSA_KV_HEADS * HEAD_DIM), ('nsa_vw', NSA_KV_HEADS * HEAD_DIM),
    ('nsa_g', 3 * NSA_HEADS),
    ('dsa_q', DSA_HEADS * HEAD_DIM), ('dsa_k', HEAD_DIM), ('dsa_v', HEAD_DIM),
    ('idx_q', IDX_HEADS * IDX_DIM), ('idx_k', IDX_DIM), ('idx_w', IDX_HEADS),
)
SEG_OFF = {}
_off = 0
for _name, _width in SEGMENTS:
    SEG_OFF[_name] = (_off, _width)
    _off += _width
N_IN = _off

PLAIN_ORDER = ('fox_q', 'fox_k', 'fox_v', 'nsa_vc', 'nsa_vs', 'nsa_vw', 'dsa_v', (None, 64),
               'fox_f', 'nsa_g', 'idx_w', (None, 2 * LANES - FOX_HEADS - 3 * NSA_HEADS - IDX_HEADS))
ROPE_ORDER = ('nsa_q', 'nsa_kc', 'nsa_ks', 'nsa_kw', 'dsa_q', 'dsa_k', 'idx_k', 'idx_q')


def _layout(order):
    pos, off = {}, 0
    for item in order:
        if isinstance(item, tuple):
            off += item[1]
        else:
            pos[item] = off
            off += SEG_OFF[item][1]
    return pos, off


PLAIN_POS, PLAIN_W = _layout(PLAIN_ORDER)
ROPE_POS, ROPE_W = _layout(ROPE_ORDER)
SMALL_COL = PLAIN_POS['fox_f']
SMALL_F = 0
SMALL_G = FOX_HEADS
SMALL_W = FOX_HEADS + 3 * NSA_HEADS


def _cparams(sem):
    return pltpu.CompilerParams(dimension_semantics=sem, vmem_limit_bytes=VMEM_LIMIT_BYTES)


def _nt_dot(a, b):
    return lax.dot_general(a, b, (((1,), (1,)), ((), ())), preferred_element_type=F32)


def _dot(a, b):
    return jnp.dot(a, b, preferred_element_type=F32)


def _softmax_rows(s):
    m = jnp.max(s, axis=-1, keepdims=True)
    p = jnp.exp(s - m)
    return p, jnp.sum(p, axis=-1, keepdims=True)


def _for_key_extent(q_last, seq, body, step=KEY_STEP):
    n = seq // step
    if n <= 1:
        body(seq)
        return
    c = q_last // step
    for i in range(n):
        pl.when(c == i)(functools.partial(body, (i + 1) * step))


def _layer_norm(z, g, b):
    mu = jnp.mean(z, axis=-1, keepdims=True)
    zc = z - mu
    var = jnp.mean(zc * zc, axis=-1, keepdims=True)
    return zc * lax.rsqrt(var + LN_EPS) * g + b


def _proj_kernel(x_ref, w_ref, o_ref, xb_ref):
    @pl.when(pl.program_id(1) == 0)
    def _():
        xb_ref[...] = x_ref[...].astype(BF16)

    o_ref[...] = _dot(xb_ref[...], w_ref[...])


def _proj_rope_kernel(x_ref, wa_ref, wb_ref, cos_ref, sin_ref, o_ref, xb_ref):
    @pl.when(pl.program_id(1) == 0)
    def _():
        xb_ref[...] = x_ref[...].astype(BF16)

    xb = xb_ref[...]
    o_ref[...] = _dot(xb, wa_ref[...]) * cos_ref[...] + _dot(xb, wb_ref[...]) * sin_ref[...]


def _project(x2d, w_plain, w_a, w_b, cos_t, sin_t, seq):
    n = x2d.shape[0]
    tm = min(1024, seq)
    tc = cos_t.shape[1]
    plain = pl.pallas_call(
        _proj_kernel,
        grid=(n // tm, PLAIN_W // tc),
        in_specs=[pl.BlockSpec((tm, D_MODEL), lambda i, j: (i, 0)),
                  pl.BlockSpec((D_MODEL, tc), lambda i, j: (0, j))],
        out_specs=pl.BlockSpec((tm, tc), lambda i, j: (i, j)),
        out_shape=jax.ShapeDtypeStruct((n, PLAIN_W), F32),
        scratch_shapes=[pltpu.VMEM((tm, D_MODEL), BF16)],
        compiler_params=_cparams(("parallel", "arbitrary")),
    )(x2d, w_plain)
    nt = seq // tm
    roped = pl.pallas_call(
        _proj_rope_kernel,
        grid=(n // tm, ROPE_W // tc),
        in_specs=[pl.BlockSpec((tm, D_MODEL), lambda i, j: (i, 0)),
                  pl.BlockSpec((D_MODEL, tc), lambda i, j: (0, j)),
                  pl.BlockSpec((D_MODEL, tc), lambda i, j: (0, j)),
                  pl.BlockSpec((tm, tc), lambda i, j: (i % nt, 0)),
                  pl.BlockSpec((tm, tc), lambda i, j: (i % nt, 0))],
        out_specs=pl.BlockSpec((tm, tc), lambda i, j: (i, j)),
        out_shape=jax.ShapeDtypeStruct((n, ROPE_W), F32),
        scratch_shapes=[pltpu.VMEM((tm, D_MODEL), BF16)],
        compiler_params=_cparams(("parallel", "arbitrary")),
    )(x2d, w_a, w_b, cos_t, sin_t)
    return plain, roped


def _fox_cum_kernel(f_ref, fb_ref, o_ref):
    x = f_ref[0] + fb_ref[...]
    c = jnp.minimum(x, 0.0) - jnp.log1p(jnp.exp(-jnp.abs(x)))
    seq = c.shape[-1]
    lane = lax.broadcasted_iota(I32, c.shape, 1)
    sh = 1
    while sh < seq:
        c = c + jnp.where(lane >= sh, pltpu.roll(c, sh, 1), 0.0)
        sh *= 2
    o_ref[0] = c


def _fox_cumsum(f_rows, f_bias):
    b, h, seq = f_rows.shape
    return pl.pallas_call(
        _fox_cum_kernel,
        grid=(b,),
        in_specs=[pl.BlockSpec((1, h, seq), lambda i: (i, 0, 0)),
                  pl.BlockSpec((h, 1), lambda i: (0, 0))],
        out_specs=pl.BlockSpec((1, h, seq), lambda i: (i, 0, 0)),
        out_shape=jax.ShapeDtypeStruct((b, h, seq), F32),
        compiler_params=_cparams(("parallel",)),
    )(f_rows, f_bias)


def _fox_kernel(q_ref, k_ref, v_ref, cc_ref, cr_ref, o_ref, *, tq, seq):
    q0 = pl.program_id(1) * tq

    def body(klen):
        qpos = q0 + lax.broadcasted_iota(I32, (tq, klen), 0)
        causal = lax.broadcasted_iota(I32, (tq, klen), 1) <= qpos
        for h in range(FOX_HEADS):
            sl = slice(h * HEAD_DIM, (h + 1) * HEAD_DIM)
            qh = (q_ref[0, :, sl] * SCALE).astype(BF16)
            kh = k_ref[0, :klen, sl].astype(BF16)
            vh = v_ref[0, :klen, sl].astype(BF16)
            s = _nt_dot(qh, kh) + (cc_ref[0, :, h:h + 1] - cr_ref[0, h:h + 1, :klen])
            p, l = _softmax_rows(jnp.where(causal, s, NEG))
            o_ref[0, :, sl] = (_dot(p.astype(BF16), vh) / l).astype(o_ref.dtype)

    _for_key_extent(q0 + tq - 1, seq, body, step=tq)


def _fox_attention(hp3, c_cols, c_rows):
    b, seq, _ = hp3.shape
    tq = min(256, seq)
    w = FOX_HEADS * HEAD_DIM
    return pl.pallas_call(
        functools.partial(_fox_kernel, tq=tq, seq=seq),
        grid=(b, seq // tq),
        in_specs=[pl.BlockSpec((1, tq, w), lambda i, j: (i, j, PLAIN_POS['fox_q'] // w)),
                  pl.BlockSpec((1, seq, w), lambda i, j: (i, 0, PLAIN_POS['fox_k'] // w)),
                  pl.BlockSpec((1, seq, w), lambda i, j: (i, 0, PLAIN_POS['fox_v'] // w)),
                  pl.BlockSpec((1, tq, FOX_HEADS), lambda i, j: (i, j, 0)),
                  pl.BlockSpec((1, FOX_HEADS, seq), lambda i, j: (i, 0, 0))],
        out_specs=pl.BlockSpec((1, tq, w), lambda i, j: (i, j, 0)),
        out_shape=jax.ShapeDtypeStruct((b, seq, w), BF16),
        compiler_params=_cparams(("parallel", "arbitrary")),
    )(hp3, hp3, hp3, c_cols, c_rows)


def _gelu_tanh(x):
    return 0.5 * x * (1.0 + jnp.tanh(np.float32(np.sqrt(2.0 / np.pi)) * (x + 0.044715 * (x * x * x))))


def _compress_kernel(rk_ref, rv_ref, pek_ref, pev_ref, w1k_ref, w2k_ref, w1v_ref, w2v_ref, ok_ref, ov_ref):
    half = CMP_STRIDE * HEAD_DIM

    def one(r_ref, pe_ref, w1_ref, w2_ref, o_ref):
        r = r_ref[0]
        n_rows = r.shape[0]
        lo = _dot((r + pe_ref[:, :half]).astype(BF16), w1_ref[:half, :])
        hi = _dot((r + pe_ref[:, half:]).astype(BF16), w1_ref[half:, :])
        hid = _gelu_tanh(lo + pltpu.roll(hi, n_rows - 1, 0))
        o_ref[0] = _dot(hid.astype(BF16), w2_ref[...])

    one(rk_ref, pek_ref, w1k_ref, w2k_ref, ok_ref)
    one(rv_ref, pev_ref, w1v_ref, w2v_ref, ov_ref)


def _nsa_compress(rk, rv, pek, pev, w1k, w2k, w1v, w2v):
    bg, rows, width = rk.shape
    tok = pl.BlockSpec((1, rows, width), lambda i: (i, 0, 0))
    full = lambda a: pl.BlockSpec(a.shape, lambda i: (0,) * a.ndim)
    out = pl.BlockSpec((1, rows, HEAD_DIM), lambda i: (i, 0, 0))
    return pl.pallas_call(
        _compress_kernel,
        grid=(bg,),
        in_specs=[tok, tok, full(pek), full(pev), full(w1k), full(w2k), full(w1v), full(w2v)],
        out_specs=[out, out],
        out_shape=[jax.ShapeDtypeStruct((bg, rows, HEAD_DIM), F32)] * 2,
        compiler_params=_cparams(("parallel",)),
    )(rk, rv, pek, pev, w1k, w2k, w1v, w2v)


def _split3_nt_dot(b01, a):
    a1 = a.astype(BF16)
    r1 = a - a1.astype(F32)
    a2 = r1.astype(BF16)
    a3 = (r1 - a2.astype(F32)).astype(BF16)
    return _nt_dot(b01, a1) + _nt_dot(b01, a2) + _nt_dot(b01, a3)


def _nsa_kernel(q_ref, kc_ref, vc_ref, ks_ref, kw_ref, vs_ref, vw_ref, sm_ref, o_ref, oslc_ref, *, tq, seq, wlen):
    q0 = pl.program_id(1) * tq
    n_c = kc_ref.shape[2]
    n_s = seq // SEL_BLOCK
    rows = NSA_REP * tq
    tcol = q0 + lax.broadcasted_iota(I32, (tq, 1), 0)
    tcol_r = jnp.concatenate([tcol] * NSA_REP, axis=0)

    cidx = lax.broadcasted_iota(I32, (rows, n_c), 1)
    vis_r = (cidx * CMP_STRIDE + (CMP_LEN - 1)) <= tcol_r
    any_vis_r = (tcol_r >= (CMP_LEN - 1)).astype(F32)
    oc = lax.broadcasted_iota(I32, (n_s, n_c), 1) * CMP_STRIDE
    ob = lax.broadcasted_iota(I32, (n_s, n_c), 0) * SEL_BLOCK
    overlap_t = ((oc < ob + SEL_BLOCK) & (oc + CMP_LEN > ob)).astype(BF16)
    jj_t = lax.broadcasted_iota(I32, (n_s, tq), 0)
    blk_t = (q0 + lax.broadcasted_iota(I32, (1, tq), 1)) // SEL_BLOCK
    w0 = pl.multiple_of(jnp.maximum(q0 + tq - wlen, 0), 8)
    dist = tcol_r - (w0 + lax.broadcasted_iota(I32, (rows, wlen), 1))
    band_r = (dist >= 0) & (dist < WINDOW)
    gates = sm_ref[0]

    qs_all, qs_aug_all, o_cmp_all = [], [], []
    for g in range(NSA_KV_HEADS):
        qs = (jnp.concatenate(
            [q_ref[0, :, (g * NSA_REP + r) * HEAD_DIM:(g * NSA_REP + r + 1) * HEAD_DIM] for r in range(NSA_REP)],
            axis=0) * SCALE).astype(BF16)

        s_c = _nt_dot(qs, kc_ref[0, g].astype(BF16))
        p_c, l_c = _softmax_rows(jnp.where(vis_r, s_c, NEG))
        p_c = p_c / l_c * any_vis_r
        o_cmp_all.append(_dot(p_c.astype(BF16), vc_ref[0, g].astype(BF16)))

        p_sum = p_c[0:tq]
        for r in range(1, NSA_REP):
            p_sum = p_sum + p_c[r * tq:(r + 1) * tq]
        imp = _split3_nt_dot(overlap_t, p_sum)
        forced = (jj_t == 0) | (jj_t == blk_t) | (jj_t == blk_t - 1)
        imp = jnp.where(jj_t <= blk_t, jnp.where(forced, FORCE, imp), -1.0)
        rank = jnp.zeros((n_s, tq), F32)
        for j2 in range(n_s):
            row = imp[j2:j2 + 1, :]
            beats = (row > imp) | ((row == imp) & (j2 < jj_t))
            rank = rank + beats.astype(F32)
        sel = (rank < float(min(SEL_TOPN, n_s))) & (jj_t <= blk_t)
        bias_t = jnp.concatenate([jnp.where(sel, 0.0, NEG), jnp.zeros((tq - n_s, tq), F32)], axis=0)
        sel_bias = bias_t.T[:, :LANES - HEAD_DIM]
        qs_all.append(qs)
        qs_aug_all.append(jnp.concatenate([qs, jnp.concatenate([sel_bias] * NSA_REP, axis=0).astype(BF16)], axis=1))

    def slc_body(klen):
        block_onehot = (lax.broadcasted_iota(I32, (klen, LANES - HEAD_DIM), 0) // SEL_BLOCK
                        == lax.broadcasted_iota(I32, (klen, LANES - HEAD_DIM), 1)).astype(BF16)
        causal_r = lax.broadcasted_iota(I32, (rows, klen), 1) <= tcol_r
        for g in range(NSA_KV_HEADS):
            ksl = slice(g * HEAD_DIM, (g + 1) * HEAD_DIM)
            k_aug = jnp.concatenate([ks_ref[0, :klen, ksl].astype(BF16), block_onehot], axis=1)
            p_s, l_s = _softmax_rows(jnp.where(causal_r, _nt_dot(qs_aug_all[g], k_aug), NEG))
            oslc_ref[g] = _dot(p_s.astype(BF16), vs_ref[0, :klen, ksl].astype(BF16)) / l_s

    _for_key_extent(q0 + tq - 1, seq, slc_body)

    for g in range(NSA_KV_HEADS):
        ksl = slice(g * HEAD_DIM, (g + 1) * HEAD_DIM)
        o_cmp, o_slc = o_cmp_all[g], oslc_ref[g]

        kw = kw_ref[0, pl.ds(w0, wlen), ksl].astype(BF16)
        vw = vw_ref[0, pl.ds(w0, wlen), ksl].astype(BF16)
        p_w, l_w = _softmax_rows(jnp.where(band_r, _nt_dot(qs_all[g], kw), NEG))
        o_win = _dot(p_w.astype(BF16), vw) / l_w

        for r in range(NSA_REP):
            head = g * NSA_REP + r
            rs = slice(r * tq, (r + 1) * tq)

            def gate(branch, head=head):
                col = SMALL_G + branch * NSA_HEADS + head
                return 1.0 / (1.0 + jnp.exp(-gates[:, col:col + 1]))

            out = gate(0) * o_cmp[rs] + gate(1) * o_slc[rs] + gate(2) * o_win[rs]
            o_ref[0, :, head * HEAD_DIM:(head + 1) * HEAD_DIM] = out.astype(o_ref.dtype)


def _nsa_attention(hr3, hp3, k_cmp, v_cmp):
    b, seq, _ = hr3.shape
    tq = min(128, seq)
    wlen = min(WINDOW + tq, seq)
    qw = NSA_HEADS * HEAD_DIM
    kvw = NSA_KV_HEADS * HEAD_DIM
    n_c = k_cmp.shape[2]
    cmp_spec = pl.BlockSpec((1, NSA_KV_HEADS, n_c, HEAD_DIM), lambda i, j: (i, 0, 0, 0))

    def seq_spec(col):
        return pl.BlockSpec((1, seq, kvw), lambda i, j: (i, 0, col // kvw))

    return pl.pallas_call(
        functools.partial(_nsa_kernel, tq=tq, seq=seq, wlen=wlen),
        grid=(b, seq // tq),
        in_specs=[pl.BlockSpec((1, tq, qw), lambda i, j: (i, j, ROPE_POS['nsa_q'] // qw)),
                  cmp_spec, cmp_spec,
                  seq_spec(ROPE_POS['nsa_ks']), seq_spec(ROPE_POS['nsa_kw']),
                  seq_spec(PLAIN_POS['nsa_vs']), seq_spec(PLAIN_POS['nsa_vw']),
                  pl.BlockSpec((1, tq, LANES), lambda i, j: (i, j, SMALL_COL // LANES))],
        out_specs=pl.BlockSpec((1, tq, qw), lambda i, j: (i, j, 0)),
        out_shape=jax.ShapeDtypeStruct((b, seq, qw), BF16),
        scratch_shapes=[pltpu.VMEM((NSA_KV_HEADS, NSA_REP * tq, HEAD_DIM), F32)],
        compiler_params=_cparams(("parallel", "arbitrary")),
    )(hr3, k_cmp, v_cmp, hr3, hr3, hp3, hp3, hp3)


def _row_count(mask):
    return jnp.sum(mask.astype(F32), axis=-1, keepdims=True)


NEG_KEY = int(np.float32(NEG).view(np.int32)) ^ 0x7FFFFFFF


def _dsa_kernel(q_ref, kk_ref, v_ref, iq_ref, sm_ref, o_ref, key_ref, bias_ref, *, tq, seq, topk):
    q0 = pl.program_id(1) * tq
    tcol = q0 + lax.broadcasted_iota(I32, (tq, 1), 0)
    w_idx = sm_ref[0, :, SMALL_W:SMALL_W + IDX_HEADS] * (IDX_HEADS ** -0.5) * (IDX_DIM ** -0.5)
    kf = float(topk)
    sub = min(tq, LANES)

    def body(klen):
        ik = kk_ref[0, :klen, HEAD_DIM:2 * HEAD_DIM].astype(BF16)
        score = jnp.zeros((tq, klen), F32)
        for h in range(IDX_HEADS):
            d = _nt_dot(iq_ref[0, :, h * IDX_DIM:(h + 1) * IDX_DIM].astype(BF16), ik)
            score = score + w_idx[:, h:h + 1] * jnp.maximum(d, 0.0)
        score = jnp.where(lax.broadcasted_iota(I32, (tq, klen), 1) <= tcol, score, NEG)

        bits = lax.bitcast_convert_type(score, I32)
        key_ref[:, :klen] = jnp.where(bits < 0, bits ^ 0x7FFFFFFF, bits)

        def tau_step(i, tau):
            cand = tau + jnp.left_shift(jnp.int32(1), 31 - i)
            cnt = _row_count(key_ref[:, :klen] >= cand)
            return jnp.where(cnt >= kf, cand, tau)

        tau = lax.fori_loop(0, 32, tau_step, jnp.full((tq, 1), -2 ** 31, I32))
        need = kf - _row_count(key_ref[:, :klen] > tau)

        tri = (lax.broadcasted_iota(I32, (LANES, LANES), 0) <= lax.broadcasted_iota(I32, (LANES, LANES), 1)).astype(BF16)
        seen = jnp.zeros((tq, 1), F32)
        for blk in range(klen // LANES):
            ks = slice(blk * LANES, (blk + 1) * LANES)
            key = key_ref[:, ks]
            tie = key == tau
            count = _dot(jnp.where(tie, 1.0, 0.0).astype(BF16), tri) + seen
            seen = count[:, LANES - 1:LANES]
            kpos = blk * LANES + lax.broadcasted_iota(I32, (tq, LANES), 1)
            chosen = ((key > tau) | (tie & (count <= need))) & (kpos <= tcol)
            bias_ref[:, ks] = jnp.where(chosen, 0.0, NEG)

        k = kk_ref[0, :klen, 0:HEAD_DIM].astype(BF16)
        v = v_ref[0, :klen, 0:HEAD_DIM].astype(BF16)
        for r0 in range(0, tq, sub):
            bias = bias_ref[r0:r0 + sub, :klen]
            for h in range(DSA_HEADS):
                sl = slice(h * HEAD_DIM, (h + 1) * HEAD_DIM)
                s = _nt_dot((q_ref[0, r0:r0 + sub, sl] * SCALE).astype(BF16), k)
                p, l = _softmax_rows(s + bias)
                o_ref[0, r0:r0 + sub, sl] = (_dot(p.astype(BF16), v) / l).astype(o_ref.dtype)

    _for_key_extent(q0 + tq - 1, seq, body)


def _dsa_attention(hr3, hp3):
    b, seq, _ = hr3.shape
    tq = min(256, seq)
    topk = min(DSA_TOPK, seq // 4)
    qw = DSA_HEADS * HEAD_DIM
    iqw = IDX_HEADS * IDX_DIM
    return pl.pallas_call(
        functools.partial(_dsa_kernel, tq=tq, seq=seq, topk=topk),
        grid=(b, seq // tq),
        in_specs=[pl.BlockSpec((1, tq, qw), lambda i, j: (i, j, ROPE_POS['dsa_q'] // qw)),
                  pl.BlockSpec((1, seq, LANES), lambda i, j: (i, 0, ROPE_POS['dsa_k'] // LANES)),
                  pl.BlockSpec((1, seq, LANES), lambda i, j: (i, 0, PLAIN_POS['dsa_v'] // LANES)),
                  pl.BlockSpec((1, tq, iqw), lambda i, j: (i, j, ROPE_POS['idx_q'] // iqw)),
                  pl.BlockSpec((1, tq, LANES), lambda i, j: (i, j, SMALL_COL // LANES))],
        out_specs=pl.BlockSpec((1, tq, qw), lambda i, j: (i, j, 0)),
        out_shape=jax.ShapeDtypeStruct((b, seq, qw), BF16),
        scratch_shapes=[pltpu.VMEM((tq, seq), I32), pltpu.VMEM((tq, seq), F32)],
        compiler_params=_cparams(("parallel", "arbitrary")),
    )(hr3, hr3, hp3, hr3, hp3)


def _outproj_kernel(of_ref, on_ref, od_ref, x_ref, wf_ref, wn_ref, wd_ref, g_ref, b_ref, o_ref, *, alpha):
    mix = _dot(of_ref[...], wf_ref[...]) + _dot(on_ref[...], wn_ref[...]) + _dot(od_ref[...], wd_ref[...])
    o_ref[...] = _layer_norm(alpha * x_ref[...] + mix, g_ref[...], b_ref[...])


def _outproj_ln(o_fox, o_nsa, o_dsa, x2d, w_f, w_n, w_d, g, b, alpha):
    n = x2d.shape[0]
    tm = min(512, n)
    row = lambda a: pl.BlockSpec((tm, a.shape[1]), lambda i: (i, 0))
    full = lambda a: pl.BlockSpec(a.shape, lambda i: (0, 0))
    return pl.pallas_call(
        functools.partial(_outproj_kernel, alpha=alpha),
        grid=(n // tm,),
        in_specs=[row(o_fox), row(o_nsa), row(o_dsa), row(x2d), full(w_f), full(w_n), full(w_d), full(g), full(b)],
        out_specs=pl.BlockSpec((tm, D_MODEL), lambda i: (i, 0)),
        out_shape=jax.ShapeDtypeStruct((n, D_MODEL), F32),
        compiler_params=_cparams(("parallel",)),
    )(o_fox, o_nsa, o_dsa, x2d, w_f, w_n, w_d, g, b)


def _router_kernel(x_ref, wr_ref, br_ref, g_ref):
    tn = x_ref.shape[0]
    logits = _nt_dot(wr_ref[...], x_ref[...].astype(BF16))
    s = 1.0 / (1.0 + jnp.exp(-logits))
    sb = s + br_ref[...]
    low = jnp.float32(-3e38)

    grp = []
    for gi in range(N_GROUPS):
        blk = sb[gi * GROUP_SIZE:(gi + 1) * GROUP_SIZE]
        m1 = jnp.max(blk, axis=0, keepdims=True)
        is_max = blk == m1
        n_max = jnp.sum(is_max.astype(F32), axis=0, keepdims=True)
        m2 = jnp.max(jnp.where(is_max, low, blk), axis=0, keepdims=True)
        grp.append(m1 + jnp.where(n_max >= 2.0, m1, m2))
    masked = []
    for gi in range(N_GROUPS):
        rank = jnp.zeros((1, tn), F32)
        for g2 in range(N_GROUPS):
            if g2 == gi:
                continue
            beats = (grp[g2] > grp[gi]) | ((grp[g2] == grp[gi]) if g2 < gi else False)
            rank = rank + beats.astype(F32)
        keep = rank < float(TOPK_GROUPS)
        masked.append(jnp.where(keep, sb[gi * GROUP_SIZE:(gi + 1) * GROUP_SIZE], NEG))
    masked = jnp.concatenate(masked, axis=0)

    eidx = lax.broadcasted_iota(I32, (N_EXPERTS, tn), 0)
    rank = jnp.zeros((N_EXPERTS, tn), F32)
    for e2 in range(N_EXPERTS):
        row = masked[e2:e2 + 1]
        beats = (row > masked) | ((row == masked) & (e2 < eidx))
        rank = rank + beats.astype(F32)
    gw = jnp.where(rank < float(MOE_TOPK), s, 0.0)
    g_ref[...] = gw / jnp.sum(gw, axis=0, keepdims=True) * ROUTED_SCALE


def _router(x2d, wr_t, br_col):
    n = x2d.shape[0]
    tn = min(512, n)
    return pl.pallas_call(
        _router_kernel,
        grid=(n // tn,),
        in_specs=[pl.BlockSpec((tn, D_MODEL), lambda i: (i, 0)),
                  pl.BlockSpec(wr_t.shape, lambda i: (0, 0)),
                  pl.BlockSpec(br_col.shape, lambda i: (0, 0))],
        out_specs=pl.BlockSpec((N_EXPERTS, tn), lambda i: (0, i)),
        out_shape=jax.ShapeDtypeStruct((N_EXPERTS, n), F32),
        compiler_params=_cparams(("parallel",)),
    )(x2d, wr_t, br_col)


def _silu(x):
    return x / (1.0 + jnp.exp(-x))


def _moe_kernel(x_ref, gt_ref, wg_ref, wu_ref, wd_ref, sg_ref, su_ref, sd_ref, lg_ref, lb_ref, o_ref,
                xb_ref, acc_ref, *, alpha):
    e = pl.program_id(1)

    @pl.when(e == 0)
    def _():
        xb = x_ref[...].astype(BF16)
        xb_ref[...] = xb
        h = _silu(_dot(xb, sg_ref[...])) * _dot(xb, su_ref[...])
        acc_ref[...] = _dot(h.astype(BF16), sd_ref[...])

    xb = xb_ref[...]
    h = (_silu(_dot(xb, wg_ref[0, 0].astype(BF16))) * _dot(xb, wu_ref[0, 0].astype(BF16))).astype(BF16)
    gates = gt_ref[...]
    lane = lax.broadcasted_iota(I32, gates.shape, 1)
    gcol = jnp.sum(jnp.where(lane == e, gates, 0.0), axis=1, keepdims=True)
    chunk = 256
    for c in range(D_MODEL // chunk):
        cs = slice(c * chunk, (c + 1) * chunk)
        acc_ref[:, cs] += gcol * _dot(h, wd_ref[0, 0, :, cs].astype(BF16))

    @pl.when(e == pl.num_programs(1) - 1)
    def _():
        o_ref[...] = _layer_norm(alpha * x_ref[...] + acc_ref[...], lg_ref[...], lb_ref[...])


def _moe_ln(x2d, gates, layer, w_gate, w_up, w_down, ws_gate, ws_up, ws_down, g, b, alpha):
    n = x2d.shape[0]
    tn = min(1024, n)
    full = lambda a: pl.BlockSpec(a.shape, lambda i, e: (0,) * a.ndim)
    return pl.pallas_call(
        functools.partial(_moe_kernel, alpha=alpha),
        grid=(n // tn, N_EXPERTS),
        in_specs=[pl.BlockSpec((tn, D_MODEL), lambda i, e: (i, 0)),
                  pl.BlockSpec((tn, N_EXPERTS), lambda i, e: (i, 0)),
                  pl.BlockSpec((1, 1, D_MODEL, EXPERT_DIM), lambda i, e: (layer, e, 0, 0)),
                  pl.BlockSpec((1, 1, D_MODEL, EXPERT_DIM), lambda i, e: (layer, e, 0, 0)),
                  pl.BlockSpec((1, 1, EXPERT_DIM, D_MODEL), lambda i, e: (layer, e, 0, 0)),
                  full(ws_gate), full(ws_up), full(ws_down), full(g), full(b)],
        out_specs=pl.BlockSpec((tn, D_MODEL), lambda i, e: (i, 0)),
        out_shape=jax.ShapeDtypeStruct((n, D_MODEL), F32),
        scratch_shapes=[pltpu.VMEM((tn, D_MODEL), BF16), pltpu.VMEM((tn, D_MODEL), F32)],
        compiler_params=_cparams(("parallel", "arbitrary")),
    )(x2d, gates, w_gate, w_up, w_down, ws_gate, ws_up, ws_down, g, b)


def _gather_cols(w, order):
    parts = []
    for item in order:
        if isinstance(item, tuple):
            parts.append(jnp.zeros(w.shape[:-1] + (item[1],), w.dtype))
        else:
            off, width = SEG_OFF[item]
            parts.append(w[..., off:off + width])
    return jnp.concatenate(parts, axis=-1)


def _rotate_half_cols(w):
    lead = w.shape[:-1]
    w4 = w.reshape(lead + (-1, 2, HEAD_DIM // 2))
    return jnp.concatenate([-w4[..., 1:2, :], w4[..., 0:1, :]], axis=-2).reshape(w.shape)


def _rope_tables(seq, width):
    half = HEAD_DIM // 2
    inv_freq = ROPE_THETA ** (-jnp.arange(half, dtype=F32) * (2.0 / HEAD_DIM))
    ang = jnp.arange(seq).astype(F32)[:, None] * inv_freq[None, :]
    reps = width // half
    return jnp.tile(jnp.cos(ang), (1, reps)), jnp.tile(jnp.sin(ang), (1, reps))


def _token_rows(tok, b, seq):
    t4 = tok.reshape(b, seq, NSA_KV_HEADS, HEAD_DIM).transpose(0, 2, 1, 3)
    return t4.reshape(b * NSA_KV_HEADS, seq // CMP_STRIDE, CMP_STRIDE * HEAD_DIM)


def kernel(x, w_in, fox_forget_bias, cmp_pos_k, cmp_w1_k, cmp_w2_k, cmp_pos_v, cmp_w1_v, cmp_w2_v, w_out,
           ln1_g, ln1_b, w_router, b_router, w_gate, w_up, w_down, ws_gate, ws_up, ws_down, ln2_g, ln2_b):
    b, seq, dm = x.shape
    depth = w_in.shape[0]
    n = b * seq
    alpha = float((2 * depth) ** 0.25)

    w_plain = _gather_cols(w_in, PLAIN_ORDER).astype(BF16)
    w_rope = _gather_cols(w_in, ROPE_ORDER)
    w_rope_a = w_rope.astype(BF16)
    w_rope_b = _rotate_half_cols(w_rope).astype(BF16)
    cos_t, sin_t = _rope_tables(seq, PROJ_TILE_COLS)
    fw, nw = FOX_HEADS * HEAD_DIM, NSA_HEADS * HEAD_DIM
    w_out_b = w_out.astype(BF16)

    x2d = x.reshape(n, dm)
    for l in range(depth):
        hp, hr = _project(x2d, w_plain[l], w_rope_a[l], w_rope_b[l], cos_t, sin_t, seq)
        hp3 = hp.reshape(b, seq, PLAIN_W)
        hr3 = hr.reshape(b, seq, ROPE_W)

        f_logit = hp3[:, :, SMALL_COL + SMALL_F:SMALL_COL + SMALL_F + FOX_HEADS]
        c_rows = _fox_cumsum(f_logit.transpose(0, 2, 1), fox_forget_bias[l].reshape(FOX_HEADS, 1))
        o_fox = _fox_attention(hp3, c_rows.transpose(0, 2, 1), c_rows)

        kvw = NSA_KV_HEADS * HEAD_DIM
        rk = _token_rows(hr3[:, :, ROPE_POS['nsa_kc']:ROPE_POS['nsa_kc'] + kvw], b, seq)
        rv = _token_rows(hp3[:, :, PLAIN_POS['nsa_vc']:PLAIN_POS['nsa_vc'] + kvw], b, seq)
        k_cmp, v_cmp = _nsa_compress(
            rk, rv, cmp_pos_k[l].reshape(1, -1), cmp_pos_v[l].reshape(1, -1),
            cmp_w1_k[l].astype(BF16), cmp_w2_k[l].astype(BF16), cmp_w1_v[l].astype(BF16), cmp_w2_v[l].astype(BF16))
        n_rows = seq // CMP_STRIDE
        o_nsa = _nsa_attention(hr3, hp3, k_cmp.reshape(b, NSA_KV_HEADS, n_rows, HEAD_DIM),
                               v_cmp.reshape(b, NSA_KV_HEADS, n_rows, HEAD_DIM))

        o_dsa = _dsa_attention(hr3, hp3)

        x2d = _outproj_ln(o_fox.reshape(n, fw), o_nsa.reshape(n, nw), o_dsa.reshape(n, nw), x2d,
                          w_out_b[l, :fw], w_out_b[l, fw:fw + nw], w_out_b[l, fw + nw:],
                          ln1_g[l].reshape(1, dm), ln1_b[l].reshape(1, dm), alpha)

        gates_t = _router(x2d, w_router[l].T.astype(BF16), b_router[l].reshape(N_EXPERTS, 1))
        x2d = _moe_ln(x2d, gates_t.T, l, w_gate, w_up, w_down,
                      ws_gate[l].astype(BF16), ws_up[l].astype(BF16), ws_down[l].astype(BF16),
                      ln2_g[l].reshape(1, dm), ln2_b[l].reshape(1, dm), alpha)
    return x2d.reshape(b, seq, dm)
```

```python
import functools

import numpy as np
import jax
import jax.numpy as jnp
from jax import lax
from jax.experimental import pallas as pl
from jax.experimental.pallas import tpu as pltpu

D_MODEL = 1024
HEAD_DIM = 64
FOX_HEADS = 4
NSA_HEADS = 6
NSA_KV_HEADS = 2
NSA_REP = NSA_HEADS // NSA_KV_HEADS
DSA_HEADS = 6
ROPE_THETA = 10000.0
CMP_LEN = 32
CMP_STRIDE = 16
CMP_HIDDEN = 2 * HEAD_DIM
SEL_BLOCK = 64
SEL_TOPN = 16
WINDOW = 512
IDX_HEADS = 4
IDX_DIM = 64
DSA_TOPK = 256
N_EXPERTS = 64
N_GROUPS = 8
GROUP_SIZE = N_EXPERTS // N_GROUPS
TOPK_GROUPS = 4
MOE_TOPK = 8
EXPERT_DIM = 256
SHARED_DIM = 256
ROUTED_SCALE = 2.5
LN_EPS = 1e-5
NEG = -1e30
FORCE = 1e6
SCALE = HEAD_DIM ** -0.5

F32 = jnp.float32
BF16 = jnp.bfloat16
I32 = jnp.int32

VMEM_LIMIT_BYTES = 52 * 1024 * 1024
LANES = 128
KEY_STEP = 512
PROJ_TILE_COLS = 512

SEGMENTS = (
    ('fox_q', FOX_HEADS * HEAD_DIM), ('fox_k', FOX_HEADS * HEAD_DIM),
    ('fox_v', FOX_HEADS * HEAD_DIM), ('fox_f', FOX_HEADS),
    ('nsa_q', NSA_HEADS * HEAD_DIM),
    ('nsa_kc', NSA_KV_HEADS * HEAD_DIM), ('nsa_vc', NSA_KV_HEADS * HEAD_DIM),
    ('nsa_ks', NSA_KV_HEADS * HEAD_DIM), ('nsa_vs', NSA_KV_HEADS * HEAD_DIM),
    ('nsa_kw', NSA_KV_HEADS * HEAD_DIM), ('nsa_vw', NSA_KV_HEADS * HEAD_DIM),
    ('nsa_g', 3 * NSA_HEADS),
    ('dsa_q', DSA_HEADS * HEAD_DIM), ('dsa_k', HEAD_DIM), ('dsa_v', HEAD_DIM),
    ('idx_q', IDX_HEADS * IDX_DIM), ('idx_k', IDX_DIM), ('idx_w', IDX_HEADS),
)
SEG_OFF = {}
_off = 0
for _name, _width in SEGMENTS:
    SEG_OFF[_name] = (_off, _width)
    _off += _width
N_IN = _off

PLAIN_ORDER = ('fox_q', 'fox_k', 'fox_v', 'nsa_vc', 'nsa_vs', 'nsa_vw', 'dsa_v', (None, 64),
               'fox_f', 'nsa_g', 'idx_w', (None, 2 * LANES - FOX_HEADS - 3 * NSA_HEADS - IDX_HEADS))
ROPE_ORDER = ('nsa_q', 'nsa_kc', 'nsa_ks', 'nsa_kw', 'dsa_q', 'dsa_k', 'idx_k', 'idx_q')


def _layout(order):
    pos, off = {}, 0
    for item in order:
        if isinstance(item, tuple):
            off += item[1]
        else:
            pos[item] = off
            off += SEG_OFF[item][1]
    return pos, off


PLAIN_POS, PLAIN_W = _layout(PLAIN_ORDER)
ROPE_POS, ROPE_W = _layout(ROPE_ORDER)
SMALL_COL = PLAIN_POS['fox_f']
SMALL_F = 0
SMALL_G = FOX_HEADS
SMALL_W = FOX_HEADS + 3 * NSA_HEADS


def _cparams(sem):
    return pltpu.CompilerParams(dimension_semantics=sem, vmem_limit_bytes=VMEM_LIMIT_BYTES)


def _nt_dot(a, b):
    return lax.dot_general(a, b, (((1,), (1,)), ((), ())), preferred_element_type=F32)


def _dot(a, b):
    return jnp.dot(a, b, preferred_element_type=F32)


def _softmax_rows(s):
    m = jnp.max(s, axis=-1, keepdims=True)
    p = jnp.exp(s - m)
    return p, jnp.sum(p, axis=-1, keepdims=True)


def _for_key_extent(q_last, seq, body, step=KEY_STEP):
    n = seq // step
    if n <= 1:
        body(seq)
        return
    c = q_last // step
    for i in range(n):
        pl.when(c == i)(functools.partial(body, (i + 1) * step))


def _layer_norm(z, g, b):
    mu = jnp.mean(z, axis=-1, keepdims=True)
    zc = z - mu
    var = jnp.mean(zc * zc, axis=-1, keepdims=True)
    return zc * lax.rsqrt(var + LN_EPS) * g + b


def _proj_kernel(x_ref, w_ref, o_ref, xb_ref):
    @pl.when(pl.program_id(1) == 0)
    def _():
        xb_ref[...] = x_ref[...].astype(BF16)

    o_ref[...] = _dot(xb_ref[...], w_ref[...])


def _proj_rope_kernel(x_ref, wa_ref, wb_ref, cos_ref, sin_ref, o_ref, xb_ref):
    @pl.when(pl.program_id(1) == 0)
    def _():
        xb_ref[...] = x_ref[...].astype(BF16)

    xb = xb_ref[...]
    o_ref[...] = _dot(xb, wa_ref[...]) * cos_ref[...] + _dot(xb, wb_ref[...]) * sin_ref[...]


def _project(x2d, w_plain, w_a, w_b, cos_t, sin_t, seq):
    n = x2d.shape[0]
    tm = min(1024, seq)
    tc = cos_t.shape[1]
    plain = pl.pallas_call(
        _proj_kernel,
        grid=(n // tm, PLAIN_W // tc),
        in_specs=[pl.BlockSpec((tm, D_MODEL), lambda i, j: (i, 0)),
                  pl.BlockSpec((D_MODEL, tc), lambda i, j: (0, j))],
        out_specs=pl.BlockSpec((tm, tc), lambda i, j: (i, j)),
        out_shape=jax.ShapeDtypeStruct((n, PLAIN_W), F32),
        scratch_shapes=[pltpu.VMEM((tm, D_MODEL), BF16)],
        compiler_params=_cparams(("parallel", "arbitrary")),
    )(x2d, w_plain)
    nt = seq // tm
    roped = pl.pallas_call(
        _proj_rope_kernel,
        grid=(n // tm, ROPE_W // tc),
        in_specs=[pl.BlockSpec((tm, D_MODEL), lambda i, j: (i, 0)),
                  pl.BlockSpec((D_MODEL, tc), lambda i, j: (0, j)),
                  pl.BlockSpec((D_MODEL, tc), lambda i, j: (0, j)),
                  pl.BlockSpec((tm, tc), lambda i, j: (i % nt, 0)),
                  pl.BlockSpec((tm, tc), lambda i, j: (i % nt, 0))],
        out_specs=pl.BlockSpec((tm, tc), lambda i, j: (i, j)),
        out_shape=jax.ShapeDtypeStruct((n, ROPE_W), F32),
        scratch_shapes=[pltpu.VMEM((tm, D_MODEL), BF16)],
        compiler_params=_cparams(("parallel", "arbitrary")),
    )(x2d, w_a, w_b, cos_t, sin_t)
    return plain, roped


def _fox_cum_kernel(f_ref, fb_ref, o_ref):
    x = f_ref[0] + fb_ref[...]
    c = jnp.minimum(x, 0.0) - jnp.log1p(jnp.exp(-jnp.abs(x)))
    seq = c.shape[-1]
    lane = lax.broadcasted_iota(I32, c.shape, 1)
    sh = 1
    while sh < seq:
        c = c + jnp.where(lane >= sh, pltpu.roll(c, sh, 1), 0.0)
        sh *= 2
    o_ref[0] = c


def _fox_cumsum(f_rows, f_bias):
    b, h, seq = f_rows.shape
    return pl.pallas_call(
        _fox_cum_kernel,
        grid=(b,),
        in_specs=[pl.BlockSpec((1, h, seq), lambda i: (i, 0, 0)),
                  pl.BlockSpec((h, 1), lambda i: (0, 0))],
        out_specs=pl.BlockSpec((1, h, seq), lambda i: (i, 0, 0)),
        out_shape=jax.ShapeDtypeStruct((b, h, seq), F32),
        compiler_params=_cparams(("parallel",)),
    )(f_rows, f_bias)


def _fox_kernel(q_ref, k_ref, v_ref, cc_ref, cr_ref, o_ref, *, tq, seq):
    q0 = pl.program_id(1) * tq

    def body(klen):
        qpos = q0 + lax.broadcasted_iota(I32, (tq, klen), 0)
        causal = lax.broadcasted_iota(I32, (tq, klen), 1) <= qpos
        for h in range(FOX_HEADS):
            sl = slice(h * HEAD_DIM, (h + 1) * HEAD_DIM)
            qh = (q_ref[0, :, sl] * SCALE).astype(BF16)
            kh = k_ref[0, :klen, sl].astype(BF16)
            vh = v_ref[0, :klen, sl].astype(BF16)
            s = _nt_dot(qh, kh) + (cc_ref[0, :, h:h + 1] - cr_ref[0, h:h + 1, :klen])
            p, l = _softmax_rows(jnp.where(causal, s, NEG))
            o_ref[0, :, sl] = (_dot(p.astype(BF16), vh) / l).astype(o_ref.dtype)

    _for_key_extent(q0 + tq - 1, seq, body, step=tq)


def _fox_attention(hp3, c_cols, c_rows):
    b, seq, _ = hp3.shape
    tq = min(256, seq)
    w = FOX_HEADS * HEAD_DIM
    return pl.pallas_call(
        functools.partial(_fox_kernel, tq=tq, seq=seq),
        grid=(b, seq // tq),
        in_specs=[pl.BlockSpec((1, tq, w), lambda i, j: (i, j, PLAIN_POS['fox_q'] // w)),
                  pl.BlockSpec((1, seq, w), lambda i, j: (i, 0, PLAIN_POS['fox_k'] // w)),
                  pl.BlockSpec((1, seq, w), lambda i, j: (i, 0, PLAIN_POS['fox_v'] // w)),
                  pl.BlockSpec((1, tq, FOX_HEADS), lambda i, j: (i, j, 0)),
                  pl.BlockSpec((1, FOX_HEADS, seq), lambda i, j: (i, 0, 0))],
        out_specs=pl.BlockSpec((1, tq, w), lambda i, j: (i, j, 0)),
        out_shape=jax.ShapeDtypeStruct((b, seq, w), BF16),
        compiler_params=_cparams(("parallel", "arbitrary")),
    )(hp3, hp3, hp3, c_cols, c_rows)


def _gelu_tanh(x):
    return 0.5 * x * (1.0 + jnp.tanh(np.float32(np.sqrt(2.0 / np.pi)) * (x + 0.044715 * (x * x * x))))


def _compress_kernel(rk_ref, rv_ref, pek_ref, pev_ref, w1k_ref, w2k_ref, w1v_ref, w2v_ref, ok_ref, ov_ref):
    half = CMP_STRIDE * HEAD_DIM

    def one(r_ref, pe_ref, w1_ref, w2_ref, o_ref):
        r = r_ref[0]
        n_rows = r.shape[0]
        lo = _dot((r + pe_ref[:, :half]).astype(BF16), w1_ref[:half, :])
        hi = _dot((r + pe_ref[:, half:]).astype(BF16), w1_ref[half:, :])
        hid = _gelu_tanh(lo + pltpu.roll(hi, n_rows - 1, 0))
        o_ref[0] = _dot(hid.astype(BF16), w2_ref[...])

    one(rk_ref, pek_ref, w1k_ref, w2k_ref, ok_ref)
    one(rv_ref, pev_ref, w1v_ref, w2v_ref, ov_ref)


def _nsa_compress(rk, rv, pek, pev, w1k, w2k, w1v, w2v):
    bg, rows, width = rk.shape
    tok = pl.BlockSpec((1, rows, width), lambda i: (i, 0, 0))
    full = lambda a: pl.BlockSpec(a.shape, lambda i: (0,) * a.ndim)
    out = pl.BlockSpec((1, rows, HEAD_DIM), lambda i: (i, 0, 0))
    return pl.pallas_call(
        _compress_kernel,
        grid=(bg,),
        in_specs=[tok, tok, full(pek), full(pev), full(w1k), full(w2k), full(w1v), full(w2v)],
        out_specs=[out, out],
        out_shape=[jax.ShapeDtypeStruct((bg, rows, HEAD_DIM), F32)] * 2,
        compiler_params=_cparams(("parallel",)),
    )(rk, rv, pek, pev, w1k, w2k, w1v, w2v)


def _split3_nt_dot(b01, a):
    a1 = a.astype(BF16)
    r1 = a - a1.astype(F32)
    a2 = r1.astype(BF16)
    a3 = (r1 - a2.astype(F32)).astype(BF16)
    return _nt_dot(b01, a1) + _nt_dot(b01, a2) + _nt_dot(b01, a3)


def _nsa_kernel(q_ref, kc_ref, vc_ref, ks_ref, kw_ref, vs_ref, vw_ref, sm_ref, o_ref, oslc_ref, *, tq, seq, wlen):
    q0 = pl.program_id(1) * tq
    n_c = kc_ref.shape[2]
    n_s = seq // SEL_BLOCK
    rows = NSA_REP * tq
    tcol = q0 + lax.broadcasted_iota(I32, (tq, 1), 0)
    tcol_r = jnp.concatenate([tcol] * NSA_REP, axis=0)

    cidx = lax.broadcasted_iota(I32, (rows, n_c), 1)
    vis_r = (cidx * CMP_STRIDE + (CMP_LEN - 1)) <= tcol_r
    any_vis_r = (tcol_r >= (CMP_LEN - 1)).astype(F32)
    oc = lax.broadcasted_iota(I32, (n_s, n_c), 1) * CMP_STRIDE
    ob = lax.broadcasted_iota(I32, (n_s, n_c), 0) * SEL_BLOCK
    overlap_t = ((oc < ob + SEL_BLOCK) & (oc + CMP_LEN > ob)).astype(BF16)
    jj_t = lax.broadcasted_iota(I32, (n_s, tq), 0)
    blk_t = (q0 + lax.broadcasted_iota(I32, (1, tq), 1)) // SEL_BLOCK
    w0 = pl.multiple_of(jnp.maximum(q0 + tq - wlen, 0), 8)
    dist = tcol_r - (w0 + lax.broadcasted_iota(I32, (rows, wlen), 1))
    band_r = (dist >= 0) & (dist < WINDOW)
    gates = sm_ref[0]

    qs_all, qs_aug_all, o_cmp_all = [], [], []
    for g in range(NSA_KV_HEADS):
        qs = (jnp.concatenate(
            [q_ref[0, :, (g * NSA_REP + r) * HEAD_DIM:(g * NSA_REP + r + 1) * HEAD_DIM] for r in range(NSA_REP)],
            axis=0) * SCALE).astype(BF16)

        s_c = _nt_dot(qs, kc_ref[0, g].astype(BF16))
        p_c, l_c = _softmax_rows(jnp.where(vis_r, s_c, NEG))
        p_c = p_c / l_c * any_vis_r
        o_cmp_all.append(_dot(p_c.astype(BF16), vc_ref[0, g].astype(BF16)))

        p_sum = p_c[0:tq]
        for r in range(1, NSA_REP):
            p_sum = p_sum + p_c[r * tq:(r + 1) * tq]
        imp = _split3_nt_dot(overlap_t, p_sum)
        forced = (jj_t == 0) | (jj_t == blk_t) | (jj_t == blk_t - 1)
        imp = jnp.where(jj_t <= blk_t, jnp.where(forced, FORCE, imp), -1.0)
        rank = jnp.zeros((n_s, tq), F32)
        for j2 in range(n_s):
            row = imp[j2:j2 + 1, :]
            beats = (row > imp) | ((row == imp) & (j2 < jj_t))
            rank = rank + beats.astype(F32)
        sel = (rank < float(min(SEL_TOPN, n_s))) & (jj_t <= blk_t)
        bias_t = jnp.concatenate([jnp.where(sel, 0.0, NEG), jnp.zeros((tq - n_s, tq), F32)], axis=0)
        sel_bias = bias_t.T[:, :LANES - HEAD_DIM]
        qs_all.append(qs)
        qs_aug_all.append(jnp.concatenate([qs, jnp.concatenate([sel_bias] * NSA_REP, axis=0).astype(BF16)], axis=1))

    def slc_body(klen):
        block_onehot = (lax.broadcasted_iota(I32, (klen, LANES - HEAD_DIM), 0) // SEL_BLOCK
                        == lax.broadcasted_iota(I32, (klen, LANES - HEAD_DIM), 1)).astype(BF16)
        causal_r = lax.broadcasted_iota(I32, (rows, klen), 1) <= tcol_r
        for g in range(NSA_KV_HEADS):
            ksl = slice(g * HEAD_DIM, (g + 1) * HEAD_DIM)
            k_aug = jnp.concatenate([ks_ref[0, :klen, ksl].astype(BF16), block_onehot], axis=1)
            p_s, l_s = _softmax_rows(jnp.where(causal_r, _nt_dot(qs_aug_all[g], k_aug), NEG))
            oslc_ref[g] = _dot(p_s.astype(BF16), vs_ref[0, :klen, ksl].astype(BF16)) / l_s

    _for_key_extent(q0 + tq - 1, seq, slc_body)

    for g in range(NSA_KV_HEADS):
        ksl = slice(g * HEAD_DIM, (g + 1) * HEAD_DIM)
        o_cmp, o_slc = o_cmp_all[g], oslc_ref[g]

        kw = kw_ref[0, pl.ds(w0, wlen), ksl].astype(BF16)
        vw = vw_ref[0, pl.ds(w0, wlen), ksl].astype(BF16)
        p_w, l_w = _softmax_rows(jnp.where(band_r, _nt_dot(qs_all[g], kw), NEG))
        o_win = _dot(p_w.astype(BF16), vw) / l_w

        for r in range(NSA_REP):
            head = g * NSA_REP + r
            rs = slice(r * tq, (r + 1) * tq)

            def gate(branch, head=head):
                col = SMALL_G + branch * NSA_HEADS + head
                return 1.0 / (1.0 + jnp.exp(-gates[:, col:col + 1]))

            out = gate(0) * o_cmp[rs] + gate(1) * o_slc[rs] + gate(2) * o_win[rs]
            o_ref[0, :, head * HEAD_DIM:(head + 1) * HEAD_DIM] = out.astype(o_ref.dtype)


def _nsa_attention(hr3, hp3, k_cmp, v_cmp):
    b, seq, _ = hr3.shape
    tq = min(256, seq)
    wlen = min(WINDOW + tq, seq)
    qw = NSA_HEADS * HEAD_DIM
    kvw = NSA_KV_HEADS * HEAD_DIM
    n_c = k_cmp.shape[2]
    cmp_spec = pl.BlockSpec((1, NSA_KV_HEADS, n_c, HEAD_DIM), lambda i, j: (i, 0, 0, 0))

    def seq_spec(col):
        return pl.BlockSpec((1, seq, kvw), lambda i, j: (i, 0, col // kvw))

    return pl.pallas_call(
        functools.partial(_nsa_kernel, tq=tq, seq=seq, wlen=wlen),
        grid=(b, seq // tq),
        in_specs=[pl.BlockSpec((1, tq, qw), lambda i, j: (i, j, ROPE_POS['nsa_q'] // qw)),
                  cmp_spec, cmp_spec,
                  seq_spec(ROPE_POS['nsa_ks']), seq_spec(ROPE_POS['nsa_kw']),
                  seq_spec(PLAIN_POS['nsa_vs']), seq_spec(PLAIN_POS['nsa_vw']),
                  pl.BlockSpec((1, tq, LANES), lambda i, j: (i, j, SMALL_COL // LANES))],
        out_specs=pl.BlockSpec((1, tq, qw), lambda i, j: (i, j, 0)),
        out_shape=jax.ShapeDtypeStruct((b, seq, qw), BF16),
        scratch_shapes=[pltpu.VMEM((NSA_KV_HEADS, NSA_REP * tq, HEAD_DIM), F32)],
        compiler_params=_cparams(("parallel", "arbitrary")),
    )(hr3, k_cmp, v_cmp, hr3, hr3, hp3, hp3, hp3)


def _row_count(mask):
    return jnp.sum(mask.astype(F32), axis=-1, keepdims=True)


NEG_KEY = int(np.float32(NEG).view(np.int32)) ^ 0x7FFFFFFF


def _dsa_kernel(q_ref, kk_ref, v_ref, iq_ref, sm_ref, o_ref, key_ref, bias_ref, *, tq, seq, topk):
    q0 = pl.program_id(1) * tq
    tcol = q0 + lax.broadcasted_iota(I32, (tq, 1), 0)
    w_idx = sm_ref[0, :, SMALL_W:SMALL_W + IDX_HEADS] * (IDX_HEADS ** -0.5) * (IDX_DIM ** -0.5)
    kf = float(topk)

    def body(klen):
        ik = kk_ref[0, :klen, HEAD_DIM:2 * HEAD_DIM].astype(BF16)
        score = jnp.zeros((tq, klen), F32)
        for h in range(IDX_HEADS):
            d = _nt_dot(iq_ref[0, :, h * IDX_DIM:(h + 1) * IDX_DIM].astype(BF16), ik)
            score = score + w_idx[:, h:h + 1] * jnp.maximum(d, 0.0)
        score = jnp.where(lax.broadcasted_iota(I32, (tq, klen), 1) <= tcol, score, NEG)

        bits = lax.bitcast_convert_type(score, I32)
        key_ref[:, :klen] = jnp.where(bits < 0, bits ^ 0x7FFFFFFF, bits)

        def tau_step(i, tau):
            cand = tau + jnp.left_shift(jnp.int32(1), 31 - i)
            cnt = _row_count(key_ref[:, :klen] >= cand)
            return jnp.where(cnt >= kf, cand, tau)

        tau = lax.fori_loop(0, 32, tau_step, jnp.full((tq, 1), -2 ** 31, I32))
        need = kf - _row_count(key_ref[:, :klen] > tau)

        tri = (lax.broadcasted_iota(I32, (LANES, LANES), 0) <= lax.broadcasted_iota(I32, (LANES, LANES), 1)).astype(BF16)
        seen = jnp.zeros((tq, 1), F32)
        for blk in range(klen // LANES):
            ks = slice(blk * LANES, (blk + 1) * LANES)
            key = key_ref[:, ks]
            tie = key == tau
            count = _dot(jnp.where(tie, 1.0, 0.0).astype(BF16), tri) + seen
            seen = count[:, LANES - 1:LANES]
            kpos = blk * LANES + lax.broadcasted_iota(I32, (tq, LANES), 1)
            chosen = ((key > tau) | (tie & (count <= need))) & (kpos <= tcol)
            bias_ref[:, ks] = jnp.where(chosen, 0.0, NEG)

        k = kk_ref[0, :klen, 0:HEAD_DIM].astype(BF16)
        v = v_ref[0, :klen, 0:HEAD_DIM].astype(BF16)
        bias = bias_ref[:, :klen]
        for h in range(DSA_HEADS):
            sl = slice(h * HEAD_DIM, (h + 1) * HEAD_DIM)
            s = _nt_dot((q_ref[0, :, sl] * SCALE).astype(BF16), k)
            p, l = _softmax_rows(s + bias)
            o_ref[0, :, sl] = (_dot(p.astype(BF16), v) / l).astype(o_ref.dtype)

    _for_key_extent(q0 + tq - 1, seq, body)


def _dsa_attention(hr3, hp3):
    b, seq, _ = hr3.shape
    tq = min(256, seq)
    topk = min(DSA_TOPK, seq // 4)
    qw = DSA_HEADS * HEAD_DIM
    iqw = IDX_HEADS * IDX_DIM
    return pl.pallas_call(
        functools.partial(_dsa_kernel, tq=tq, seq=seq, topk=topk),
        grid=(b, seq // tq),
        in_specs=[pl.BlockSpec((1, tq, qw), lambda i, j: (i, j, ROPE_POS['dsa_q'] // qw)),
                  pl.BlockSpec((1, seq, LANES), lambda i, j: (i, 0, ROPE_POS['dsa_k'] // LANES)),
                  pl.BlockSpec((1, seq, LANES), lambda i, j: (i, 0, PLAIN_POS['dsa_v'] // LANES)),
                  pl.BlockSpec((1, tq, iqw), lambda i, j: (i, j, ROPE_POS['idx_q'] // iqw)),
                  pl.BlockSpec((1, tq, LANES), lambda i, j: (i, j, SMALL_COL // LANES))],
        out_specs=pl.BlockSpec((1, tq, qw), lambda i, j: (i, j, 0)),
        out_shape=jax.ShapeDtypeStruct((b, seq, qw), BF16),
        scratch_shapes=[pltpu.VMEM((tq, seq), I32), pltpu.VMEM((tq, seq), F32)],
        compiler_params=_cparams(("parallel", "arbitrary")),
    )(hr3, hr3, hp3, hr3, hp3)


def _outproj_kernel(of_ref, on_ref, od_ref, x_ref, wf_ref, wn_ref, wd_ref, g_ref, b_ref, o_ref, *, alpha):
    mix = _dot(of_ref[...], wf_ref[...]) + _dot(on_ref[...], wn_ref[...]) + _dot(od_ref[...], wd_ref[...])
    o_ref[...] = _layer_norm(alpha * x_ref[...] + mix, g_ref[...], b_ref[...])


def _outproj_ln(o_fox, o_nsa, o_dsa, x2d, w_f, w_n, w_d, g, b, alpha):
    n = x2d.shape[0]
    tm = min(512, n)
    row = lambda a: pl.BlockSpec((tm, a.shape[1]), lambda i: (i, 0))
    full = lambda a: pl.BlockSpec(a.shape, lambda i: (0, 0))
    return pl.pallas_call(
        functools.partial(_outproj_kernel, alpha=alpha),
        grid=(n // tm,),
        in_specs=[row(o_fox), row(o_nsa), row(o_dsa), row(x2d), full(w_f), full(w_n), full(w_d), full(g), full(b)],
        out_specs=pl.BlockSpec((tm, D_MODEL), lambda i: (i, 0)),
        out_shape=jax.ShapeDtypeStruct((n, D_MODEL), F32),
        compiler_params=_cparams(("parallel",)),
    )(o_fox, o_nsa, o_dsa, x2d, w_f, w_n, w_d, g, b)


def _router_kernel(x_ref, wr_ref, br_ref, g_ref):
    tn = x_ref.shape[0]
    logits = _nt_dot(wr_ref[...], x_ref[...].astype(BF16))
    s = 1.0 / (1.0 + jnp.exp(-logits))
    sb = s + br_ref[...]
    low = jnp.float32(-3e38)

    grp = []
    for gi in range(N_GROUPS):
        blk = sb[gi * GROUP_SIZE:(gi + 1) * GROUP_SIZE]
        m1 = jnp.max(blk, axis=0, keepdims=True)
        is_max = blk == m1
        n_max = jnp.sum(is_max.astype(F32), axis=0, keepdims=True)
        m2 = jnp.max(jnp.where(is_max, low, blk), axis=0, keepdims=True)
        grp.append(m1 + jnp.where(n_max >= 2.0, m1, m2))
    masked = []
    for gi in range(N_GROUPS):
        rank = jnp.zeros((1, tn), F32)
        for g2 in range(N_GROUPS):
            if g2 == gi:
                continue
            beats = (grp[g2] > grp[gi]) | ((grp[g2] == grp[gi]) if g2 < gi else False)
            rank = rank + beats.astype(F32)
        keep = rank < float(TOPK_GROUPS)
        masked.append(jnp.where(keep, sb[gi * GROUP_SIZE:(gi + 1) * GROUP_SIZE], NEG))
    masked = jnp.concatenate(masked, axis=0)

    eidx = lax.broadcasted_iota(I32, (N_EXPERTS, tn), 0)
    rank = jnp.zeros((N_EXPERTS, tn), F32)
    for e2 in range(N_EXPERTS):
        row = masked[e2:e2 + 1]
        beats = (row > masked) | ((row == masked) & (e2 < eidx))
        rank = rank + beats.astype(F32)
    gw = jnp.where(rank < float(MOE_TOPK), s, 0.0)
    g_ref[...] = gw / jnp.sum(gw, axis=0, keepdims=True) * ROUTED_SCALE


def _router(x2d, wr_t, br_col):
    n = x2d.shape[0]
    tn = min(512, n)
    return pl.pallas_call(
        _router_kernel,
        grid=(n // tn,),
        in_specs=[pl.BlockSpec((tn, D_MODEL), lambda i: (i, 0)),
                  pl.BlockSpec(wr_t.shape, lambda i: (0, 0)),
                  pl.BlockSpec(br_col.shape, lambda i: (0, 0))],
        out_specs=pl.BlockSpec((N_EXPERTS, tn), lambda i: (0, i)),
        out_shape=jax.ShapeDtypeStruct((N_EXPERTS, n), F32),
        compiler_params=_cparams(("parallel",)),
    )(x2d, wr_t, br_col)


def _silu(x):
    return x / (1.0 + jnp.exp(-x))


def _moe_kernel(x_ref, gt_ref, wg_ref, wu_ref, wd_ref, sg_ref, su_ref, sd_ref, lg_ref, lb_ref, o_ref,
                xb_ref, acc_ref, *, alpha):
    e = pl.program_id(1)

    @pl.when(e == 0)
    def _():
        xb = x_ref[...].astype(BF16)
        xb_ref[...] = xb
        h = _silu(_dot(xb, sg_ref[...])) * _dot(xb, su_ref[...])
        acc_ref[...] = _dot(h.astype(BF16), sd_ref[...])

    xb = xb_ref[...]
    h = (_silu(_dot(xb, wg_ref[0, 0].astype(BF16))) * _dot(xb, wu_ref[0, 0].astype(BF16))).astype(BF16)
    gates = gt_ref[...]
    lane = lax.broadcasted_iota(I32, gates.shape, 1)
    gcol = jnp.sum(jnp.where(lane == e, gates, 0.0), axis=1, keepdims=True)
    chunk = 256
    for c in range(D_MODEL // chunk):
        cs = slice(c * chunk, (c + 1) * chunk)
        acc_ref[:, cs] += gcol * _dot(h, wd_ref[0, 0, :, cs].astype(BF16))

    @pl.when(e == pl.num_programs(1) - 1)
    def _():
        o_ref[...] = _layer_norm(alpha * x_ref[...] + acc_ref[...], lg_ref[...], lb_ref[...])


def _moe_ln(x2d, gates, layer, w_gate, w_up, w_down, ws_gate, ws_up, ws_down, g, b, alpha):
    n = x2d.shape[0]
    tn = min(1024, n)
    full = lambda a: pl.BlockSpec(a.shape, lambda i, e: (0,) * a.ndim)
    return pl.pallas_call(
        functools.partial(_moe_kernel, alpha=alpha),
        grid=(n // tn, N_EXPERTS),
        in_specs=[pl.BlockSpec((tn, D_MODEL), lambda i, e: (i, 0)),
                  pl.BlockSpec((tn, N_EXPERTS), lambda i, e: (i, 0)),
                  pl.BlockSpec((1, 1, D_MODEL, EXPERT_DIM), lambda i, e: (layer, e, 0, 0)),
                  pl.BlockSpec((1, 1, D_MODEL, EXPERT_DIM), lambda i, e: (layer, e, 0, 0)),
                  pl.BlockSpec((1, 1, EXPERT_DIM, D_MODEL), lambda i, e: (layer, e, 0, 0)),
                  full(ws_gate), full(ws_up), full(ws_down), full(g), full(b)],
        out_specs=pl.BlockSpec((tn, D_MODEL), lambda i, e: (i, 0)),
        out_shape=jax.ShapeDtypeStruct((n, D_MODEL), F32),
        scratch_shapes=[pltpu.VMEM((tn, D_MODEL), BF16), pltpu.VMEM((tn, D_MODEL), F32)],
        compiler_params=_cparams(("parallel", "arbitrary")),
    )(x2d, gates, w_gate, w_up, w_down, ws_gate, ws_up, ws_down, g, b)


def _gather_cols(w, order):
    parts = []
    for item in order:
        if isinstance(item, tuple):
            parts.append(jnp.zeros(w.shape[:-1] + (item[1],), w.dtype))
        else:
            off, width = SEG_OFF[item]
            parts.append(w[..., off:off + width])
    return jnp.concatenate(parts, axis=-1)


def _rotate_half_cols(w):
    lead = w.shape[:-1]
    w4 = w.reshape(lead + (-1, 2, HEAD_DIM // 2))
    return jnp.concatenate([-w4[..., 1:2, :], w4[..., 0:1, :]], axis=-2).reshape(w.shape)


def _rope_tables(seq, width):
    half = HEAD_DIM // 2
    inv_freq = ROPE_THETA ** (-jnp.arange(half, dtype=F32) * (2.0 / HEAD_DIM))
    ang = jnp.arange(seq).astype(F32)[:, None] * inv_freq[None, :]
    reps = width // half
    return jnp.tile(jnp.cos(ang), (1, reps)), jnp.tile(jnp.sin(ang), (1, reps))


def _token_rows(tok, b, seq):
    t4 = tok.reshape(b, seq, NSA_KV_HEADS, HEAD_DIM).transpose(0, 2, 1, 3)
    return t4.reshape(b * NSA_KV_HEADS, seq // CMP_STRIDE, CMP_STRIDE * HEAD_DIM)


def kernel(x, w_in, fox_forget_bias, cmp_pos_k, cmp_w1_k, cmp_w2_k, cmp_pos_v, cmp_w1_v, cmp_w2_v, w_out,
           ln1_g, ln1_b, w_router, b_router, w_gate, w_up, w_down, ws_gate, ws_up, ws_down, ln2_g, ln2_b):
    b, seq, dm = x.shape
    depth = w_in.shape[0]
    n = b * seq
    alpha = float((2 * depth) ** 0.25)

    w_plain = _gather_cols(w_in, PLAIN_ORDER).astype(BF16)
    w_rope = _gather_cols(w_in, ROPE_ORDER)
    w_rope_a = w_rope.astype(BF16)
    w_rope_b = _rotate_half_cols(w_rope).astype(BF16)
    cos_t, sin_t = _rope_tables(seq, PROJ_TILE_COLS)
    fw, nw = FOX_HEADS * HEAD_DIM, NSA_HEADS * HEAD_DIM
    w_out_b = w_out.astype(BF16)

    x2d = x.reshape(n, dm)
    for l in range(depth):
        hp, hr = _project(x2d, w_plain[l], w_rope_a[l], w_rope_b[l], cos_t, sin_t, seq)
        hp3 = hp.reshape(b, seq, PLAIN_W)
        hr3 = hr.reshape(b, seq, ROPE_W)

        f_logit = hp3[:, :, SMALL_COL + SMALL_F:SMALL_COL + SMALL_F + FOX_HEADS]
        c_rows = _fox_cumsum(f_logit.transpose(0, 2, 1), fox_forget_bias[l].reshape(FOX_HEADS, 1))
        o_fox = _fox_attention(hp3, c_rows.transpose(0, 2, 1), c_rows)

        kvw = NSA_KV_HEADS * HEAD_DIM
        rk = _token_rows(hr3[:, :, ROPE_POS['nsa_kc']:ROPE_POS['nsa_kc'] + kvw], b, seq)
        rv = _token_rows(hp3[:, :, PLAIN_POS['nsa_vc']:PLAIN_POS['nsa_vc'] + kvw], b, seq)
        k_cmp, v_cmp = _nsa_compress(
            rk, rv, cmp_pos_k[l].reshape(1, -1), cmp_pos_v[l].reshape(1, -1),
            cmp_w1_k[l].astype(BF16), cmp_w2_k[l].astype(BF16), cmp_w1_v[l].astype(BF16), cmp_w2_v[l].astype(BF16))
        n_rows = seq // CMP_STRIDE
        o_nsa = _nsa_attention(hr3, hp3, k_cmp.reshape(b, NSA_KV_HEADS, n_rows, HEAD_DIM),
                               v_cmp.reshape(b, NSA_KV_HEADS, n_rows, HEAD_DIM))

        o_dsa = _dsa_attention(hr3, hp3)

        x2d = _outproj_ln(o_fox.reshape(n, fw), o_nsa.reshape(n, nw), o_dsa.reshape(n, nw), x2d,
                          w_out_b[l, :fw], w_out_b[l, fw:fw + nw], w_out_b[l, fw + nw:],
                          ln1_g[l].reshape(1, dm), ln1_b[l].reshape(1, dm), alpha)

        gates_t = _router(x2d, w_router[l].T.astype(BF16), b_router[l].reshape(N_EXPERTS, 1))
        x2d = _moe_ln(x2d, gates_t.T, l, w_gate, w_up, w_down,
                      ws_gate[l].astype(BF16), ws_up[l].astype(BF16), ws_down[l].astype(BF16),
                      ln2_g[l].reshape(1, dm), ln2_b[l].reshape(1, dm), alpha)
    return x2d.reshape(b, seq, dm)
```

```python
import functools

import numpy as np
import jax
import jax.numpy as jnp
from jax import lax
from jax.experimental import pallas as pl
from jax.experimental.pallas import tpu as pltpu

D_MODEL = 1024
HEAD_DIM = 64
FOX_HEADS = 4
NSA_HEADS = 6
NSA_KV_HEADS = 2
NSA_REP = NSA_HEADS // NSA_KV_HEADS
DSA_HEADS = 6
ROPE_THETA = 10000.0
CMP_LEN = 32
CMP_STRIDE = 16
CMP_HIDDEN = 2 * HEAD_DIM
SEL_BLOCK = 64
SEL_TOPN = 16
WINDOW = 512
IDX_HEADS = 4
IDX_DIM = 64
DSA_TOPK = 256
N_EXPERTS = 64
N_GROUPS = 8
GROUP_SIZE = N_EXPERTS // N_GROUPS
TOPK_GROUPS = 4
MOE_TOPK = 8
EXPERT_DIM = 256
SHARED_DIM = 256
ROUTED_SCALE = 2.5
LN_EPS = 1e-5
NEG = -1e30
FORCE = 1e6
SCALE = HEAD_DIM ** -0.5

F32 = jnp.float32
BF16 = jnp.bfloat16
I32 = jnp.int32

VMEM_LIMIT_BYTES = 52 * 1024 * 1024
LANES = 128
KEY_STEP = 512
PROJ_TILE_COLS = 512

SEGMENTS = (
    ('fox_q', FOX_HEADS * HEAD_DIM), ('fox_k', FOX_HEADS * HEAD_DIM),
    ('fox_v', FOX_HEADS * HEAD_DIM), ('fox_f', FOX_HEADS),
    ('nsa_q', NSA_HEADS * HEAD_DIM),
    ('nsa_kc', NSA_KV_HEADS * HEAD_DIM), ('nsa_vc', NSA_KV_HEADS * HEAD_DIM),
    ('nsa_ks', NSA_KV_HEADS * HEAD_DIM), ('nsa_vs', NSA_KV_HEADS * HEAD_DIM),
    ('nsa_kw', NSA_KV_HEADS * HEAD_DIM), ('nsa_vw', NSA_KV_HEADS * HEAD_DIM),
    ('nsa_g', 3 * NSA_HEADS),
    ('dsa_q', DSA_HEADS * HEAD_DIM), ('dsa_k', HEAD_DIM), ('dsa_v', HEAD_DIM),
    ('idx_q', IDX_HEADS * IDX_DIM), ('idx_k', IDX_DIM), ('idx_w', IDX_HEADS),
)
SEG_OFF = {}
_off = 0
for _name, _width in SEGMENTS:
    SEG_OFF[_name] = (_off, _width)
    _off += _width
N_IN = _off

PLAIN_ORDER = ('fox_q', 'fox_k', 'fox_v', 'nsa_vc', 'nsa_vs', 'nsa_vw', 'dsa_v', (None, 64),
               'fox_f', 'nsa_g', 'idx_w', (None, 2 * LANES - FOX_HEADS - 3 * NSA_HEADS - IDX_HEADS))
ROPE_ORDER = ('nsa_q', 'nsa_kc', 'nsa_ks', 'nsa_kw', 'dsa_q', 'dsa_k', 'idx_k', 'idx_q')


def _layout(order):
    pos, off = {}, 0
    for item in order:
        if isinstance(item, tuple):
            off += item[1]
        else:
            pos[item] = off
            off += SEG_OFF[item][1]
    return pos, off


PLAIN_POS, PLAIN_W = _layout(PLAIN_ORDER)
ROPE_POS, ROPE_W = _layout(ROPE_ORDER)
SMALL_COL = PLAIN_POS['fox_f']
SMALL_F = 0
SMALL_G = FOX_HEADS
SMALL_W = FOX_HEADS + 3 * NSA_HEADS


def _cparams(sem):
    return pltpu.CompilerParams(dimension_semantics=sem, vmem_limit_bytes=VMEM_LIMIT_BYTES)


def _nt_dot(a, b):
    return lax.dot_general(a, b, (((1,), (1,)), ((), ())), preferred_element_type=F32)


def _dot(a, b):
    return jnp.dot(a, b, preferred_element_type=F32)


def _softmax_rows(s):
    m = jnp.max(s, axis=-1, keepdims=True)
    p = jnp.exp(s - m)
    return p, jnp.sum(p, axis=-1, keepdims=True)


def _for_key_extent(q_last, seq, body, step=KEY_STEP):
    n = seq // step
    if n <= 1:
        body(seq)
        return
    c = q_last // step
    for i in range(n):
        pl.when(c == i)(functools.partial(body, (i + 1) * step))


def _layer_norm(z, g, b):
    mu = jnp.mean(z, axis=-1, keepdims=True)
    zc = z - mu
    var = jnp.mean(zc * zc, axis=-1, keepdims=True)
    return zc * lax.rsqrt(var + LN_EPS) * g + b


def _proj_kernel(x_ref, w_ref, o_ref, xb_ref):
    @pl.when(pl.program_id(1) == 0)
    def _():
        xb_ref[...] = x_ref[...].astype(BF16)

    o_ref[...] = _dot(xb_ref[...], w_ref[...])


def _proj_rope_kernel(x_ref, wa_ref, wb_ref, cos_ref, sin_ref, o_ref, xb_ref):
    @pl.when(pl.program_id(1) == 0)
    def _():
        xb_ref[...] = x_ref[...].astype(BF16)

    xb = xb_ref[...]
    o_ref[...] = _dot(xb, wa_ref[...]) * cos_ref[...] + _dot(xb, wb_ref[...]) * sin_ref[...]


def _project(x2d, w_plain, w_a, w_b, cos_t, sin_t, seq):
    n = x2d.shape[0]
    tm = min(1024, seq)
    tc = cos_t.shape[1]
    plain = pl.pallas_call(
        _proj_kernel,
        grid=(n // tm, PLAIN_W // tc),
        in_specs=[pl.BlockSpec((tm, D_MODEL), lambda i, j: (i, 0)),
                  pl.BlockSpec((D_MODEL, tc), lambda i, j: (0, j))],
        out_specs=pl.BlockSpec((tm, tc), lambda i, j: (i, j)),
        out_shape=jax.ShapeDtypeStruct((n, PLAIN_W), F32),
        scratch_shapes=[pltpu.VMEM((tm, D_MODEL), BF16)],
        compiler_params=_cparams(("parallel", "arbitrary")),
    )(x2d, w_plain)
    nt = seq // tm
    roped = pl.pallas_call(
        _proj_rope_kernel,
        grid=(n // tm, ROPE_W // tc),
        in_specs=[pl.BlockSpec((tm, D_MODEL), lambda i, j: (i, 0)),
                  pl.BlockSpec((D_MODEL, tc), lambda i, j: (0, j)),
                  pl.BlockSpec((D_MODEL, tc), lambda i, j: (0, j)),
                  pl.BlockSpec((tm, tc), lambda i, j: (i % nt, 0)),
                  pl.BlockSpec((tm, tc), lambda i, j: (i % nt, 0))],
        out_specs=pl.BlockSpec((tm, tc), lambda i, j: (i, j)),
        out_shape=jax.ShapeDtypeStruct((n, ROPE_W), F32),
        scratch_shapes=[pltpu.VMEM((tm, D_MODEL), BF16)],
        compiler_params=_cparams(("parallel", "arbitrary")),
    )(x2d, w_a, w_b, cos_t, sin_t)
    return plain, roped


def _fox_cum_kernel(f_ref, fb_ref, o_ref):
    x = f_ref[0] + fb_ref[...]
    c = jnp.minimum(x, 0.0) - jnp.log1p(jnp.exp(-jnp.abs(x)))
    seq = c.shape[-1]
    lane = lax.broadcasted_iota(I32, c.shape, 1)
    sh = 1
    while sh < seq:
        c = c + jnp.where(lane >= sh, pltpu.roll(c, sh, 1), 0.0)
        sh *= 2
    o_ref[0] = c


def _fox_cumsum(f_rows, f_bias):
    b, h, seq = f_rows.shape
    return pl.pallas_call(
        _fox_cum_kernel,
        grid=(b,),
        in_specs=[pl.BlockSpec((1, h, seq), lambda i: (i, 0, 0)),
                  pl.BlockSpec((h, 1), lambda i: (0, 0))],
        out_specs=pl.BlockSpec((1, h, seq), lambda i: (i, 0, 0)),
        out_shape=jax.ShapeDtypeStruct((b, h, seq), F32),
        compiler_params=_cparams(("parallel",)),
    )(f_rows, f_bias)


def _fox_kernel(q_ref, k_ref, v_ref, cc_ref, cr_ref, o_ref, *, tq, seq):
    q0 = pl.program_id(1) * tq

    def body(klen):
        qpos = q0 + lax.broadcasted_iota(I32, (tq, klen), 0)
        causal = lax.broadcasted_iota(I32, (tq, klen), 1) <= qpos
        for h in range(FOX_HEADS):
            sl = slice(h * HEAD_DIM, (h + 1) * HEAD_DIM)
            qh = (q_ref[0, :, sl] * SCALE).astype(BF16)
            kh = k_ref[0, :klen, sl].astype(BF16)
            vh = v_ref[0, :klen, sl].astype(BF16)
            s = _nt_dot(qh, kh) + (cc_ref[0, :, h:h + 1] - cr_ref[0, h:h + 1, :klen])
            p, l = _softmax_rows(jnp.where(causal, s, NEG))
            o_ref[0, :, sl] = (_dot(p.astype(BF16), vh) / l).astype(o_ref.dtype)

    _for_key_extent(q0 + tq - 1, seq, body, step=tq)


def _fox_attention(hp3, c_cols, c_rows):
    b, seq, _ = hp3.shape
    tq = min(256, seq)
    w = FOX_HEADS * HEAD_DIM
    return pl.pallas_call(
        functools.partial(_fox_kernel, tq=tq, seq=seq),
        grid=(b, seq // tq),
        in_specs=[pl.BlockSpec((1, tq, w), lambda i, j: (i, j, PLAIN_POS['fox_q'] // w)),
                  pl.BlockSpec((1, seq, w), lambda i, j: (i, 0, PLAIN_POS['fox_k'] // w)),
                  pl.BlockSpec((1, seq, w), lambda i, j: (i, 0, PLAIN_POS['fox_v'] // w)),
                  pl.BlockSpec((1, tq, FOX_HEADS), lambda i, j: (i, j, 0)),
                  pl.BlockSpec((1, FOX_HEADS, seq), lambda i, j: (i, 0, 0))],
        out_specs=pl.BlockSpec((1, tq, w), lambda i, j: (i, j, 0)),
        out_shape=jax.ShapeDtypeStruct((b, seq, w), BF16),
        compiler_params=_cparams(("parallel", "arbitrary")),
    )(hp3, hp3, hp3, c_cols, c_rows)


def _gelu_tanh(x):
    return 0.5 * x * (1.0 + jnp.tanh(np.float32(np.sqrt(2.0 / np.pi)) * (x + 0.044715 * (x * x * x))))


def _compress_kernel(rk_ref, rv_ref, pek_ref, pev_ref, w1k_ref, w2k_ref, w1v_ref, w2v_ref, ok_ref, ov_ref):
    half = CMP_STRIDE * HEAD_DIM

    def one(r_ref, pe_ref, w1_ref, w2_ref, o_ref):
        r = r_ref[0]
        n_rows = r.shape[0]
        lo = _dot((r + pe_ref[:, :half]).astype(BF16), w1_ref[:half, :])
        hi = _dot((r + pe_ref[:, half:]).astype(BF16), w1_ref[half:, :])
        hid = _gelu_tanh(lo + pltpu.roll(hi, n_rows - 1, 0))
        o_ref[0] = _dot(hid.astype(BF16), w2_ref[...])

    one(rk_ref, pek_ref, w1k_ref, w2k_ref, ok_ref)
    one(rv_ref, pev_ref, w1v_ref, w2v_ref, ov_ref)


def _nsa_compress(rk, rv, pek, pev, w1k, w2k, w1v, w2v):
    bg, rows, width = rk.shape
    tok = pl.BlockSpec((1, rows, width), lambda i: (i, 0, 0))
    full = lambda a: pl.BlockSpec(a.shape, lambda i: (0,) * a.ndim)
    out = pl.BlockSpec((1, rows, HEAD_DIM), lambda i: (i, 0, 0))
    return pl.pallas_call(
        _compress_kernel,
        grid=(bg,),
        in_specs=[tok, tok, full(pek), full(pev), full(w1k), full(w2k), full(w1v), full(w2v)],
        out_specs=[out, out],
        out_shape=[jax.ShapeDtypeStruct((bg, rows, HEAD_DIM), F32)] * 2,
        compiler_params=_cparams(("parallel",)),
    )(rk, rv, pek, pev, w1k, w2k, w1v, w2v)


def _split3_nt_dot(b01, a):
    a1 = a.astype(BF16)
    r1 = a - a1.astype(F32)
    a2 = r1.astype(BF16)
    a3 = (r1 - a2.astype(F32)).astype(BF16)
    return _nt_dot(b01, a1) + _nt_dot(b01, a2) + _nt_dot(b01, a3)


def _nsa_kernel(q_ref, kc_ref, vc_ref, ks_ref, kw_ref, vs_ref, vw_ref, sm_ref, o_ref, oslc_ref, *, tq, seq, wlen):
    q0 = pl.program_id(1) * tq
    n_c = kc_ref.shape[2]
    n_s = seq // SEL_BLOCK
    rows = NSA_REP * tq
    tcol = q0 + lax.broadcasted_iota(I32, (tq, 1), 0)
    tcol_r = jnp.concatenate([tcol] * NSA_REP, axis=0)

    cidx = lax.broadcasted_iota(I32, (rows, n_c), 1)
    vis_r = (cidx * CMP_STRIDE + (CMP_LEN - 1)) <= tcol_r
    any_vis_r = (tcol_r >= (CMP_LEN - 1)).astype(F32)
    oc = lax.broadcasted_iota(I32, (n_s, n_c), 1) * CMP_STRIDE
    ob = lax.broadcasted_iota(I32, (n_s, n_c), 0) * SEL_BLOCK
    overlap_t = ((oc < ob + SEL_BLOCK) & (oc + CMP_LEN > ob)).astype(BF16)
    jj_t = lax.broadcasted_iota(I32, (n_s, tq), 0)
    blk_t = (q0 + lax.broadcasted_iota(I32, (1, tq), 1)) // SEL_BLOCK
    w0 = pl.multiple_of(jnp.maximum(q0 + tq - wlen, 0), 8)
    dist = tcol_r - (w0 + lax.broadcasted_iota(I32, (rows, wlen), 1))
    band_r = (dist >= 0) & (dist < WINDOW)
    gates = sm_ref[0]

    qs_all, qs_aug_all, o_cmp_all = [], [], []
    for g in range(NSA_KV_HEADS):
        qs = (jnp.concatenate(
            [q_ref[0, :, (g * NSA_REP + r) * HEAD_DIM:(g * NSA_REP + r + 1) * HEAD_DIM] for r in range(NSA_REP)],
            axis=0) * SCALE).astype(BF16)

        s_c = _nt_dot(qs, kc_ref[0, g].astype(BF16))
        p_c, l_c = _softmax_rows(jnp.where(vis_r, s_c, NEG))
        p_c = p_c / l_c * any_vis_r
        o_cmp_all.append(_dot(p_c.astype(BF16), vc_ref[0, g].astype(BF16)))

        p_sum = p_c[0:tq]
        for r in range(1, NSA_REP):
            p_sum = p_sum + p_c[r * tq:(r + 1) * tq]
        imp = _split3_nt_dot(overlap_t, p_sum)
        forced = (jj_t == 0) | (jj_t == blk_t) | (jj_t == blk_t - 1)
        imp = jnp.where(jj_t <= blk_t, jnp.where(forced, FORCE, imp), -1.0)
        rank = jnp.zeros((n_s, tq), F32)
        for j2 in range(n_s):
            row = imp[j2:j2 + 1, :]
            beats = (row > imp) | ((row == imp) & (j2 < jj_t))
            rank = rank + beats.astype(F32)
        sel = (rank < float(min(SEL_TOPN, n_s))) & (jj_t <= blk_t)
        bias_t = jnp.concatenate([jnp.where(sel, 0.0, NEG), jnp.zeros((tq - n_s, tq), F32)], axis=0)
        sel_bias = bias_t.T[:, :LANES - HEAD_DIM]
        qs_all.append(qs)
        qs_aug_all.append(jnp.concatenate([qs, jnp.concatenate([sel_bias] * NSA_REP, axis=0).astype(BF16)], axis=1))

    def slc_body(klen):
        block_onehot = (lax.broadcasted_iota(I32, (klen, LANES - HEAD_DIM), 0) // SEL_BLOCK
                        == lax.broadcasted_iota(I32, (klen, LANES - HEAD_DIM), 1)).astype(BF16)
        causal_r = lax.broadcasted_iota(I32, (rows, klen), 1) <= tcol_r
        for g in range(NSA_KV_HEADS):
            ksl = slice(g * HEAD_DIM, (g + 1) * HEAD_DIM)
            k_aug = jnp.concatenate([ks_ref[0, :klen, ksl].astype(BF16), block_onehot], axis=1)
            p_s, l_s = _softmax_rows(jnp.where(causal_r, _nt_dot(qs_aug_all[g], k_aug), NEG))
            oslc_ref[g] = _dot(p_s.astype(BF16), vs_ref[0, :klen, ksl].astype(BF16)) / l_s

    _for_key_extent(q0 + tq - 1, seq, slc_body)

    for g in range(NSA_KV_HEADS):
        ksl = slice(g * HEAD_DIM, (g + 1) * HEAD_DIM)
        o_cmp, o_slc = o_cmp_all[g], oslc_ref[g]

        kw = kw_ref[0, pl.ds(w0, wlen), ksl].astype(BF16)
        vw = vw_ref[0, pl.ds(w0, wlen), ksl].astype(BF16)
        p_w, l_w = _softmax_rows(jnp.where(band_r, _nt_dot(qs_all[g], kw), NEG))
        o_win = _dot(p_w.astype(BF16), vw) / l_w

        for r in range(NSA_REP):
            head = g * NSA_REP + r
            rs = slice(r * tq, (r + 1) * tq)

            def gate(branch, head=head):
                col = SMALL_G + branch * NSA_HEADS + head
                return 1.0 / (1.0 + jnp.exp(-gates[:, col:col + 1]))

            out = gate(0) * o_cmp[rs] + gate(1) * o_slc[rs] + gate(2) * o_win[rs]
            o_ref[0, :, head * HEAD_DIM:(head + 1) * HEAD_DIM] = out.astype(o_ref.dtype)


def _nsa_attention(hr3, hp3, k_cmp, v_cmp):
    b, seq, _ = hr3.shape
    tq = min(256, seq)
    wlen = min(WINDOW + tq, seq)
    qw = NSA_HEADS * HEAD_DIM
    kvw = NSA_KV_HEADS * HEAD_DIM
    n_c = k_cmp.shape[2]
    cmp_spec = pl.BlockSpec((1, NSA_KV_HEADS, n_c, HEAD_DIM), lambda i, j: (i, 0, 0, 0))

    def seq_spec(col):
        return pl.BlockSpec((1, seq, kvw), lambda i, j: (i, 0, col // kvw))

    return pl.pallas_call(
        functools.partial(_nsa_kernel, tq=tq, seq=seq, wlen=wlen),
        grid=(b, seq // tq),
        in_specs=[pl.BlockSpec((1, tq, qw), lambda i, j: (i, j, ROPE_POS['nsa_q'] // qw)),
                  cmp_spec, cmp_spec,
                  seq_spec(ROPE_POS['nsa_ks']), seq_spec(ROPE_POS['nsa_kw']),
                  seq_spec(PLAIN_POS['nsa_vs']), seq_spec(PLAIN_POS['nsa_vw']),
                  pl.BlockSpec((1, tq, LANES), lambda i, j: (i, j, SMALL_COL // LANES))],
        out_specs=pl.BlockSpec((1, tq, qw), lambda i, j: (i, j, 0)),
        out_shape=jax.ShapeDtypeStruct((b, seq, qw), BF16),
        scratch_shapes=[pltpu.VMEM((NSA_KV_HEADS, NSA_REP * tq, HEAD_DIM), F32)],
        compiler_params=_cparams(("parallel", "arbitrary")),
    )(hr3, k_cmp, v_cmp, hr3, hr3, hp3, hp3, hp3)


def _dsa_kernel(q_ref, kk_ref, v_ref, iq_ref, wt_ref, o_ref, key_ref, bias_ref, *, tq, seq, topk):
    q0 = pl.program_id(1) * tq
    trow = q0 + lax.broadcasted_iota(I32, (1, tq), 1)
    w_idx = wt_ref[0] * (IDX_HEADS ** -0.5) * (IDX_DIM ** -0.5)
    kf = float(topk)

    def body(klen):
        ik = kk_ref[0, :klen, HEAD_DIM:2 * HEAD_DIM].astype(BF16)
        score = jnp.zeros((klen, tq), F32)
        for h in range(IDX_HEADS):
            d = _nt_dot(ik, iq_ref[0, :, h * IDX_DIM:(h + 1) * IDX_DIM].astype(BF16))
            score = score + w_idx[h:h + 1, :] * jnp.maximum(d, 0.0)
        score = jnp.where(lax.broadcasted_iota(I32, (klen, tq), 0) <= trow, score, NEG)

        bits = lax.bitcast_convert_type(score, I32)
        key_ref[:klen, :] = jnp.where(bits < 0, bits ^ 0x7FFFFFFF, bits)

        def count(compare, bound):
            bound8 = jnp.broadcast_to(bound, (8, tq))
            accs = [jnp.zeros((8, tq), F32) for _ in range(4)]
            for j in range(klen // 8):
                accs[j % 4] = accs[j % 4] + compare(key_ref[j * 8:(j + 1) * 8, :], bound8).astype(F32)
            return jnp.sum((accs[0] + accs[1]) + (accs[2] + accs[3]), axis=0, keepdims=True)

        def tau_step(i, tau):
            cand = tau + jnp.left_shift(jnp.int32(1), 31 - i)
            return jnp.where(count(jnp.greater_equal, cand) >= kf, cand, tau)

        tau = lax.fori_loop(0, 32, tau_step, jnp.full((1, tq), -2 ** 31, I32))
        need = kf - count(jnp.greater, tau)

        tri = (lax.broadcasted_iota(I32, (LANES, LANES), 0) >= lax.broadcasted_iota(I32, (LANES, LANES), 1)).astype(BF16)
        seen = jnp.zeros((1, tq), F32)
        for blk in range(klen // LANES):
            ks = slice(blk * LANES, (blk + 1) * LANES)
            key = key_ref[ks, :]
            tie = key == tau
            running = _dot(tri, jnp.where(tie, 1.0, 0.0).astype(BF16)) + seen
            seen = running[LANES - 1:LANES, :]
            kpos = blk * LANES + lax.broadcasted_iota(I32, (LANES, tq), 0)
            chosen = ((key > tau) | (tie & (running <= need))) & (kpos <= trow)
            bias_ref[:, ks] = jnp.where(chosen, 0.0, NEG).T

        k = kk_ref[0, :klen, 0:HEAD_DIM].astype(BF16)
        v = v_ref[0, :klen, 0:HEAD_DIM].astype(BF16)
        bias = bias_ref[:, :klen]
        for h in range(DSA_HEADS):
            sl = slice(h * HEAD_DIM, (h + 1) * HEAD_DIM)
            s = _nt_dot((q_ref[0, :, sl] * SCALE).astype(BF16), k)
            p, l = _softmax_rows(s + bias)
            o_ref[0, :, sl] = (_dot(p.astype(BF16), v) / l).astype(o_ref.dtype)

    _for_key_extent(q0 + tq - 1, seq, body)


def _dsa_attention(hr3, hp3, iw_t):
    b, seq, _ = hr3.shape
    tq = min(256, seq)
    topk = min(DSA_TOPK, seq // 4)
    qw = DSA_HEADS * HEAD_DIM
    iqw = IDX_HEADS * IDX_DIM
    return pl.pallas_call(
        functools.partial(_dsa_kernel, tq=tq, seq=seq, topk=topk),
        grid=(b, seq // tq),
        in_specs=[pl.BlockSpec((1, tq, qw), lambda i, j: (i, j, ROPE_POS['dsa_q'] // qw)),
                  pl.BlockSpec((1, seq, LANES), lambda i, j: (i, 0, ROPE_POS['dsa_k'] // LANES)),
                  pl.BlockSpec((1, seq, LANES), lambda i, j: (i, 0, PLAIN_POS['dsa_v'] // LANES)),
                  pl.BlockSpec((1, tq, iqw), lambda i, j: (i, j, ROPE_POS['idx_q'] // iqw)),
                  pl.BlockSpec((1, IDX_HEADS, tq), lambda i, j: (i, 0, j))],
        out_specs=pl.BlockSpec((1, tq, qw), lambda i, j: (i, j, 0)),
        out_shape=jax.ShapeDtypeStruct((b, seq, qw), BF16),
        scratch_shapes=[pltpu.VMEM((seq, tq), I32), pltpu.VMEM((tq, seq), F32)],
        compiler_params=_cparams(("parallel", "arbitrary")),
    )(hr3, hr3, hp3, hr3, iw_t)


def _outproj_kernel(of_ref, on_ref, od_ref, x_ref, wf_ref, wn_ref, wd_ref, g_ref, b_ref, o_ref, *, alpha):
    mix = _dot(of_ref[...], wf_ref[...]) + _dot(on_ref[...], wn_ref[...]) + _dot(od_ref[...], wd_ref[...])
    o_ref[...] = _layer_norm(alpha * x_ref[...] + mix, g_ref[...], b_ref[...])


def _outproj_ln(o_fox, o_nsa, o_dsa, x2d, w_f, w_n, w_d, g, b, alpha):
    n = x2d.shape[0]
    tm = min(512, n)
    row = lambda a: pl.BlockSpec((tm, a.shape[1]), lambda i: (i, 0))
    full = lambda a: pl.BlockSpec(a.shape, lambda i: (0, 0))
    return pl.pallas_call(
        functools.partial(_outproj_kernel, alpha=alpha),
        grid=(n // tm,),
        in_specs=[row(o_fox), row(o_nsa), row(o_dsa), row(x2d), full(w_f), full(w_n), full(w_d), full(g), full(b)],
        out_specs=pl.BlockSpec((tm, D_MODEL), lambda i: (i, 0)),
        out_shape=jax.ShapeDtypeStruct((n, D_MODEL), F32),
        compiler_params=_cparams(("parallel",)),
    )(o_fox, o_nsa, o_dsa, x2d, w_f, w_n, w_d, g, b)


def _router_kernel(x_ref, wr_ref, br_ref, g_ref):
    tn = x_ref.shape[0]
    logits = _nt_dot(wr_ref[...], x_ref[...].astype(BF16))
    s = 1.0 / (1.0 + jnp.exp(-logits))
    sb = s + br_ref[...]
    low = jnp.float32(-3e38)

    grp = []
    for gi in range(N_GROUPS):
        blk = sb[gi * GROUP_SIZE:(gi + 1) * GROUP_SIZE]
        m1 = jnp.max(blk, axis=0, keepdims=True)
        is_max = blk == m1
        n_max = jnp.sum(is_max.astype(F32), axis=0, keepdims=True)
        m2 = jnp.max(jnp.where(is_max, low, blk), axis=0, keepdims=True)
        grp.append(m1 + jnp.where(n_max >= 2.0, m1, m2))
    masked = []
    for gi in range(N_GROUPS):
        rank = jnp.zeros((1, tn), F32)
        for g2 in range(N_GROUPS):
            if g2 == gi:
                continue
            beats = (grp[g2] > grp[gi]) | ((grp[g2] == grp[gi]) if g2 < gi else False)
            rank = rank + beats.astype(F32)
        keep = rank < float(TOPK_GROUPS)
        masked.append(jnp.where(keep, sb[gi * GROUP_SIZE:(gi + 1) * GROUP_SIZE], NEG))
    masked = jnp.concatenate(masked, axis=0)

    eidx = lax.broadcasted_iota(I32, (N_EXPERTS, tn), 0)
    rank = jnp.zeros((N_EXPERTS, tn), F32)
    for e2 in range(N_EXPERTS):
        row = masked[e2:e2 + 1]
        beats = (row > masked) | ((row == masked) & (e2 < eidx))
        rank = rank + beats.astype(F32)
    gw = jnp.where(rank < float(MOE_TOPK), s, 0.0)
    g_ref[...] = gw / jnp.sum(gw, axis=0, keepdims=True) * ROUTED_SCALE


def _router(x2d, wr_t, br_col):
    n = x2d.shape[0]
    tn = min(512, n)
    return pl.pallas_call(
        _router_kernel,
        grid=(n // tn,),
        in_specs=[pl.BlockSpec((tn, D_MODEL), lambda i: (i, 0)),
                  pl.BlockSpec(wr_t.shape, lambda i: (0, 0)),
                  pl.BlockSpec(br_col.shape, lambda i: (0, 0))],
        out_specs=pl.BlockSpec((N_EXPERTS, tn), lambda i: (0, i)),
        out_shape=jax.ShapeDtypeStruct((N_EXPERTS, n), F32),
        compiler_params=_cparams(("parallel",)),
    )(x2d, wr_t, br_col)


def _silu(x):
    return x / (1.0 + jnp.exp(-x))


def _moe_kernel(x_ref, gt_ref, wg_ref, wu_ref, wd_ref, sg_ref, su_ref, sd_ref, lg_ref, lb_ref, o_ref,
                xb_ref, acc_ref, *, alpha):
    e = pl.program_id(1)

    @pl.when(e == 0)
    def _():
        xb = x_ref[...].astype(BF16)
        xb_ref[...] = xb
        h = _silu(_dot(xb, sg_ref[...])) * _dot(xb, su_ref[...])
        acc_ref[...] = _dot(h.astype(BF16), sd_ref[...])

    xb = xb_ref[...]
    h = (_silu(_dot(xb, wg_ref[0, 0].astype(BF16))) * _dot(xb, wu_ref[0, 0].astype(BF16))).astype(BF16)
    gates = gt_ref[...]
    lane = lax.broadcasted_iota(I32, gates.shape, 1)
    gcol = jnp.sum(jnp.where(lane == e, gates, 0.0), axis=1, keepdims=True)
    chunk = 256
    for c in range(D_MODEL // chunk):
        cs = slice(c * chunk, (c + 1) * chunk)
        acc_ref[:, cs] += gcol * _dot(h, wd_ref[0, 0, :, cs].astype(BF16))

    @pl.when(e == pl.num_programs(1) - 1)
    def _():
        o_ref[...] = _layer_norm(alpha * x_ref[...] + acc_ref[...], lg_ref[...], lb_ref[...])


def _moe_ln(x2d, gates, layer, w_gate, w_up, w_down, ws_gate, ws_up, ws_down, g, b, alpha):
    n = x2d.shape[0]
    tn = min(2048, n)
    full = lambda a: pl.BlockSpec(a.shape, lambda i, e: (0,) * a.ndim)
    once = pl.Buffered(1)
    return pl.pallas_call(
        functools.partial(_moe_kernel, alpha=alpha),
        grid=(n // tn, N_EXPERTS),
        in_specs=[pl.BlockSpec((tn, D_MODEL), lambda i, e: (i, 0), pipeline_mode=once),
                  pl.BlockSpec((tn, N_EXPERTS), lambda i, e: (i, 0)),
                  pl.BlockSpec((1, 1, D_MODEL, EXPERT_DIM), lambda i, e: (layer, e, 0, 0)),
                  pl.BlockSpec((1, 1, D_MODEL, EXPERT_DIM), lambda i, e: (layer, e, 0, 0)),
                  pl.BlockSpec((1, 1, EXPERT_DIM, D_MODEL), lambda i, e: (layer, e, 0, 0)),
                  full(ws_gate), full(ws_up), full(ws_down), full(g), full(b)],
        out_specs=pl.BlockSpec((tn, D_MODEL), lambda i, e: (i, 0), pipeline_mode=once),
        out_shape=jax.ShapeDtypeStruct((n, D_MODEL), F32),
        scratch_shapes=[pltpu.VMEM((tn, D_MODEL), BF16), pltpu.VMEM((tn, D_MODEL), F32)],
        compiler_params=_cparams(("parallel", "arbitrary")),
    )(x2d, gates, w_gate, w_up, w_down, ws_gate, ws_up, ws_down, g, b)


def _gather_cols(w, order):
    parts = []
    for item in order:
        if isinstance(item, tuple):
            parts.append(jnp.zeros(w.shape[:-1] + (item[1],), w.dtype))
        else:
            off, width = SEG_OFF[item]
            parts.append(w[..., off:off + width])
    return jnp.concatenate(parts, axis=-1)


def _rotate_half_cols(w):
    lead = w.shape[:-1]
    w4 = w.reshape(lead + (-1, 2, HEAD_DIM // 2))
    return jnp.concatenate([-w4[..., 1:2, :], w4[..., 0:1, :]], axis=-2).reshape(w.shape)


def _rope_tables(seq, width):
    half = HEAD_DIM // 2
    inv_freq = ROPE_THETA ** (-jnp.arange(half, dtype=F32) * (2.0 / HEAD_DIM))
    ang = jnp.arange(seq).astype(F32)[:, None] * inv_freq[None, :]
    reps = width // half
    return jnp.tile(jnp.cos(ang), (1, reps)), jnp.tile(jnp.sin(ang), (1, reps))


def _token_rows(tok, b, seq):
    t4 = tok.reshape(b, seq, NSA_KV_HEADS, HEAD_DIM).transpose(0, 2, 1, 3)
    return t4.reshape(b * NSA_KV_HEADS, seq // CMP_STRIDE, CMP_STRIDE * HEAD_DIM)


def kernel(x, w_in, fox_forget_bias, cmp_pos_k, cmp_w1_k, cmp_w2_k, cmp_pos_v, cmp_w1_v, cmp_w2_v, w_out,
           ln1_g, ln1_b, w_router, b_router, w_gate, w_up, w_down, ws_gate, ws_up, ws_down, ln2_g, ln2_b):
    b, seq, dm = x.shape
    depth = w_in.shape[0]
    n = b * seq
    alpha = float((2 * depth) ** 0.25)

    w_plain = _gather_cols(w_in, PLAIN_ORDER).astype(BF16)
    w_rope = _gather_cols(w_in, ROPE_ORDER)
    w_rope_a = w_rope.astype(BF16)
    w_rope_b = _rotate_half_cols(w_rope).astype(BF16)
    cos_t, sin_t = _rope_tables(seq, PROJ_TILE_COLS)
    fw, nw = FOX_HEADS * HEAD_DIM, NSA_HEADS * HEAD_DIM
    w_out_b = w_out.astype(BF16)

    x2d = x.reshape(n, dm)
    for l in range(depth):
        hp, hr = _project(x2d, w_plain[l], w_rope_a[l], w_rope_b[l], cos_t, sin_t, seq)
        hp3 = hp.reshape(b, seq, PLAIN_W)
        hr3 = hr.reshape(b, seq, ROPE_W)

        f_logit = hp3[:, :, SMALL_COL + SMALL_F:SMALL_COL + SMALL_F + FOX_HEADS]
        c_rows = _fox_cumsum(f_logit.transpose(0, 2, 1), fox_forget_bias[l].reshape(FOX_HEADS, 1))
        o_fox = _fox_attention(hp3, c_rows.transpose(0, 2, 1), c_rows)

        kvw = NSA_KV_HEADS * HEAD_DIM
        rk = _token_rows(hr3[:, :, ROPE_POS['nsa_kc']:ROPE_POS['nsa_kc'] + kvw], b, seq)
        rv = _token_rows(hp3[:, :, PLAIN_POS['nsa_vc']:PLAIN_POS['nsa_vc'] + kvw], b, seq)
        k_cmp, v_cmp = _nsa_compress(
            rk, rv, cmp_pos_k[l].reshape(1, -1), cmp_pos_v[l].reshape(1, -1),
            cmp_w1_k[l].astype(BF16), cmp_w2_k[l].astype(BF16), cmp_w1_v[l].astype(BF16), cmp_w2_v[l].astype(BF16))
        n_rows = seq // CMP_STRIDE
        o_nsa = _nsa_attention(hr3, hp3, k_cmp.reshape(b, NSA_KV_HEADS, n_rows, HEAD_DIM),
                               v_cmp.reshape(b, NSA_KV_HEADS, n_rows, HEAD_DIM))

        iw_t = hp3[:, :, SMALL_COL + SMALL_W:SMALL_COL + SMALL_W + IDX_HEADS].transpose(0, 2, 1)
        o_dsa = _dsa_attention(hr3, hp3, iw_t)

        x2d = _outproj_ln(o_fox.reshape(n, fw), o_nsa.reshape(n, nw), o_dsa.reshape(n, nw), x2d,
                          w_out_b[l, :fw], w_out_b[l, fw:fw + nw], w_out_b[l, fw + nw:],
                          ln1_g[l].reshape(1, dm), ln1_b[l].reshape(1, dm), alpha)

        gates_t = _router(x2d, w_router[l].T.astype(BF16), b_router[l].reshape(N_EXPERTS, 1))
        x2d = _moe_ln(x2d, gates_t.T, l, w_gate, w_up, w_down,
                      ws_gate[l].astype(BF16), ws_up[l].astype(BF16), ws_down[l].astype(BF16),
                      ln2_g[l].reshape(1, dm), ln2_b[l].reshape(1, dm), alpha)
    return x2d.reshape(b, seq, dm)
```

```python
import functools

import numpy as np
import jax
import jax.numpy as jnp
from jax import lax
from jax.experimental import pallas as pl
from jax.experimental.pallas import tpu as pltpu

D_MODEL = 1024
HEAD_DIM = 64
FOX_HEADS = 4
NSA_HEADS = 6
NSA_KV_HEADS = 2
NSA_REP = NSA_HEADS // NSA_KV_HEADS
DSA_HEADS = 6
ROPE_THETA = 10000.0
CMP_LEN = 32
CMP_STRIDE = 16
CMP_HIDDEN = 2 * HEAD_DIM
SEL_BLOCK = 64
SEL_TOPN = 16
WINDOW = 512
IDX_HEADS = 4
IDX_DIM = 64
DSA_TOPK = 256
N_EXPERTS = 64
N_GROUPS = 8
GROUP_SIZE = N_EXPERTS // N_GROUPS
TOPK_GROUPS = 4
MOE_TOPK = 8
EXPERT_DIM = 256
SHARED_DIM = 256
ROUTED_SCALE = 2.5
LN_EPS = 1e-5
NEG = -1e30
FORCE = 1e6
SCALE = HEAD_DIM ** -0.5

F32 = jnp.float32
BF16 = jnp.bfloat16
I32 = jnp.int32

VMEM_LIMIT_BYTES = 52 * 1024 * 1024
LANES = 128
KEY_STEP = 512
PROJ_TILE_COLS = 512

SEGMENTS = (
    ('fox_q', FOX_HEADS * HEAD_DIM), ('fox_k', FOX_HEADS * HEAD_DIM),
    ('fox_v', FOX_HEADS * HEAD_DIM), ('fox_f', FOX_HEADS),
    ('nsa_q', NSA_HEADS * HEAD_DIM),
    ('nsa_kc', NSA_KV_HEADS * HEAD_DIM), ('nsa_vc', NSA_KV_HEADS * HEAD_DIM),
    ('nsa_ks', NSA_KV_HEADS * HEAD_DIM), ('nsa_vs', NSA_KV_HEADS * HEAD_DIM),
    ('nsa_kw', NSA_KV_HEADS * HEAD_DIM), ('nsa_vw', NSA_KV_HEADS * HEAD_DIM),
    ('nsa_g', 3 * NSA_HEADS),
    ('dsa_q', DSA_HEADS * HEAD_DIM), ('dsa_k', HEAD_DIM), ('dsa_v', HEAD_DIM),
    ('idx_q', IDX_HEADS * IDX_DIM), ('idx_k', IDX_DIM), ('idx_w', IDX_HEADS),
)
SEG_OFF = {}
_off = 0
for _name, _width in SEGMENTS:
    SEG_OFF[_name] = (_off, _width)
    _off += _width
N_IN = _off

PLAIN_ORDER = ('fox_q', 'fox_k', 'fox_v', 'nsa_vc', 'nsa_vs', 'nsa_vw', 'dsa_v', (None, 64),
               'fox_f', 'nsa_g', 'idx_w', (None, 2 * LANES - FOX_HEADS - 3 * NSA_HEADS - IDX_HEADS))
ROPE_ORDER = ('nsa_q', 'nsa_kc', 'nsa_ks', 'nsa_kw', 'dsa_q', 'dsa_k', 'idx_k', 'idx_q')


def _layout(order):
    pos, off = {}, 0
    for item in order:
        if isinstance(item, tuple):
            off += item[1]
        else:
            pos[item] = off
            off += SEG_OFF[item][1]
    return pos, off


PLAIN_POS, PLAIN_W = _layout(PLAIN_ORDER)
ROPE_POS, ROPE_W = _layout(ROPE_ORDER)
SMALL_COL = PLAIN_POS['fox_f']
SMALL_F = 0
SMALL_G = FOX_HEADS
SMALL_W = FOX_HEADS + 3 * NSA_HEADS


def _cparams(sem):
    return pltpu.CompilerParams(dimension_semantics=sem, vmem_limit_bytes=VMEM_LIMIT_BYTES)


def _nt_dot(a, b):
    return lax.dot_general(a, b, (((1,), (1,)), ((), ())), preferred_element_type=F32)


def _dot(a, b):
    return jnp.dot(a, b, preferred_element_type=F32)


def _softmax_rows(s):
    m = jnp.max(s, axis=-1, keepdims=True)
    p = jnp.exp(s - m)
    return p, jnp.sum(p, axis=-1, keepdims=True)


def _for_key_extent(q_last, seq, body, step=KEY_STEP):
    n = seq // step
    if n <= 1:
        body(seq)
        return
    c = q_last // step
    for i in range(n):
        pl.when(c == i)(functools.partial(body, (i + 1) * step))


def _col_reduce(x, op):
    n = x.shape[0] // 8
    accs = [x[j * 8:(j + 1) * 8] for j in range(min(4, n))]
    for j in range(4, n):
        accs[j % 4] = op(accs[j % 4], x[j * 8:(j + 1) * 8])
    while len(accs) > 1:
        accs = [op(accs[i], accs[i + 1]) if i + 1 < len(accs) else accs[i] for i in range(0, len(accs), 2)]
    return accs[0]


def _attend_cols(s_t, v):
    m = jnp.max(_col_reduce(s_t, jnp.maximum), axis=0, keepdims=True)
    p = jnp.exp(s_t - m)
    l = jnp.sum(_col_reduce(p, jnp.add), axis=0, keepdims=True)
    o_t = lax.dot_general(v, p.astype(BF16), (((0,), (0,)), ((), ())), preferred_element_type=F32) / l
    return o_t.T


def _layer_norm(z, g, b):
    mu = jnp.mean(z, axis=-1, keepdims=True)
    zc = z - mu
    var = jnp.mean(zc * zc, axis=-1, keepdims=True)
    return zc * lax.rsqrt(var + LN_EPS) * g + b


def _proj_kernel(x_ref, w_ref, o_ref, xb_ref):
    @pl.when(pl.program_id(1) == 0)
    def _():
        xb_ref[...] = x_ref[...].astype(BF16)

    o_ref[...] = _dot(xb_ref[...], w_ref[...])


def _proj_rope_kernel(x_ref, wa_ref, wb_ref, cos_ref, sin_ref, o_ref, xb_ref):
    @pl.when(pl.program_id(1) == 0)
    def _():
        xb_ref[...] = x_ref[...].astype(BF16)

    xb = xb_ref[...]
    o_ref[...] = _dot(xb, wa_ref[...]) * cos_ref[...] + _dot(xb, wb_ref[...]) * sin_ref[...]


def _project(x2d, w_plain, w_a, w_b, cos_t, sin_t, seq):
    n = x2d.shape[0]
    tm = min(1024, seq)
    tc = cos_t.shape[1]
    plain = pl.pallas_call(
        _proj_kernel,
        grid=(n // tm, PLAIN_W // tc),
        in_specs=[pl.BlockSpec((tm, D_MODEL), lambda i, j: (i, 0)),
                  pl.BlockSpec((D_MODEL, tc), lambda i, j: (0, j))],
        out_specs=pl.BlockSpec((tm, tc), lambda i, j: (i, j)),
        out_shape=jax.ShapeDtypeStruct((n, PLAIN_W), F32),
        scratch_shapes=[pltpu.VMEM((tm, D_MODEL), BF16)],
        compiler_params=_cparams(("parallel", "arbitrary")),
    )(x2d, w_plain)
    nt = seq // tm
    roped = pl.pallas_call(
        _proj_rope_kernel,
        grid=(n // tm, ROPE_W // tc),
        in_specs=[pl.BlockSpec((tm, D_MODEL), lambda i, j: (i, 0)),
                  pl.BlockSpec((D_MODEL, tc), lambda i, j: (0, j)),
                  pl.BlockSpec((D_MODEL, tc), lambda i, j: (0, j)),
                  pl.BlockSpec((tm, tc), lambda i, j: (i % nt, 0)),
                  pl.BlockSpec((tm, tc), lambda i, j: (i % nt, 0))],
        out_specs=pl.BlockSpec((tm, tc), lambda i, j: (i, j)),
        out_shape=jax.ShapeDtypeStruct((n, ROPE_W), F32),
        scratch_shapes=[pltpu.VMEM((tm, D_MODEL), BF16)],
        compiler_params=_cparams(("parallel", "arbitrary")),
    )(x2d, w_a, w_b, cos_t, sin_t)
    return plain, roped


def _fox_cum_kernel(f_ref, fb_ref, o_ref):
    x = f_ref[0] + fb_ref[...]
    c = jnp.minimum(x, 0.0) - jnp.log1p(jnp.exp(-jnp.abs(x)))
    seq = c.shape[-1]
    lane = lax.broadcasted_iota(I32, c.shape, 1)
    sh = 1
    while sh < seq:
        c = c + jnp.where(lane >= sh, pltpu.roll(c, sh, 1), 0.0)
        sh *= 2
    o_ref[0] = c


def _fox_cumsum(f_rows, f_bias):
    b, h, seq = f_rows.shape
    return pl.pallas_call(
        _fox_cum_kernel,
        grid=(b,),
        in_specs=[pl.BlockSpec((1, h, seq), lambda i: (i, 0, 0)),
                  pl.BlockSpec((h, 1), lambda i: (0, 0))],
        out_specs=pl.BlockSpec((1, h, seq), lambda i: (i, 0, 0)),
        out_shape=jax.ShapeDtypeStruct((b, h, seq), F32),
        compiler_params=_cparams(("parallel",)),
    )(f_rows, f_bias)


def _fox_kernel(q_ref, k_ref, v_ref, cc_ref, cr_ref, o_ref, *, tq, seq):
    q0 = pl.program_id(1) * tq

    def body(klen):
        qpos = q0 + lax.broadcasted_iota(I32, (klen, tq), 1)
        causal = lax.broadcasted_iota(I32, (klen, tq), 0) <= qpos
        for h in range(FOX_HEADS):
            sl = slice(h * HEAD_DIM, (h + 1) * HEAD_DIM)
            qh = (q_ref[0, :, sl] * SCALE).astype(BF16)
            kh = k_ref[0, :klen, sl].astype(BF16)
            vh = v_ref[0, :klen, sl].astype(BF16)
            s = _nt_dot(kh, qh) + (cr_ref[0, h:h + 1, :] - cc_ref[0, :klen, h:h + 1])
            o_ref[0, :, sl] = _attend_cols(jnp.where(causal, s, NEG), vh).astype(o_ref.dtype)

    _for_key_extent(q0 + tq - 1, seq, body, step=tq)


def _fox_attention(hp3, c_cols, c_rows):
    b, seq, _ = hp3.shape
    tq = min(256, seq)
    w = FOX_HEADS * HEAD_DIM
    return pl.pallas_call(
        functools.partial(_fox_kernel, tq=tq, seq=seq),
        grid=(b, seq // tq),
        in_specs=[pl.BlockSpec((1, tq, w), lambda i, j: (i, j, PLAIN_POS['fox_q'] // w)),
                  pl.BlockSpec((1, seq, w), lambda i, j: (i, 0, PLAIN_POS['fox_k'] // w)),
                  pl.BlockSpec((1, seq, w), lambda i, j: (i, 0, PLAIN_POS['fox_v'] // w)),
                  pl.BlockSpec((1, seq, FOX_HEADS), lambda i, j: (i, 0, 0)),
                  pl.BlockSpec((1, FOX_HEADS, tq), lambda i, j: (i, 0, j))],
        out_specs=pl.BlockSpec((1, tq, w), lambda i, j: (i, j, 0)),
        out_shape=jax.ShapeDtypeStruct((b, seq, w), BF16),
        compiler_params=_cparams(("parallel", "arbitrary")),
    )(hp3, hp3, hp3, c_cols, c_rows)


def _gelu_tanh(x):
    return 0.5 * x * (1.0 + jnp.tanh(np.float32(np.sqrt(2.0 / np.pi)) * (x + 0.044715 * (x * x * x))))


def _compress_kernel(rk_ref, rv_ref, pek_ref, pev_ref, w1k_ref, w2k_ref, w1v_ref, w2v_ref, ok_ref, ov_ref):
    half = CMP_STRIDE * HEAD_DIM

    def one(r_ref, pe_ref, w1_ref, w2_ref, o_ref):
        r = r_ref[0]
        n_rows = r.shape[0]
        lo = _dot((r + pe_ref[:, :half]).astype(BF16), w1_ref[:half, :])
        hi = _dot((r + pe_ref[:, half:]).astype(BF16), w1_ref[half:, :])
        hid = _gelu_tanh(lo + pltpu.roll(hi, n_rows - 1, 0))
        o_ref[0] = _dot(hid.astype(BF16), w2_ref[...])

    one(rk_ref, pek_ref, w1k_ref, w2k_ref, ok_ref)
    one(rv_ref, pev_ref, w1v_ref, w2v_ref, ov_ref)


def _nsa_compress(rk, rv, pek, pev, w1k, w2k, w1v, w2v):
    bg, rows, width = rk.shape
    tok = pl.BlockSpec((1, rows, width), lambda i: (i, 0, 0))
    full = lambda a: pl.BlockSpec(a.shape, lambda i: (0,) * a.ndim)
    out = pl.BlockSpec((1, rows, HEAD_DIM), lambda i: (i, 0, 0))
    return pl.pallas_call(
        _compress_kernel,
        grid=(bg,),
        in_specs=[tok, tok, full(pek), full(pev), full(w1k), full(w2k), full(w1v), full(w2v)],
        out_specs=[out, out],
        out_shape=[jax.ShapeDtypeStruct((bg, rows, HEAD_DIM), F32)] * 2,
        compiler_params=_cparams(("parallel",)),
    )(rk, rv, pek, pev, w1k, w2k, w1v, w2v)


def _split3_nt_dot(b01, a):
    a1 = a.astype(BF16)
    r1 = a - a1.astype(F32)
    a2 = r1.astype(BF16)
    a3 = (r1 - a2.astype(F32)).astype(BF16)
    return _nt_dot(b01, a1) + _nt_dot(b01, a2) + _nt_dot(b01, a3)


def _nsa_kernel(q_ref, kc_ref, vc_ref, ks_ref, kw_ref, vs_ref, vw_ref, sm_ref, o_ref, oslc_ref, *, tq, seq, wlen):
    q0 = pl.program_id(1) * tq
    n_c = kc_ref.shape[2]
    n_s = seq // SEL_BLOCK
    rows = NSA_REP * tq
    tcol = q0 + lax.broadcasted_iota(I32, (tq, 1), 0)
    tcol_r = jnp.concatenate([tcol] * NSA_REP, axis=0)

    cidx = lax.broadcasted_iota(I32, (rows, n_c), 1)
    vis_r = (cidx * CMP_STRIDE + (CMP_LEN - 1)) <= tcol_r
    any_vis_r = (tcol_r >= (CMP_LEN - 1)).astype(F32)
    oc = lax.broadcasted_iota(I32, (n_s, n_c), 1) * CMP_STRIDE
    ob = lax.broadcasted_iota(I32, (n_s, n_c), 0) * SEL_BLOCK
    overlap_t = ((oc < ob + SEL_BLOCK) & (oc + CMP_LEN > ob)).astype(BF16)
    jj_t = lax.broadcasted_iota(I32, (n_s, tq), 0)
    blk_t = (q0 + lax.broadcasted_iota(I32, (1, tq), 1)) // SEL_BLOCK
    w0 = pl.multiple_of(jnp.maximum(q0 + tq - wlen, 0), 8)
    dist = tcol_r - (w0 + lax.broadcasted_iota(I32, (rows, wlen), 1))
    band_r = (dist >= 0) & (dist < WINDOW)
    gates = sm_ref[0]

    qs_all, qs_aug_all, o_cmp_all = [], [], []
    for g in range(NSA_KV_HEADS):
        qs = (jnp.concatenate(
            [q_ref[0, :, (g * NSA_REP + r) * HEAD_DIM:(g * NSA_REP + r + 1) * HEAD_DIM] for r in range(NSA_REP)],
            axis=0) * SCALE).astype(BF16)

        s_c = _nt_dot(qs, kc_ref[0, g].astype(BF16))
        p_c, l_c = _softmax_rows(jnp.where(vis_r, s_c, NEG))
        p_c = p_c / l_c * any_vis_r
        o_cmp_all.append(_dot(p_c.astype(BF16), vc_ref[0, g].astype(BF16)))

        p_sum = p_c[0:tq]
        for r in range(1, NSA_REP):
            p_sum = p_sum + p_c[r * tq:(r + 1) * tq]
        imp = _split3_nt_dot(overlap_t, p_sum)
        forced = (jj_t == 0) | (jj_t == blk_t) | (jj_t == blk_t - 1)
        imp = jnp.where(jj_t <= blk_t, jnp.where(forced, FORCE, imp), -1.0)
        rank = jnp.zeros((n_s, tq), F32)
        for j2 in range(n_s):
            row = imp[j2:j2 + 1, :]
            beats = (row > imp) | ((row == imp) & (j2 < jj_t))
            rank = rank + beats.astype(F32)
        sel = (rank < float(min(SEL_TOPN, n_s))) & (jj_t <= blk_t)
        bias_t = jnp.concatenate([jnp.where(sel, 0.0, NEG), jnp.zeros((tq - n_s, tq), F32)], axis=0)
        sel_bias = bias_t.T[:, :LANES - HEAD_DIM]
        qs_all.append(qs)
        qs_aug_all.append(jnp.concatenate([qs, jnp.concatenate([sel_bias] * NSA_REP, axis=0).astype(BF16)], axis=1))

    def slc_body(klen):
        block_onehot = (lax.broadcasted_iota(I32, (klen, LANES - HEAD_DIM), 0) // SEL_BLOCK
                        == lax.broadcasted_iota(I32, (klen, LANES - HEAD_DIM), 1)).astype(BF16)
        causal_r = lax.broadcasted_iota(I32, (rows, klen), 1) <= tcol_r
        for g in range(NSA_KV_HEADS):
            ksl = slice(g * HEAD_DIM, (g + 1) * HEAD_DIM)
            k_aug = jnp.concatenate([ks_ref[0, :klen, ksl].astype(BF16), block_onehot], axis=1)
            p_s, l_s = _softmax_rows(jnp.where(causal_r, _nt_dot(qs_aug_all[g], k_aug), NEG))
            oslc_ref[g] = _dot(p_s.astype(BF16), vs_ref[0, :klen, ksl].astype(BF16)) / l_s

    _for_key_extent(q0 + tq - 1, seq, slc_body)

    for g in range(NSA_KV_HEADS):
        ksl = slice(g * HEAD_DIM, (g + 1) * HEAD_DIM)
        o_cmp, o_slc = o_cmp_all[g], oslc_ref[g]

        kw = kw_ref[0, pl.ds(w0, wlen), ksl].astype(BF16)
        vw = vw_ref[0, pl.ds(w0, wlen), ksl].astype(BF16)
        p_w, l_w = _softmax_rows(jnp.where(band_r, _nt_dot(qs_all[g], kw), NEG))
        o_win = _dot(p_w.astype(BF16), vw) / l_w

        for r in range(NSA_REP):
            head = g * NSA_REP + r
            rs = slice(r * tq, (r + 1) * tq)

            def gate(branch, head=head):
                col = SMALL_G + branch * NSA_HEADS + head
                return 1.0 / (1.0 + jnp.exp(-gates[:, col:col + 1]))

            out = gate(0) * o_cmp[rs] + gate(1) * o_slc[rs] + gate(2) * o_win[rs]
            o_ref[0, :, head * HEAD_DIM:(head + 1) * HEAD_DIM] = out.astype(o_ref.dtype)


def _nsa_attention(hr3, hp3, k_cmp, v_cmp):
    b, seq, _ = hr3.shape
    tq = min(256, seq)
    wlen = min(WINDOW + tq, seq)
    qw = NSA_HEADS * HEAD_DIM
    kvw = NSA_KV_HEADS * HEAD_DIM
    n_c = k_cmp.shape[2]
    cmp_spec = pl.BlockSpec((1, NSA_KV_HEADS, n_c, HEAD_DIM), lambda i, j: (i, 0, 0, 0))

    def seq_spec(col):
        return pl.BlockSpec((1, seq, kvw), lambda i, j: (i, 0, col // kvw))

    return pl.pallas_call(
        functools.partial(_nsa_kernel, tq=tq, seq=seq, wlen=wlen),
        grid=(b, seq // tq),
        in_specs=[pl.BlockSpec((1, tq, qw), lambda i, j: (i, j, ROPE_POS['nsa_q'] // qw)),
                  cmp_spec, cmp_spec,
                  seq_spec(ROPE_POS['nsa_ks']), seq_spec(ROPE_POS['nsa_kw']),
                  seq_spec(PLAIN_POS['nsa_vs']), seq_spec(PLAIN_POS['nsa_vw']),
                  pl.BlockSpec((1, tq, LANES), lambda i, j: (i, j, SMALL_COL // LANES))],
        out_specs=pl.BlockSpec((1, tq, qw), lambda i, j: (i, j, 0)),
        out_shape=jax.ShapeDtypeStruct((b, seq, qw), BF16),
        scratch_shapes=[pltpu.VMEM((NSA_KV_HEADS, NSA_REP * tq, HEAD_DIM), F32)],
        compiler_params=_cparams(("parallel", "arbitrary")),
    )(hr3, k_cmp, v_cmp, hr3, hr3, hp3, hp3, hp3)


def _dsa_kernel(q_ref, kk_ref, v_ref, iq_ref, wt_ref, o_ref, key_ref, bias_ref, *, tq, seq, topk):
    q0 = pl.program_id(1) * tq
    trow = q0 + lax.broadcasted_iota(I32, (1, tq), 1)
    w_idx = wt_ref[0] * (IDX_HEADS ** -0.5) * (IDX_DIM ** -0.5)
    kf = float(topk)

    def body(klen):
        ik = kk_ref[0, :klen, HEAD_DIM:2 * HEAD_DIM].astype(BF16)
        score = jnp.zeros((klen, tq), F32)
        for h in range(IDX_HEADS):
            d = _nt_dot(ik, iq_ref[0, :, h * IDX_DIM:(h + 1) * IDX_DIM].astype(BF16))
            score = score + w_idx[h:h + 1, :] * jnp.maximum(d, 0.0)
        score = jnp.where(lax.broadcasted_iota(I32, (klen, tq), 0) <= trow, score, NEG)

        bits = lax.bitcast_convert_type(score, I32)
        key_ref[:klen, :] = jnp.where(bits < 0, bits ^ 0x7FFFFFFF, bits)

        def count(compare, bound):
            bound8 = jnp.broadcast_to(bound, (8, tq))
            accs = [jnp.zeros((8, tq), F32) for _ in range(4)]
            for j in range(klen // 8):
                accs[j % 4] = accs[j % 4] + compare(key_ref[j * 8:(j + 1) * 8, :], bound8).astype(F32)
            return jnp.sum((accs[0] + accs[1]) + (accs[2] + accs[3]), axis=0, keepdims=True)

        def tau_step(i, tau):
            cand = tau + jnp.left_shift(jnp.int32(1), 31 - i)
            return jnp.where(count(jnp.greater_equal, cand) >= kf, cand, tau)

        tau = lax.fori_loop(0, 32, tau_step, jnp.full((1, tq), -2 ** 31, I32))
        need = kf - count(jnp.greater, tau)

        tri = (lax.broadcasted_iota(I32, (LANES, LANES), 0) >= lax.broadcasted_iota(I32, (LANES, LANES), 1)).astype(BF16)
        seen = jnp.zeros((1, tq), F32)
        for blk in range(klen // LANES):
            ks = slice(blk * LANES, (blk + 1) * LANES)
            key = key_ref[ks, :]
            tie = key == tau
            running = _dot(tri, jnp.where(tie, 1.0, 0.0).astype(BF16)) + seen
            seen = running[LANES - 1:LANES, :]
            kpos = blk * LANES + lax.broadcasted_iota(I32, (LANES, tq), 0)
            chosen = ((key > tau) | (tie & (running <= need))) & (kpos <= trow)
            bias_ref[:, ks] = jnp.where(chosen, 0.0, NEG).T

        k = kk_ref[0, :klen, 0:HEAD_DIM].astype(BF16)
        v = v_ref[0, :klen, 0:HEAD_DIM].astype(BF16)
        bias = bias_ref[:, :klen]
        for h in range(DSA_HEADS):
            sl = slice(h * HEAD_DIM, (h + 1) * HEAD_DIM)
            s = _nt_dot((q_ref[0, :, sl] * SCALE).astype(BF16), k)
            p, l = _softmax_rows(s + bias)
            o_ref[0, :, sl] = (_dot(p.astype(BF16), v) / l).astype(o_ref.dtype)

    _for_key_extent(q0 + tq - 1, seq, body)


def _dsa_attention(hr3, hp3, iw_t):
    b, seq, _ = hr3.shape
    tq = min(256, seq)
    topk = min(DSA_TOPK, seq // 4)
    qw = DSA_HEADS * HEAD_DIM
    iqw = IDX_HEADS * IDX_DIM
    return pl.pallas_call(
        functools.partial(_dsa_kernel, tq=tq, seq=seq, topk=topk),
        grid=(b, seq // tq),
        in_specs=[pl.BlockSpec((1, tq, qw), lambda i, j: (i, j, ROPE_POS['dsa_q'] // qw)),
                  pl.BlockSpec((1, seq, LANES), lambda i, j: (i, 0, ROPE_POS['dsa_k'] // LANES)),
                  pl.BlockSpec((1, seq, LANES), lambda i, j: (i, 0, PLAIN_POS['dsa_v'] // LANES)),
                  pl.BlockSpec((1, tq, iqw), lambda i, j: (i, j, ROPE_POS['idx_q'] // iqw)),
                  pl.BlockSpec((1, IDX_HEADS, tq), lambda i, j: (i, 0, j))],
        out_specs=pl.BlockSpec((1, tq, qw), lambda i, j: (i, j, 0)),
        out_shape=jax.ShapeDtypeStruct((b, seq, qw), BF16),
        scratch_shapes=[pltpu.VMEM((seq, tq), I32), pltpu.VMEM((tq, seq), F32)],
        compiler_params=_cparams(("parallel", "arbitrary")),
    )(hr3, hr3, hp3, hr3, iw_t)


def _outproj_kernel(of_ref, on_ref, od_ref, x_ref, wf_ref, wn_ref, wd_ref, g_ref, b_ref, o_ref, *, alpha):
    mix = _dot(of_ref[...], wf_ref[...]) + _dot(on_ref[...], wn_ref[...]) + _dot(od_ref[...], wd_ref[...])
    o_ref[...] = _layer_norm(alpha * x_ref[...] + mix, g_ref[...], b_ref[...])


def _outproj_ln(o_fox, o_nsa, o_dsa, x2d, w_f, w_n, w_d, g, b, alpha):
    n = x2d.shape[0]
    tm = min(512, n)
    row = lambda a: pl.BlockSpec((tm, a.shape[1]), lambda i: (i, 0))
    full = lambda a: pl.BlockSpec(a.shape, lambda i: (0, 0))
    return pl.pallas_call(
        functools.partial(_outproj_kernel, alpha=alpha),
        grid=(n // tm,),
        in_specs=[row(o_fox), row(o_nsa), row(o_dsa), row(x2d), full(w_f), full(w_n), full(w_d), full(g), full(b)],
        out_specs=pl.BlockSpec((tm, D_MODEL), lambda i: (i, 0)),
        out_shape=jax.ShapeDtypeStruct((n, D_MODEL), F32),
        compiler_params=_cparams(("parallel",)),
    )(o_fox, o_nsa, o_dsa, x2d, w_f, w_n, w_d, g, b)


def _router_kernel(x_ref, wr_ref, br_ref, g_ref):
    tn = x_ref.shape[0]
    logits = _nt_dot(wr_ref[...], x_ref[...].astype(BF16))
    s = 1.0 / (1.0 + jnp.exp(-logits))
    sb = s + br_ref[...]
    low = jnp.float32(-3e38)

    grp = []
    for gi in range(N_GROUPS):
        blk = sb[gi * GROUP_SIZE:(gi + 1) * GROUP_SIZE]
        m1 = jnp.max(blk, axis=0, keepdims=True)
        is_max = blk == m1
        n_max = jnp.sum(is_max.astype(F32), axis=0, keepdims=True)
        m2 = jnp.max(jnp.where(is_max, low, blk), axis=0, keepdims=True)
        grp.append(m1 + jnp.where(n_max >= 2.0, m1, m2))
    masked = []
    for gi in range(N_GROUPS):
        rank = jnp.zeros((1, tn), F32)
        for g2 in range(N_GROUPS):
            if g2 == gi:
                continue
            beats = (grp[g2] > grp[gi]) | ((grp[g2] == grp[gi]) if g2 < gi else False)
            rank = rank + beats.astype(F32)
        keep = rank < float(TOPK_GROUPS)
        masked.append(jnp.where(keep, sb[gi * GROUP_SIZE:(gi + 1) * GROUP_SIZE], NEG))
    masked = jnp.concatenate(masked, axis=0)

    eidx = lax.broadcasted_iota(I32, (N_EXPERTS, tn), 0)
    rank = jnp.zeros((N_EXPERTS, tn), F32)
    for e2 in range(N_EXPERTS):
        row = masked[e2:e2 + 1]
        beats = (row > masked) | ((row == masked) & (e2 < eidx))
        rank = rank + beats.astype(F32)
    gw = jnp.where(rank < float(MOE_TOPK), s, 0.0)
    g_ref[...] = gw / jnp.sum(gw, axis=0, keepdims=True) * ROUTED_SCALE


def _router(x2d, wr_t, br_col):
    n = x2d.shape[0]
    tn = min(512, n)
    return pl.pallas_call(
        _router_kernel,
        grid=(n // tn,),
        in_specs=[pl.BlockSpec((tn, D_MODEL), lambda i: (i, 0)),
                  pl.BlockSpec(wr_t.shape, lambda i: (0, 0)),
                  pl.BlockSpec(br_col.shape, lambda i: (0, 0))],
        out_specs=pl.BlockSpec((N_EXPERTS, tn), lambda i: (0, i)),
        out_shape=jax.ShapeDtypeStruct((N_EXPERTS, n), F32),
        compiler_params=_cparams(("parallel",)),
    )(x2d, wr_t, br_col)


def _silu(x):
    return x / (1.0 + jnp.exp(-x))


def _moe_kernel(x_ref, gt_ref, wg_ref, wu_ref, wd_ref, sg_ref, su_ref, sd_ref, lg_ref, lb_ref, o_ref,
                xb_ref, acc_ref, *, alpha):
    e = pl.program_id(1)

    @pl.when(e == 0)
    def _():
        xb = x_ref[...].astype(BF16)
        xb_ref[...] = xb
        h = _silu(_dot(xb, sg_ref[...])) * _dot(xb, su_ref[...])
        acc_ref[...] = _dot(h.astype(BF16), sd_ref[...])

    xb = xb_ref[...]
    h = (_silu(_dot(xb, wg_ref[0, 0].astype(BF16))) * _dot(xb, wu_ref[0, 0].astype(BF16))).astype(BF16)
    gates = gt_ref[...]
    lane = lax.broadcasted_iota(I32, gates.shape, 1)
    gcol = jnp.sum(jnp.where(lane == e, gates, 0.0), axis=1, keepdims=True)
    chunk = 256
    for c in range(D_MODEL // chunk):
        cs = slice(c * chunk, (c + 1) * chunk)
        acc_ref[:, cs] += gcol * _dot(h, wd_ref[0, 0, :, cs].astype(BF16))

    @pl.when(e == pl.num_programs(1) - 1)
    def _():
        o_ref[...] = _layer_norm(alpha * x_ref[...] + acc_ref[...], lg_ref[...], lb_ref[...])


def _moe_ln(x2d, gates, layer, w_gate, w_up, w_down, ws_gate, ws_up, ws_down, g, b, alpha):
    n = x2d.shape[0]
    tn = min(2048, n)
    full = lambda a: pl.BlockSpec(a.shape, lambda i, e: (0,) * a.ndim)
    once = pl.Buffered(1)
    return pl.pallas_call(
        functools.partial(_moe_kernel, alpha=alpha),
        grid=(n // tn, N_EXPERTS),
        in_specs=[pl.BlockSpec((tn, D_MODEL), lambda i, e: (i, 0), pipeline_mode=once),
                  pl.BlockSpec((tn, N_EXPERTS), lambda i, e: (i, 0)),
                  pl.BlockSpec((1, 1, D_MODEL, EXPERT_DIM), lambda i, e: (layer, e, 0, 0)),
                  pl.BlockSpec((1, 1, D_MODEL, EXPERT_DIM), lambda i, e: (layer, e, 0, 0)),
                  pl.BlockSpec((1, 1, EXPERT_DIM, D_MODEL), lambda i, e: (layer, e, 0, 0)),
                  full(ws_gate), full(ws_up), full(ws_down), full(g), full(b)],
        out_specs=pl.BlockSpec((tn, D_MODEL), lambda i, e: (i, 0), pipeline_mode=once),
        out_shape=jax.ShapeDtypeStruct((n, D_MODEL), F32),
        scratch_shapes=[pltpu.VMEM((tn, D_MODEL), BF16), pltpu.VMEM((tn, D_MODEL), F32)],
        compiler_params=_cparams(("parallel", "arbitrary")),
    )(x2d, gates, w_gate, w_up, w_down, ws_gate, ws_up, ws_down, g, b)


def _gather_cols(w, order):
    parts = []
    for item in order:
        if isinstance(item, tuple):
            parts.append(jnp.zeros(w.shape[:-1] + (item[1],), w.dtype))
        else:
            off, width = SEG_OFF[item]
            parts.append(w[..., off:off + width])
    return jnp.concatenate(parts, axis=-1)


def _rotate_half_cols(w):
    lead = w.shape[:-1]
    w4 = w.reshape(lead + (-1, 2, HEAD_DIM // 2))
    return jnp.concatenate([-w4[..., 1:2, :], w4[..., 0:1, :]], axis=-2).reshape(w.shape)


def _rope_tables(seq, width):
    half = HEAD_DIM // 2
    inv_freq = ROPE_THETA ** (-jnp.arange(half, dtype=F32) * (2.0 / HEAD_DIM))
    ang = jnp.arange(seq).astype(F32)[:, None] * inv_freq[None, :]
    reps = width // half
    return jnp.tile(jnp.cos(ang), (1, reps)), jnp.tile(jnp.sin(ang), (1, reps))


def _token_rows(tok, b, seq):
    t4 = tok.reshape(b, seq, NSA_KV_HEADS, HEAD_DIM).transpose(0, 2, 1, 3)
    return t4.reshape(b * NSA_KV_HEADS, seq // CMP_STRIDE, CMP_STRIDE * HEAD_DIM)


def kernel(x, w_in, fox_forget_bias, cmp_pos_k, cmp_w1_k, cmp_w2_k, cmp_pos_v, cmp_w1_v, cmp_w2_v, w_out,
           ln1_g, ln1_b, w_router, b_router, w_gate, w_up, w_down, ws_gate, ws_up, ws_down, ln2_g, ln2_b):
    b, seq, dm = x.shape
    depth = w_in.shape[0]
    n = b * seq
    alpha = float((2 * depth) ** 0.25)

    w_plain = _gather_cols(w_in, PLAIN_ORDER).astype(BF16)
    w_rope = _gather_cols(w_in, ROPE_ORDER)
    w_rope_a = w_rope.astype(BF16)
    w_rope_b = _rotate_half_cols(w_rope).astype(BF16)
    cos_t, sin_t = _rope_tables(seq, PROJ_TILE_COLS)
    fw, nw = FOX_HEADS * HEAD_DIM, NSA_HEADS * HEAD_DIM
    w_out_b = w_out.astype(BF16)

    x2d = x.reshape(n, dm)
    for l in range(depth):
        hp, hr = _project(x2d, w_plain[l], w_rope_a[l], w_rope_b[l], cos_t, sin_t, seq)
        hp3 = hp.reshape(b, seq, PLAIN_W)
        hr3 = hr.reshape(b, seq, ROPE_W)

        f_logit = hp3[:, :, SMALL_COL + SMALL_F:SMALL_COL + SMALL_F + FOX_HEADS]
        c_rows = _fox_cumsum(f_logit.transpose(0, 2, 1), fox_forget_bias[l].reshape(FOX_HEADS, 1))
        o_fox = _fox_attention(hp3, c_rows.transpose(0, 2, 1), c_rows)

        kvw = NSA_KV_HEADS * HEAD_DIM
        rk = _token_rows(hr3[:, :, ROPE_POS['nsa_kc']:ROPE_POS['nsa_kc'] + kvw], b, seq)
        rv = _token_rows(hp3[:, :, PLAIN_POS['nsa_vc']:PLAIN_POS['nsa_vc'] + kvw], b, seq)
        k_cmp, v_cmp = _nsa_compress(
            rk, rv, cmp_pos_k[l].reshape(1, -1), cmp_pos_v[l].reshape(1, -1),
            cmp_w1_k[l].astype(BF16), cmp_w2_k[l].astype(BF16), cmp_w1_v[l].astype(BF16), cmp_w2_v[l].astype(BF16))
        n_rows = seq // CMP_STRIDE
        o_nsa = _nsa_attention(hr3, hp3, k_cmp.reshape(b, NSA_KV_HEADS, n_rows, HEAD_DIM),
                               v_cmp.reshape(b, NSA_KV_HEADS, n_rows, HEAD_DIM))

        iw_t = hp3[:, :, SMALL_COL + SMALL_W:SMALL_COL + SMALL_W + IDX_HEADS].transpose(0, 2, 1)
        o_dsa = _dsa_attention(hr3, hp3, iw_t)

        x2d = _outproj_ln(o_fox.reshape(n, fw), o_nsa.reshape(n, nw), o_dsa.reshape(n, nw), x2d,
                          w_out_b[l, :fw], w_out_b[l, fw:fw + nw], w_out_b[l, fw + nw:],
                          ln1_g[l].reshape(1, dm), ln1_b[l].reshape(1, dm), alpha)

        gates_t = _router(x2d, w_router[l].T.astype(BF16), b_router[l].reshape(N_EXPERTS, 1))
        x2d = _moe_ln(x2d, gates_t.T, l, w_gate, w_up, w_down,
                      ws_gate[l].astype(BF16), ws_up[l].astype(BF16), ws_down[l].astype(BF16),
                      ln2_g[l].reshape(1, dm), ln2_b[l].reshape(1, dm), alpha)
    return x2d.reshape(b, seq, dm)
```

```python
import functools

import numpy as np
import jax
import jax.numpy as jnp
from jax import lax
from jax.experimental import pallas as pl
from jax.experimental.pallas import tpu as pltpu

D_MODEL = 1024
HEAD_DIM = 64
FOX_HEADS = 4
NSA_HEADS = 6
NSA_KV_HEADS = 2
NSA_REP = NSA_HEADS // NSA_KV_HEADS
DSA_HEADS = 6
ROPE_THETA = 10000.0
CMP_LEN = 32
CMP_STRIDE = 16
CMP_HIDDEN = 2 * HEAD_DIM
SEL_BLOCK = 64
SEL_TOPN = 16
WINDOW = 512
IDX_HEADS = 4
IDX_DIM = 64
DSA_TOPK = 256
N_EXPERTS = 64
N_GROUPS = 8
GROUP_SIZE = N_EXPERTS // N_GROUPS
TOPK_GROUPS = 4
MOE_TOPK = 8
EXPERT_DIM = 256
SHARED_DIM = 256
ROUTED_SCALE = 2.5
LN_EPS = 1e-5
NEG = -1e30
FORCE = 1e6
SCALE = HEAD_DIM ** -0.5

F32 = jnp.float32
BF16 = jnp.bfloat16
I32 = jnp.int32

VMEM_LIMIT_BYTES = 52 * 1024 * 1024
LANES = 128
KEY_STEP = 512
PROJ_TILE_COLS = 512

SEGMENTS = (
    ('fox_q', FOX_HEADS * HEAD_DIM), ('fox_k', FOX_HEADS * HEAD_DIM),
    ('fox_v', FOX_HEADS * HEAD_DIM), ('fox_f', FOX_HEADS),
    ('nsa_q', NSA_HEADS * HEAD_DIM),
    ('nsa_kc', NSA_KV_HEADS * HEAD_DIM), ('nsa_vc', NSA_KV_HEADS * HEAD_DIM),
    ('nsa_ks', NSA_KV_HEADS * HEAD_DIM), ('nsa_vs', NSA_KV_HEADS * HEAD_DIM),
    ('nsa_kw', NSA_KV_HEADS * HEAD_DIM), ('nsa_vw', NSA_KV_HEADS * HEAD_DIM),
    ('nsa_g', 3 * NSA_HEADS),
    ('dsa_q', DSA_HEADS * HEAD_DIM), ('dsa_k', HEAD_DIM), ('dsa_v', HEAD_DIM),
    ('idx_q', IDX_HEADS * IDX_DIM), ('idx_k', IDX_DIM), ('idx_w', IDX_HEADS),
)
SEG_OFF = {}
_off = 0
for _name, _width in SEGMENTS:
    SEG_OFF[_name] = (_off, _width)
    _off += _width
N_IN = _off

PLAIN_ORDER = ('fox_q', 'fox_k', 'fox_v', 'nsa_vc', 'nsa_vs', 'nsa_vw', 'dsa_v', (None, 64),
               'fox_f', 'nsa_g', 'idx_w', (None, 2 * LANES - FOX_HEADS - 3 * NSA_HEADS - IDX_HEADS))
ROPE_ORDER = ('nsa_q', 'nsa_kc', 'nsa_ks', 'nsa_kw', 'dsa_q', 'dsa_k', 'idx_k', 'idx_q')


def _layout(order):
    pos, off = {}, 0
    for item in order:
        if isinstance(item, tuple):
            off += item[1]
        else:
            pos[item] = off
            off += SEG_OFF[item][1]
    return pos, off


PLAIN_POS, PLAIN_W = _layout(PLAIN_ORDER)
ROPE_POS, ROPE_W = _layout(ROPE_ORDER)
SMALL_COL = PLAIN_POS['fox_f']
SMALL_F = 0
SMALL_G = FOX_HEADS
SMALL_W = FOX_HEADS + 3 * NSA_HEADS


def _cparams(sem):
    return pltpu.CompilerParams(dimension_semantics=sem, vmem_limit_bytes=VMEM_LIMIT_BYTES)


def _nt_dot(a, b):
    return lax.dot_general(a, b, (((1,), (1,)), ((), ())), preferred_element_type=F32)


def _dot(a, b):
    return jnp.dot(a, b, preferred_element_type=F32)


def _softmax_rows(s):
    m = jnp.max(s, axis=-1, keepdims=True)
    p = jnp.exp(s - m)
    return p, jnp.sum(p, axis=-1, keepdims=True)


def _for_key_extent(q_last, seq, body, step=KEY_STEP):
    n = seq // step
    if n <= 1:
        body(seq)
        return
    c = q_last // step
    for i in range(n):
        pl.when(c == i)(functools.partial(body, (i + 1) * step))


def _layer_norm(z, g, b):
    mu = jnp.mean(z, axis=-1, keepdims=True)
    zc = z - mu
    var = jnp.mean(zc * zc, axis=-1, keepdims=True)
    return zc * lax.rsqrt(var + LN_EPS) * g + b


def _proj_kernel(x_ref, w_ref, o_ref, xb_ref):
    @pl.when(pl.program_id(1) == 0)
    def _():
        xb_ref[...] = x_ref[...].astype(BF16)

    o_ref[...] = _dot(xb_ref[...], w_ref[...])


def _proj_rope_kernel(x_ref, wa_ref, wb_ref, cos_ref, sin_ref, o_ref, xb_ref):
    @pl.when(pl.program_id(1) == 0)
    def _():
        xb_ref[...] = x_ref[...].astype(BF16)

    xb = xb_ref[...]
    o_ref[...] = _dot(xb, wa_ref[...]) * cos_ref[...] + _dot(xb, wb_ref[...]) * sin_ref[...]


def _project(x2d, w_plain, w_a, w_b, cos_t, sin_t, seq):
    n = x2d.shape[0]
    tm = min(1024, seq)
    tc = cos_t.shape[1]
    plain = pl.pallas_call(
        _proj_kernel,
        grid=(n // tm, PLAIN_W // tc),
        in_specs=[pl.BlockSpec((tm, D_MODEL), lambda i, j: (i, 0)),
                  pl.BlockSpec((D_MODEL, tc), lambda i, j: (0, j))],
        out_specs=pl.BlockSpec((tm, tc), lambda i, j: (i, j)),
        out_shape=jax.ShapeDtypeStruct((n, PLAIN_W), F32),
        scratch_shapes=[pltpu.VMEM((tm, D_MODEL), BF16)],
        compiler_params=_cparams(("parallel", "arbitrary")),
    )(x2d, w_plain)
    nt = seq // tm
    roped = pl.pallas_call(
        _proj_rope_kernel,
        grid=(n // tm, ROPE_W // tc),
        in_specs=[pl.BlockSpec((tm, D_MODEL), lambda i, j: (i, 0)),
                  pl.BlockSpec((D_MODEL, tc), lambda i, j: (0, j)),
                  pl.BlockSpec((D_MODEL, tc), lambda i, j: (0, j)),
                  pl.BlockSpec((tm, tc), lambda i, j: (i % nt, 0)),
                  pl.BlockSpec((tm, tc), lambda i, j: (i % nt, 0))],
        out_specs=pl.BlockSpec((tm, tc), lambda i, j: (i, j)),
        out_shape=jax.ShapeDtypeStruct((n, ROPE_W), F32),
        scratch_shapes=[pltpu.VMEM((tm, D_MODEL), BF16)],
        compiler_params=_cparams(("parallel", "arbitrary")),
    )(x2d, w_a, w_b, cos_t, sin_t)
    return plain, roped


def _fox_cum_kernel(f_ref, fb_ref, o_ref):
    x = f_ref[0] + fb_ref[...]
    c = jnp.minimum(x, 0.0) - jnp.log1p(jnp.exp(-jnp.abs(x)))
    seq = c.shape[-1]
    lane = lax.broadcasted_iota(I32, c.shape, 1)
    sh = 1
    while sh < seq:
        c = c + jnp.where(lane >= sh, pltpu.roll(c, sh, 1), 0.0)
        sh *= 2
    o_ref[0] = c


def _fox_cumsum(f_rows, f_bias):
    b, h, seq = f_rows.shape
    return pl.pallas_call(
        _fox_cum_kernel,
        grid=(b,),
        in_specs=[pl.BlockSpec((1, h, seq), lambda i: (i, 0, 0)),
                  pl.BlockSpec((h, 1), lambda i: (0, 0))],
        out_specs=pl.BlockSpec((1, h, seq), lambda i: (i, 0, 0)),
        out_shape=jax.ShapeDtypeStruct((b, h, seq), F32),
        compiler_params=_cparams(("parallel",)),
    )(f_rows, f_bias)


def _fox_kernel(q_ref, k_ref, v_ref, cc_ref, cr_ref, o_ref, *, tq, seq):
    q0 = pl.program_id(1) * tq

    def body(klen):
        qpos = q0 + lax.broadcasted_iota(I32, (tq, klen), 0)
        causal = lax.broadcasted_iota(I32, (tq, klen), 1) <= qpos
        for h in range(FOX_HEADS):
            sl = slice(h * HEAD_DIM, (h + 1) * HEAD_DIM)
            qh = (q_ref[0, :, sl] * SCALE).astype(BF16)
            kh = k_ref[0, :klen, sl].astype(BF16)
            vh = v_ref[0, :klen, sl].astype(BF16)
            s = _nt_dot(qh, kh) + (cc_ref[0, :, h:h + 1] - cr_ref[0, h:h + 1, :klen])
            p, l = _softmax_rows(jnp.where(causal, s, NEG))
            o_ref[0, :, sl] = (_dot(p.astype(BF16), vh) / l).astype(o_ref.dtype)

    _for_key_extent(q0 + tq - 1, seq, body, step=tq)


def _fox_attention(hp3, c_cols, c_rows):
    b, seq, _ = hp3.shape
    tq = min(256, seq)
    w = FOX_HEADS * HEAD_DIM
    return pl.pallas_call(
        functools.partial(_fox_kernel, tq=tq, seq=seq),
        grid=(b, seq // tq),
        in_specs=[pl.BlockSpec((1, tq, w), lambda i, j: (i, j, PLAIN_POS['fox_q'] // w)),
                  pl.BlockSpec((1, seq, w), lambda i, j: (i, 0, PLAIN_POS['fox_k'] // w)),
                  pl.BlockSpec((1, seq, w), lambda i, j: (i, 0, PLAIN_POS['fox_v'] // w)),
                  pl.BlockSpec((1, tq, FOX_HEADS), lambda i, j: (i, j, 0)),
                  pl.BlockSpec((1, FOX_HEADS, seq), lambda i, j: (i, 0, 0))],
        out_specs=pl.BlockSpec((1, tq, w), lambda i, j: (i, j, 0)),
        out_shape=jax.ShapeDtypeStruct((b, seq, w), BF16),
        compiler_params=_cparams(("parallel", "arbitrary")),
    )(hp3, hp3, hp3, c_cols, c_rows)


def _gelu_tanh(x):
    return 0.5 * x * (1.0 + jnp.tanh(np.float32(np.sqrt(2.0 / np.pi)) * (x + 0.044715 * (x * x * x))))


def _compress_kernel(rk_ref, rv_ref, pek_ref, pev_ref, w1k_ref, w2k_ref, w1v_ref, w2v_ref, ok_ref, ov_ref):
    half = CMP_STRIDE * HEAD_DIM

    def one(r_ref, pe_ref, w1_ref, w2_ref, o_ref):
        r = r_ref[0]
        n_rows = r.shape[0]
        lo = _dot((r + pe_ref[:, :half]).astype(BF16), w1_ref[:half, :])
        hi = _dot((r + pe_ref[:, half:]).astype(BF16), w1_ref[half:, :])
        hid = _gelu_tanh(lo + pltpu.roll(hi, n_rows - 1, 0))
        o_ref[0] = _dot(hid.astype(BF16), w2_ref[...])

    one(rk_ref, pek_ref, w1k_ref, w2k_ref, ok_ref)
    one(rv_ref, pev_ref, w1v_ref, w2v_ref, ov_ref)


def _nsa_compress(rk, rv, pek, pev, w1k, w2k, w1v, w2v):
    bg, rows, width = rk.shape
    tok = pl.BlockSpec((1, rows, width), lambda i: (i, 0, 0))
    full = lambda a: pl.BlockSpec(a.shape, lambda i: (0,) * a.ndim)
    out = pl.BlockSpec((1, rows, HEAD_DIM), lambda i: (i, 0, 0))
    return pl.pallas_call(
        _compress_kernel,
        grid=(bg,),
        in_specs=[tok, tok, full(pek), full(pev), full(w1k), full(w2k), full(w1v), full(w2v)],
        out_specs=[out, out],
        out_shape=[jax.ShapeDtypeStruct((bg, rows, HEAD_DIM), F32)] * 2,
        compiler_params=_cparams(("parallel",)),
    )(rk, rv, pek, pev, w1k, w2k, w1v, w2v)


def _split3_nt_dot(b01, a):
    a1 = a.astype(BF16)
    r1 = a - a1.astype(F32)
    a2 = r1.astype(BF16)
    a3 = (r1 - a2.astype(F32)).astype(BF16)
    return _nt_dot(b01, a1) + _nt_dot(b01, a2) + _nt_dot(b01, a3)


def _nsa_kernel(q_ref, kc_ref, vc_ref, ks_ref, kw_ref, vs_ref, vw_ref, sm_ref, o_ref, oslc_ref, *, tq, seq, wlen):
    q0 = pl.program_id(1) * tq
    n_c = kc_ref.shape[2]
    n_s = seq // SEL_BLOCK
    rows = NSA_REP * tq
    tcol = q0 + lax.broadcasted_iota(I32, (tq, 1), 0)
    tcol_r = jnp.concatenate([tcol] * NSA_REP, axis=0)

    cidx = lax.broadcasted_iota(I32, (rows, n_c), 1)
    vis_r = (cidx * CMP_STRIDE + (CMP_LEN - 1)) <= tcol_r
    any_vis_r = (tcol_r >= (CMP_LEN - 1)).astype(F32)
    oc = lax.broadcasted_iota(I32, (n_s, n_c), 1) * CMP_STRIDE
    ob = lax.broadcasted_iota(I32, (n_s, n_c), 0) * SEL_BLOCK
    overlap_t = ((oc < ob + SEL_BLOCK) & (oc + CMP_LEN > ob)).astype(BF16)
    jj_t = lax.broadcasted_iota(I32, (n_s, tq), 0)
    blk_t = (q0 + lax.broadcasted_iota(I32, (1, tq), 1)) // SEL_BLOCK
    w0 = pl.multiple_of(jnp.maximum(q0 + tq - wlen, 0), 8)
    dist = tcol_r - (w0 + lax.broadcasted_iota(I32, (rows, wlen), 1))
    band_r = (dist >= 0) & (dist < WINDOW)
    gates = sm_ref[0]

    qs_all, qs_aug_all, o_cmp_all = [], [], []
    for g in range(NSA_KV_HEADS):
        qs = (jnp.concatenate(
            [q_ref[0, :, (g * NSA_REP + r) * HEAD_DIM:(g * NSA_REP + r + 1) * HEAD_DIM] for r in range(NSA_REP)],
            axis=0) * SCALE).astype(BF16)

        s_c = _nt_dot(qs, kc_ref[0, g].astype(BF16))
        p_c, l_c = _softmax_rows(jnp.where(vis_r, s_c, NEG))
        p_c = p_c / l_c * any_vis_r
        o_cmp_all.append(_dot(p_c.astype(BF16), vc_ref[0, g].astype(BF16)))

        p_sum = p_c[0:tq]
        for r in range(1, NSA_REP):
            p_sum = p_sum + p_c[r * tq:(r + 1) * tq]
        imp = _split3_nt_dot(overlap_t, p_sum)
        forced = (jj_t == 0) | (jj_t == blk_t) | (jj_t == blk_t - 1)
        imp = jnp.where(jj_t <= blk_t, jnp.where(forced, FORCE, imp), -1.0)
        rank = jnp.zeros((n_s, tq), F32)
        for j2 in range(n_s):
            row = imp[j2:j2 + 1, :]
            beats = (row > imp) | ((row == imp) & (j2 < jj_t))
            rank = rank + beats.astype(F32)
        sel = (rank < float(min(SEL_TOPN, n_s))) & (jj_t <= blk_t)
        bias_t = jnp.concatenate([jnp.where(sel, 0.0, NEG), jnp.zeros((tq - n_s, tq), F32)], axis=0)
        sel_bias = bias_t.T[:, :LANES - HEAD_DIM]
        qs_all.append(qs)
        qs_aug_all.append(jnp.concatenate([qs, jnp.concatenate([sel_bias] * NSA_REP, axis=0).astype(BF16)], axis=1))

    def slc_body(klen):
        block_onehot = (lax.broadcasted_iota(I32, (klen, LANES - HEAD_DIM), 0) // SEL_BLOCK
                        == lax.broadcasted_iota(I32, (klen, LANES - HEAD_DIM), 1)).astype(BF16)
        causal_r = lax.broadcasted_iota(I32, (rows, klen), 1) <= tcol_r
        for g in range(NSA_KV_HEADS):
            ksl = slice(g * HEAD_DIM, (g + 1) * HEAD_DIM)
            k_aug = jnp.concatenate([ks_ref[0, :klen, ksl].astype(BF16), block_onehot], axis=1)
            p_s, l_s = _softmax_rows(jnp.where(causal_r, _nt_dot(qs_aug_all[g], k_aug), NEG))
            oslc_ref[g] = _dot(p_s.astype(BF16), vs_ref[0, :klen, ksl].astype(BF16)) / l_s

    _for_key_extent(q0 + tq - 1, seq, slc_body)

    for g in range(NSA_KV_HEADS):
        ksl = slice(g * HEAD_DIM, (g + 1) * HEAD_DIM)
        o_cmp, o_slc = o_cmp_all[g], oslc_ref[g]

        kw = kw_ref[0, pl.ds(w0, wlen), ksl].astype(BF16)
        vw = vw_ref[0, pl.ds(w0, wlen), ksl].astype(BF16)
        p_w, l_w = _softmax_rows(jnp.where(band_r, _nt_dot(qs_all[g], kw), NEG))
        o_win = _dot(p_w.astype(BF16), vw) / l_w

        for r in range(NSA_REP):
            head = g * NSA_REP + r
            rs = slice(r * tq, (r + 1) * tq)

            def gate(branch, head=head):
                col = SMALL_G + branch * NSA_HEADS + head
                return 1.0 / (1.0 + jnp.exp(-gates[:, col:col + 1]))

            out = gate(0) * o_cmp[rs] + gate(1) * o_slc[rs] + gate(2) * o_win[rs]
            o_ref[0, :, head * HEAD_DIM:(head + 1) * HEAD_DIM] = out.astype(o_ref.dtype)


def _nsa_attention(hr3, hp3, k_cmp, v_cmp):
    b, seq, _ = hr3.shape
    tq = min(256, seq)
    wlen = min(WINDOW + tq, seq)
    qw = NSA_HEADS * HEAD_DIM
    kvw = NSA_KV_HEADS * HEAD_DIM
    n_c = k_cmp.shape[2]
    cmp_spec = pl.BlockSpec((1, NSA_KV_HEADS, n_c, HEAD_DIM), lambda i, j: (i, 0, 0, 0))

    def seq_spec(col):
        return pl.BlockSpec((1, seq, kvw), lambda i, j: (i, 0, col // kvw))

    return pl.pallas_call(
        functools.partial(_nsa_kernel, tq=tq, seq=seq, wlen=wlen),
        grid=(b, seq // tq),
        in_specs=[pl.BlockSpec((1, tq, qw), lambda i, j: (i, j, ROPE_POS['nsa_q'] // qw)),
                  cmp_spec, cmp_spec,
                  seq_spec(ROPE_POS['nsa_ks']), seq_spec(ROPE_POS['nsa_kw']),
                  seq_spec(PLAIN_POS['nsa_vs']), seq_spec(PLAIN_POS['nsa_vw']),
                  pl.BlockSpec((1, tq, LANES), lambda i, j: (i, j, SMALL_COL // LANES))],
        out_specs=pl.BlockSpec((1, tq, qw), lambda i, j: (i, j, 0)),
        out_shape=jax.ShapeDtypeStruct((b, seq, qw), BF16),
        scratch_shapes=[pltpu.VMEM((NSA_KV_HEADS, NSA_REP * tq, HEAD_DIM), F32)],
        compiler_params=_cparams(("parallel", "arbitrary")),
    )(hr3, k_cmp, v_cmp, hr3, hr3, hp3, hp3, hp3)


def _dsa_kernel(q_ref, kk_ref, v_ref, iq_ref, wt_ref, o_ref, key_ref, bias_ref, *, tq, seq, topk):
    q0 = pl.program_id(1) * tq
    trow = q0 + lax.broadcasted_iota(I32, (1, tq), 1)
    w_idx = wt_ref[0] * (IDX_HEADS ** -0.5) * (IDX_DIM ** -0.5)
    kf = float(topk)

    def body(klen):
        ik = kk_ref[0, :klen, HEAD_DIM:2 * HEAD_DIM].astype(BF16)
        score = jnp.zeros((klen, tq), F32)
        for h in range(IDX_HEADS):
            d = _nt_dot(ik, iq_ref[0, :, h * IDX_DIM:(h + 1) * IDX_DIM].astype(BF16))
            score = score + w_idx[h:h + 1, :] * jnp.maximum(d, 0.0)
        score = jnp.where(lax.broadcasted_iota(I32, (klen, tq), 0) <= trow, score, NEG)

        bits = lax.bitcast_convert_type(score, I32)
        key_ref[:klen, :] = jnp.where(bits < 0, bits ^ 0x7FFFFFFF, bits)

        def count(compare, bound):
            bound8 = jnp.broadcast_to(bound, (8, tq))
            accs = [jnp.zeros((8, tq), F32) for _ in range(4)]
            for j in range(klen // 8):
                accs[j % 4] = accs[j % 4] + compare(key_ref[j * 8:(j + 1) * 8, :], bound8).astype(F32)
            return jnp.sum((accs[0] + accs[1]) + (accs[2] + accs[3]), axis=0, keepdims=True)

        def tau_step(i, tau):
            cand = tau + jnp.left_shift(jnp.int32(1), 31 - i)
            return jnp.where(count(jnp.greater_equal, cand) >= kf, cand, tau)

        tau = lax.fori_loop(0, 32, tau_step, jnp.full((1, tq), -2 ** 31, I32))
        need = kf - count(jnp.greater, tau)

        tri = (lax.broadcasted_iota(I32, (LANES, LANES), 0) >= lax.broadcasted_iota(I32, (LANES, LANES), 1)).astype(BF16)
        seen = jnp.zeros((1, tq), F32)
        for blk in range(klen // LANES):
            ks = slice(blk * LANES, (blk + 1) * LANES)
            key = key_ref[ks, :]
            tie = key == tau
            running = _dot(tri, jnp.where(tie, 1.0, 0.0).astype(BF16)) + seen
            seen = running[LANES - 1:LANES, :]
            kpos = blk * LANES + lax.broadcasted_iota(I32, (LANES, tq), 0)
            chosen = ((key > tau) | (tie & (running <= need))) & (kpos <= trow)
            bias_ref[:, ks] = jnp.where(chosen, 0.0, NEG).T

        k = kk_ref[0, :klen, 0:HEAD_DIM].astype(BF16)
        v = v_ref[0, :klen, 0:HEAD_DIM].astype(BF16)
        bias = bias_ref[:, :klen]
        for h in range(DSA_HEADS):
            sl = slice(h * HEAD_DIM, (h + 1) * HEAD_DIM)
            s = _nt_dot((q_ref[0, :, sl] * SCALE).astype(BF16), k)
            p, l = _softmax_rows(s + bias)
            o_ref[0, :, sl] = (_dot(p.astype(BF16), v) / l).astype(o_ref.dtype)

    _for_key_extent(q0 + tq - 1, seq, body)


def _dsa_attention(hr3, hp3, iw_t):
    b, seq, _ = hr3.shape
    tq = min(256, seq)
    topk = min(DSA_TOPK, seq // 4)
    qw = DSA_HEADS * HEAD_DIM
    iqw = IDX_HEADS * IDX_DIM
    return pl.pallas_call(
        functools.partial(_dsa_kernel, tq=tq, seq=seq, topk=topk),
        grid=(b, seq // tq),
        in_specs=[pl.BlockSpec((1, tq, qw), lambda i, j: (i, j, ROPE_POS['dsa_q'] // qw)),
                  pl.BlockSpec((1, seq, LANES), lambda i, j: (i, 0, ROPE_POS['dsa_k'] // LANES)),
                  pl.BlockSpec((1, seq, LANES), lambda i, j: (i, 0, PLAIN_POS['dsa_v'] // LANES)),
                  pl.BlockSpec((1, tq, iqw), lambda i, j: (i, j, ROPE_POS['idx_q'] // iqw)),
                  pl.BlockSpec((1, IDX_HEADS, tq), lambda i, j: (i, 0, j))],
        out_specs=pl.BlockSpec((1, tq, qw), lambda i, j: (i, j, 0)),
        out_shape=jax.ShapeDtypeStruct((b, seq, qw), BF16),
        scratch_shapes=[pltpu.VMEM((seq, tq), I32), pltpu.VMEM((tq, seq), F32)],
        compiler_params=_cparams(("parallel", "arbitrary")),
    )(hr3, hr3, hp3, hr3, iw_t)


def _outproj_kernel(of_ref, on_ref, od_ref, x_ref, wf_ref, wn_ref, wd_ref, g_ref, b_ref, o_ref, *, alpha):
    mix = _dot(of_ref[...], wf_ref[...]) + _dot(on_ref[...], wn_ref[...]) + _dot(od_ref[...], wd_ref[...])
    o_ref[...] = _layer_norm(alpha * x_ref[...] + mix, g_ref[...], b_ref[...])


def _outproj_ln(o_fox, o_nsa, o_dsa, x2d, w_f, w_n, w_d, g, b, alpha):
    n = x2d.shape[0]
    tm = min(512, n)
    row = lambda a: pl.BlockSpec((tm, a.shape[1]), lambda i: (i, 0))
    full = lambda a: pl.BlockSpec(a.shape, lambda i: (0, 0))
    return pl.pallas_call(
        functools.partial(_outproj_kernel, alpha=alpha),
        grid=(n // tm,),
        in_specs=[row(o_fox), row(o_nsa), row(o_dsa), row(x2d), full(w_f), full(w_n), full(w_d), full(g), full(b)],
        out_specs=pl.BlockSpec((tm, D_MODEL), lambda i: (i, 0)),
        out_shape=jax.ShapeDtypeStruct((n, D_MODEL), F32),
        compiler_params=_cparams(("parallel",)),
    )(o_fox, o_nsa, o_dsa, x2d, w_f, w_n, w_d, g, b)


def _router_kernel(x_ref, wr_ref, br_ref, g_ref):
    tn = x_ref.shape[0]
    logits = _nt_dot(wr_ref[...], x_ref[...].astype(BF16))
    s = 1.0 / (1.0 + jnp.exp(-logits))
    sb = s + br_ref[...]
    low = jnp.float32(-3e38)

    grp = []
    for gi in range(N_GROUPS):
        blk = sb[gi * GROUP_SIZE:(gi + 1) * GROUP_SIZE]
        m1 = jnp.max(blk, axis=0, keepdims=True)
        is_max = blk == m1
        n_max = jnp.sum(is_max.astype(F32), axis=0, keepdims=True)
        m2 = jnp.max(jnp.where(is_max, low, blk), axis=0, keepdims=True)
        grp.append(m1 + jnp.where(n_max >= 2.0, m1, m2))
    masked = []
    for gi in range(N_GROUPS):
        rank = jnp.zeros((1, tn), F32)
        for g2 in range(N_GROUPS):
            if g2 == gi:
                continue
            beats = (grp[g2] > grp[gi]) | ((grp[g2] == grp[gi]) if g2 < gi else False)
            rank = rank + beats.astype(F32)
        keep = rank < float(TOPK_GROUPS)
        masked.append(jnp.where(keep, sb[gi * GROUP_SIZE:(gi + 1) * GROUP_SIZE], NEG))
    masked = jnp.concatenate(masked, axis=0)

    eidx = lax.broadcasted_iota(I32, (N_EXPERTS, tn), 0).astype(F32)
    taken = jnp.zeros((N_EXPERTS, tn), F32)
    for _ in range(MOE_TOPK):
        top = jnp.max(masked, axis=0, keepdims=True)
        first = jnp.min(jnp.where(masked == top, eidx, float(N_EXPERTS)), axis=0, keepdims=True)
        hit = eidx == first
        taken = jnp.where(hit, 1.0, taken)
        masked = jnp.where(hit, low, masked)
    gw = s * taken
    g_ref[...] = gw / jnp.sum(gw, axis=0, keepdims=True) * ROUTED_SCALE


def _router(x2d, wr_t, br_col):
    n = x2d.shape[0]
    tn = min(512, n)
    return pl.pallas_call(
        _router_kernel,
        grid=(n // tn,),
        in_specs=[pl.BlockSpec((tn, D_MODEL), lambda i: (i, 0)),
                  pl.BlockSpec(wr_t.shape, lambda i: (0, 0)),
                  pl.BlockSpec(br_col.shape, lambda i: (0, 0))],
        out_specs=pl.BlockSpec((N_EXPERTS, tn), lambda i: (0, i)),
        out_shape=jax.ShapeDtypeStruct((N_EXPERTS, n), F32),
        compiler_params=_cparams(("parallel",)),
    )(x2d, wr_t, br_col)


def _silu(x):
    return x / (1.0 + jnp.exp(-x))


def _moe_kernel(x_ref, gt_ref, wg_ref, wu_ref, wd_ref, sg_ref, su_ref, sd_ref, lg_ref, lb_ref, o_ref,
                xb_ref, acc_ref, *, alpha):
    e = pl.program_id(1)

    @pl.when(e == 0)
    def _():
        xb = x_ref[...].astype(BF16)
        xb_ref[...] = xb
        h = _silu(_dot(xb, sg_ref[...])) * _dot(xb, su_ref[...])
        acc_ref[...] = _dot(h.astype(BF16), sd_ref[...])

    xb = xb_ref[...]
    h = (_silu(_dot(xb, wg_ref[0, 0].astype(BF16))) * _dot(xb, wu_ref[0, 0].astype(BF16))).astype(BF16)
    gates = gt_ref[...]
    lane = lax.broadcasted_iota(I32, gates.shape, 1)
    gcol = jnp.sum(jnp.where(lane == e, gates, 0.0), axis=1, keepdims=True)
    chunk = 256
    for c in range(D_MODEL // chunk):
        cs = slice(c * chunk, (c + 1) * chunk)
        acc_ref[:, cs] += gcol * _dot(h, wd_ref[0, 0, :, cs].astype(BF16))

    @pl.when(e == pl.num_programs(1) - 1)
    def _():
        o_ref[...] = _layer_norm(alpha * x_ref[...] + acc_ref[...], lg_ref[...], lb_ref[...])


def _moe_ln(x2d, gates, layer, w_gate, w_up, w_down, ws_gate, ws_up, ws_down, g, b, alpha):
    n = x2d.shape[0]
    tn = min(2048, n)
    full = lambda a: pl.BlockSpec(a.shape, lambda i, e: (0,) * a.ndim)
    once = pl.Buffered(1)
    return pl.pallas_call(
        functools.partial(_moe_kernel, alpha=alpha),
        grid=(n // tn, N_EXPERTS),
        in_specs=[pl.BlockSpec((tn, D_MODEL), lambda i, e: (i, 0), pipeline_mode=once),
                  pl.BlockSpec((tn, N_EXPERTS), lambda i, e: (i, 0)),
                  pl.BlockSpec((1, 1, D_MODEL, EXPERT_DIM), lambda i, e: (layer, e, 0, 0)),
                  pl.BlockSpec((1, 1, D_MODEL, EXPERT_DIM), lambda i, e: (layer, e, 0, 0)),
                  pl.BlockSpec((1, 1, EXPERT_DIM, D_MODEL), lambda i, e: (layer, e, 0, 0)),
                  full(ws_gate), full(ws_up), full(ws_down), full(g), full(b)],
        out_specs=pl.BlockSpec((tn, D_MODEL), lambda i, e: (i, 0), pipeline_mode=once),
        out_shape=jax.ShapeDtypeStruct((n, D_MODEL), F32),
        scratch_shapes=[pltpu.VMEM((tn, D_MODEL), BF16), pltpu.VMEM((tn, D_MODEL), F32)],
        compiler_params=_cparams(("parallel", "arbitrary")),
    )(x2d, gates, w_gate, w_up, w_down, ws_gate, ws_up, ws_down, g, b)


def _gather_cols(w, order):
    parts = []
    for item in order:
        if isinstance(item, tuple):
            parts.append(jnp.zeros(w.shape[:-1] + (item[1],), w.dtype))
        else:
            off, width = SEG_OFF[item]
            parts.append(w[..., off:off + width])
    return jnp.concatenate(parts, axis=-1)


def _rotate_half_cols(w):
    lead = w.shape[:-1]
    w4 = w.reshape(lead + (-1, 2, HEAD_DIM // 2))
    return jnp.concatenate([-w4[..., 1:2, :], w4[..., 0:1, :]], axis=-2).reshape(w.shape)


def _rope_tables(seq, width):
    half = HEAD_DIM // 2
    inv_freq = ROPE_THETA ** (-jnp.arange(half, dtype=F32) * (2.0 / HEAD_DIM))
    ang = jnp.arange(seq).astype(F32)[:, None] * inv_freq[None, :]
    reps = width // half
    return jnp.tile(jnp.cos(ang), (1, reps)), jnp.tile(jnp.sin(ang), (1, reps))


def _token_rows(tok, b, seq):
    t4 = tok.reshape(b, seq, NSA_KV_HEADS, HEAD_DIM).transpose(0, 2, 1, 3)
    return t4.reshape(b * NSA_KV_HEADS, seq // CMP_STRIDE, CMP_STRIDE * HEAD_DIM)


def kernel(x, w_in, fox_forget_bias, cmp_pos_k, cmp_w1_k, cmp_w2_k, cmp_pos_v, cmp_w1_v, cmp_w2_v, w_out,
           ln1_g, ln1_b, w_router, b_router, w_gate, w_up, w_down, ws_gate, ws_up, ws_down, ln2_g, ln2_b):
    b, seq, dm = x.shape
    depth = w_in.shape[0]
    n = b * seq
    alpha = float((2 * depth) ** 0.25)

    w_plain = _gather_cols(w_in, PLAIN_ORDER).astype(BF16)
    w_rope = _gather_cols(w_in, ROPE_ORDER)
    w_rope_a = w_rope.astype(BF16)
    w_rope_b = _rotate_half_cols(w_rope).astype(BF16)
    cos_t, sin_t = _rope_tables(seq, PROJ_TILE_COLS)
    fw, nw = FOX_HEADS * HEAD_DIM, NSA_HEADS * HEAD_DIM
    w_out_b = w_out.astype(BF16)

    x2d = x.reshape(n, dm)
    for l in range(depth):
        hp, hr = _project(x2d, w_plain[l], w_rope_a[l], w_rope_b[l], cos_t, sin_t, seq)
        hp3 = hp.reshape(b, seq, PLAIN_W)
        hr3 = hr.reshape(b, seq, ROPE_W)

        f_logit = hp3[:, :, SMALL_COL + SMALL_F:SMALL_COL + SMALL_F + FOX_HEADS]
        c_rows = _fox_cumsum(f_logit.transpose(0, 2, 1), fox_forget_bias[l].reshape(FOX_HEADS, 1))
        o_fox = _fox_attention(hp3, c_rows.transpose(0, 2, 1), c_rows)

        kvw = NSA_KV_HEADS * HEAD_DIM
        rk = _token_rows(hr3[:, :, ROPE_POS['nsa_kc']:ROPE_POS['nsa_kc'] + kvw], b, seq)
        rv = _token_rows(hp3[:, :, PLAIN_POS['nsa_vc']:PLAIN_POS['nsa_vc'] + kvw], b, seq)
        k_cmp, v_cmp = _nsa_compress(
            rk, rv, cmp_pos_k[l].reshape(1, -1), cmp_pos_v[l].reshape(1, -1),
            cmp_w1_k[l].astype(BF16), cmp_w2_k[l].astype(BF16), cmp_w1_v[l].astype(BF16), cmp_w2_v[l].astype(BF16))
        n_rows = seq // CMP_STRIDE
        o_nsa = _nsa_attention(hr3, hp3, k_cmp.reshape(b, NSA_KV_HEADS, n_rows, HEAD_DIM),
                               v_cmp.reshape(b, NSA_KV_HEADS, n_rows, HEAD_DIM))

        iw_t = hp3[:, :, SMALL_COL + SMALL_W:SMALL_COL + SMALL_W + IDX_HEADS].transpose(0, 2, 1)
        o_dsa = _dsa_attention(hr3, hp3, iw_t)

        x2d = _outproj_ln(o_fox.reshape(n, fw), o_nsa.reshape(n, nw), o_dsa.reshape(n, nw), x2d,
                          w_out_b[l, :fw], w_out_b[l, fw:fw + nw], w_out_b[l, fw + nw:],
                          ln1_g[l].reshape(1, dm), ln1_b[l].reshape(1, dm), alpha)

        gates_t = _router(x2d, w_router[l].T.astype(BF16), b_router[l].reshape(N_EXPERTS, 1))
        x2d = _moe_ln(x2d, gates_t.T, l, w_gate, w_up, w_down,
                      ws_gate[l].astype(BF16), ws_up[l].astype(BF16), ws_down[l].astype(BF16),
                      ln2_g[l].reshape(1, dm), ln2_b[l].reshape(1, dm), alpha)
    return x2d.reshape(b, seq, dm)
```

```python
import functools

import numpy as np
import jax
import jax.numpy as jnp
from jax import lax
from jax.experimental import pallas as pl
from jax.experimental.pallas import tpu as pltpu

D_MODEL = 1024
HEAD_DIM = 64
FOX_HEADS = 4
NSA_HEADS = 6
NSA_KV_HEADS = 2
NSA_REP = NSA_HEADS // NSA_KV_HEADS
DSA_HEADS = 6
ROPE_THETA = 10000.0
CMP_LEN = 32
CMP_STRIDE = 16
CMP_HIDDEN = 2 * HEAD_DIM
SEL_BLOCK = 64
SEL_TOPN = 16
WINDOW = 512
IDX_HEADS = 4
IDX_DIM = 64
DSA_TOPK = 256
N_EXPERTS = 64
N_GROUPS = 8
GROUP_SIZE = N_EXPERTS // N_GROUPS
TOPK_GROUPS = 4
MOE_TOPK = 8
EXPERT_DIM = 256
SHARED_DIM = 256
ROUTED_SCALE = 2.5
LN_EPS = 1e-5
NEG = -1e30
FORCE = 1e6
SCALE = HEAD_DIM ** -0.5

F32 = jnp.float32
BF16 = jnp.bfloat16
I32 = jnp.int32

VMEM_LIMIT_BYTES = 52 * 1024 * 1024
LANES = 128
KEY_STEP = 512
PROJ_TILE_COLS = 512

SEGMENTS = (
    ('fox_q', FOX_HEADS * HEAD_DIM), ('fox_k', FOX_HEADS * HEAD_DIM),
    ('fox_v', FOX_HEADS * HEAD_DIM), ('fox_f', FOX_HEADS),
    ('nsa_q', NSA_HEADS * HEAD_DIM),
    ('nsa_kc', NSA_KV_HEADS * HEAD_DIM), ('nsa_vc', NSA_KV_HEADS * HEAD_DIM),
    ('nsa_ks', NSA_KV_HEADS * HEAD_DIM), ('nsa_vs', NSA_KV_HEADS * HEAD_DIM),
    ('nsa_kw', NSA_KV_HEADS * HEAD_DIM), ('nsa_vw', NSA_KV_HEADS * HEAD_DIM),
    ('nsa_g', 3 * NSA_HEADS),
    ('dsa_q', DSA_HEADS * HEAD_DIM), ('dsa_k', HEAD_DIM), ('dsa_v', HEAD_DIM),
    ('idx_q', IDX_HEADS * IDX_DIM), ('idx_k', IDX_DIM), ('idx_w', IDX_HEADS),
)
SEG_OFF = {}
_off = 0
for _name, _width in SEGMENTS:
    SEG_OFF[_name] = (_off, _width)
    _off += _width
N_IN = _off

PLAIN_ORDER = ('fox_q', 'fox_k', 'fox_v', 'nsa_vc', 'nsa_vs', 'nsa_vw', 'dsa_v', (None, 64),
               'fox_f', 'nsa_g', 'idx_w', (None, 2 * LANES - FOX_HEADS - 3 * NSA_HEADS - IDX_HEADS))
ROPE_ORDER = ('nsa_q', 'nsa_kc', 'nsa_ks', 'nsa_kw', 'dsa_q', 'dsa_k', 'idx_k', 'idx_q')


def _layout(order):
    pos, off = {}, 0
    for item in order:
        if isinstance(item, tuple):
            off += item[1]
        else:
            pos[item] = off
            off += SEG_OFF[item][1]
    return pos, off


PLAIN_POS, PLAIN_W = _layout(PLAIN_ORDER)
ROPE_POS, ROPE_W = _layout(ROPE_ORDER)
SMALL_COL = PLAIN_POS['fox_f']
SMALL_F = 0
SMALL_G = FOX_HEADS
SMALL_W = FOX_HEADS + 3 * NSA_HEADS


def _cparams(sem):
    return pltpu.CompilerParams(dimension_semantics=sem, vmem_limit_bytes=VMEM_LIMIT_BYTES)


def _nt_dot(a, b):
    return lax.dot_general(a, b, (((1,), (1,)), ((), ())), preferred_element_type=F32)


def _dot(a, b):
    return jnp.dot(a, b, preferred_element_type=F32)


def _softmax_rows(s):
    m = jnp.max(s, axis=-1, keepdims=True)
    p = jnp.exp(s - m)
    return p, jnp.sum(p, axis=-1, keepdims=True)


def _for_key_extent(q_last, seq, body, step=KEY_STEP):
    n = seq // step
    if n <= 1:
        body(seq)
        return
    c = q_last // step
    for i in range(n):
        pl.when(c == i)(functools.partial(body, (i + 1) * step))


def _layer_norm(z, g, b):
    mu = jnp.mean(z, axis=-1, keepdims=True)
    zc = z - mu
    var = jnp.mean(zc * zc, axis=-1, keepdims=True)
    return zc * lax.rsqrt(var + LN_EPS) * g + b


def _proj_kernel(x_ref, w_ref, o_ref, xb_ref):
    @pl.when(pl.program_id(1) == 0)
    def _():
        xb_ref[...] = x_ref[...].astype(BF16)

    o_ref[...] = _dot(xb_ref[...], w_ref[...])


def _proj_rope_kernel(x_ref, wa_ref, wb_ref, cos_ref, sin_ref, o_ref, xb_ref):
    @pl.when(pl.program_id(1) == 0)
    def _():
        xb_ref[...] = x_ref[...].astype(BF16)

    xb = xb_ref[...]
    o_ref[...] = _dot(xb, wa_ref[...]) * cos_ref[...] + _dot(xb, wb_ref[...]) * sin_ref[...]


def _project(x2d, w_plain, w_a, w_b, cos_t, sin_t, seq):
    n = x2d.shape[0]
    tm = min(1024, seq)
    tc = cos_t.shape[1]
    plain = pl.pallas_call(
        _proj_kernel,
        grid=(n // tm, PLAIN_W // tc),
        in_specs=[pl.BlockSpec((tm, D_MODEL), lambda i, j: (i, 0)),
                  pl.BlockSpec((D_MODEL, tc), lambda i, j: (0, j))],
        out_specs=pl.BlockSpec((tm, tc), lambda i, j: (i, j)),
        out_shape=jax.ShapeDtypeStruct((n, PLAIN_W), F32),
        scratch_shapes=[pltpu.VMEM((tm, D_MODEL), BF16)],
        compiler_params=_cparams(("parallel", "arbitrary")),
    )(x2d, w_plain)
    nt = seq // tm
    roped = pl.pallas_call(
        _proj_rope_kernel,
        grid=(n // tm, ROPE_W // tc),
        in_specs=[pl.BlockSpec((tm, D_MODEL), lambda i, j: (i, 0)),
                  pl.BlockSpec((D_MODEL, tc), lambda i, j: (0, j)),
                  pl.BlockSpec((D_MODEL, tc), lambda i, j: (0, j)),
                  pl.BlockSpec((tm, tc), lambda i, j: (i % nt, 0)),
                  pl.BlockSpec((tm, tc), lambda i, j: (i % nt, 0))],
        out_specs=pl.BlockSpec((tm, tc), lambda i, j: (i, j)),
        out_shape=jax.ShapeDtypeStruct((n, ROPE_W), F32),
        scratch_shapes=[pltpu.VMEM((tm, D_MODEL), BF16)],
        compiler_params=_cparams(("parallel", "arbitrary")),
    )(x2d, w_a, w_b, cos_t, sin_t)
    return plain, roped


def _fox_cum_kernel(f_ref, fb_ref, o_ref):
    x = f_ref[0] + fb_ref[...]
    c = jnp.minimum(x, 0.0) - jnp.log1p(jnp.exp(-jnp.abs(x)))
    seq = c.shape[-1]
    lane = lax.broadcasted_iota(I32, c.shape, 1)
    sh = 1
    while sh < seq:
        c = c + jnp.where(lane >= sh, pltpu.roll(c, sh, 1), 0.0)
        sh *= 2
    o_ref[0] = c


def _fox_cumsum(f_rows, f_bias):
    b, h, seq = f_rows.shape
    return pl.pallas_call(
        _fox_cum_kernel,
        grid=(b,),
        in_specs=[pl.BlockSpec((1, h, seq), lambda i: (i, 0, 0)),
                  pl.BlockSpec((h, 1), lambda i: (0, 0))],
        out_specs=pl.BlockSpec((1, h, seq), lambda i: (i, 0, 0)),
        out_shape=jax.ShapeDtypeStruct((b, h, seq), F32),
        compiler_params=_cparams(("parallel",)),
    )(f_rows, f_bias)


def _fox_kernel(q_ref, k_ref, v_ref, cc_ref, cr_ref, o_ref, *, tq, seq):
    q0 = pl.program_id(1) * tq

    def body(klen):
        qpos = q0 + lax.broadcasted_iota(I32, (tq, klen), 0)
        causal = lax.broadcasted_iota(I32, (tq, klen), 1) <= qpos
        for h in range(FOX_HEADS):
            sl = slice(h * HEAD_DIM, (h + 1) * HEAD_DIM)
            qh = (q_ref[0, :, sl] * SCALE).astype(BF16)
            kh = k_ref[0, :klen, sl].astype(BF16)
            vh = v_ref[0, :klen, sl].astype(BF16)
            s = _nt_dot(qh, kh) + (cc_ref[0, :, h:h + 1] - cr_ref[0, h:h + 1, :klen])
            p, l = _softmax_rows(jnp.where(causal, s, NEG))
            o_ref[0, :, sl] = (_dot(p.astype(BF16), vh) / l).astype(o_ref.dtype)

    _for_key_extent(q0 + tq - 1, seq, body, step=tq)


def _fox_attention(hp3, c_cols, c_rows):
    b, seq, _ = hp3.shape
    tq = min(256, seq)
    w = FOX_HEADS * HEAD_DIM
    return pl.pallas_call(
        functools.partial(_fox_kernel, tq=tq, seq=seq),
        grid=(b, seq // tq),
        in_specs=[pl.BlockSpec((1, tq, w), lambda i, j: (i, j, PLAIN_POS['fox_q'] // w)),
                  pl.BlockSpec((1, seq, w), lambda i, j: (i, 0, PLAIN_POS['fox_k'] // w)),
                  pl.BlockSpec((1, seq, w), lambda i, j: (i, 0, PLAIN_POS['fox_v'] // w)),
                  pl.BlockSpec((1, tq, FOX_HEADS), lambda i, j: (i, j, 0)),
                  pl.BlockSpec((1, FOX_HEADS, seq), lambda i, j: (i, 0, 0))],
        out_specs=pl.BlockSpec((1, tq, w), lambda i, j: (i, j, 0)),
        out_shape=jax.ShapeDtypeStruct((b, seq, w), BF16),
        compiler_params=_cparams(("parallel", "arbitrary")),
    )(hp3, hp3, hp3, c_cols, c_rows)


def _gelu_tanh(x):
    return 0.5 * x * (1.0 + jnp.tanh(np.float32(np.sqrt(2.0 / np.pi)) * (x + 0.044715 * (x * x * x))))


def _compress_kernel(rk_ref, rv_ref, pek_ref, pev_ref, w1k_ref, w2k_ref, w1v_ref, w2v_ref, ok_ref, ov_ref):
    half = CMP_STRIDE * HEAD_DIM

    def one(r_ref, pe_ref, w1_ref, w2_ref, o_ref):
        r = r_ref[0]
        n_rows = r.shape[0]
        lo = _dot((r + pe_ref[:, :half]).astype(BF16), w1_ref[:half, :])
        hi = _dot((r + pe_ref[:, half:]).astype(BF16), w1_ref[half:, :])
        hid = _gelu_tanh(lo + pltpu.roll(hi, n_rows - 1, 0))
        o_ref[0] = _dot(hid.astype(BF16), w2_ref[...])

    one(rk_ref, pek_ref, w1k_ref, w2k_ref, ok_ref)
    one(rv_ref, pev_ref, w1v_ref, w2v_ref, ov_ref)


def _nsa_compress(rk, rv, pek, pev, w1k, w2k, w1v, w2v):
    bg, rows, width = rk.shape
    tok = pl.BlockSpec((1, rows, width), lambda i: (i, 0, 0))
    full = lambda a: pl.BlockSpec(a.shape, lambda i: (0,) * a.ndim)
    out = pl.BlockSpec((1, rows, HEAD_DIM), lambda i: (i, 0, 0))
    return pl.pallas_call(
        _compress_kernel,
        grid=(bg,),
        in_specs=[tok, tok, full(pek), full(pev), full(w1k), full(w2k), full(w1v), full(w2v)],
        out_specs=[out, out],
        out_shape=[jax.ShapeDtypeStruct((bg, rows, HEAD_DIM), F32)] * 2,
        compiler_params=_cparams(("parallel",)),
    )(rk, rv, pek, pev, w1k, w2k, w1v, w2v)


def _split3_nt_dot(b01, a):
    a1 = a.astype(BF16)
    r1 = a - a1.astype(F32)
    a2 = r1.astype(BF16)
    a3 = (r1 - a2.astype(F32)).astype(BF16)
    return _nt_dot(b01, a1) + _nt_dot(b01, a2) + _nt_dot(b01, a3)


def _nsa_kernel(q_ref, kc_ref, vc_ref, ks_ref, kw_ref, vs_ref, vw_ref, sm_ref, o_ref, oslc_ref, *, tq, seq, wlen):
    q0 = pl.program_id(1) * tq
    n_c = kc_ref.shape[2]
    n_s = seq // SEL_BLOCK
    rows = NSA_REP * tq
    tcol = q0 + lax.broadcasted_iota(I32, (tq, 1), 0)
    tcol_r = jnp.concatenate([tcol] * NSA_REP, axis=0)

    cidx = lax.broadcasted_iota(I32, (rows, n_c), 1)
    vis_r = (cidx * CMP_STRIDE + (CMP_LEN - 1)) <= tcol_r
    any_vis_r = (tcol_r >= (CMP_LEN - 1)).astype(F32)
    oc = lax.broadcasted_iota(I32, (n_s, n_c), 1) * CMP_STRIDE
    ob = lax.broadcasted_iota(I32, (n_s, n_c), 0) * SEL_BLOCK
    overlap_t = ((oc < ob + SEL_BLOCK) & (oc + CMP_LEN > ob)).astype(BF16)
    jj_t = lax.broadcasted_iota(I32, (n_s, tq), 0)
    blk_t = (q0 + lax.broadcasted_iota(I32, (1, tq), 1)) // SEL_BLOCK
    w0 = pl.multiple_of(jnp.maximum(q0 + tq - wlen, 0), 8)
    dist = tcol_r - (w0 + lax.broadcasted_iota(I32, (rows, wlen), 1))
    band_r = (dist >= 0) & (dist < WINDOW)
    gates = sm_ref[0]

    qs_all, qs_aug_all, o_cmp_all = [], [], []
    for g in range(NSA_KV_HEADS):
        qs = (jnp.concatenate(
            [q_ref[0, :, (g * NSA_REP + r) * HEAD_DIM:(g * NSA_REP + r + 1) * HEAD_DIM] for r in range(NSA_REP)],
            axis=0) * SCALE).astype(BF16)

        s_c = _nt_dot(qs, kc_ref[0, g].astype(BF16))
        p_c, l_c = _softmax_rows(jnp.where(vis_r, s_c, NEG))
        p_c = p_c / l_c * any_vis_r
        o_cmp_all.append(_dot(p_c.astype(BF16), vc_ref[0, g].astype(BF16)))

        p_sum = p_c[0:tq]
        for r in range(1, NSA_REP):
            p_sum = p_sum + p_c[r * tq:(r + 1) * tq]
        imp = _split3_nt_dot(overlap_t, p_sum)
        forced = (jj_t == 0) | (jj_t == blk_t) | (jj_t == blk_t - 1)
        imp = jnp.where(jj_t <= blk_t, jnp.where(forced, FORCE, imp), -1.0)
        rank = jnp.zeros((n_s, tq), F32)
        for j2 in range(n_s):
            row = imp[j2:j2 + 1, :]
            beats = (row > imp) | ((row == imp) & (j2 < jj_t))
            rank = rank + beats.astype(F32)
        sel = (rank < float(min(SEL_TOPN, n_s))) & (jj_t <= blk_t)
        bias_t = jnp.concatenate([jnp.where(sel, 0.0, NEG), jnp.zeros((tq - n_s, tq), F32)], axis=0)
        sel_bias = bias_t.T[:, :LANES - HEAD_DIM]
        qs_all.append(qs)
        qs_aug_all.append(jnp.concatenate([qs, jnp.concatenate([sel_bias] * NSA_REP, axis=0).astype(BF16)], axis=1))

    def slc_body(klen):
        block_onehot = (lax.broadcasted_iota(I32, (klen, LANES - HEAD_DIM), 0) // SEL_BLOCK
                        == lax.broadcasted_iota(I32, (klen, LANES - HEAD_DIM), 1)).astype(BF16)
        causal_r = lax.broadcasted_iota(I32, (rows, klen), 1) <= tcol_r
        for g in range(NSA_KV_HEADS):
            ksl = slice(g * HEAD_DIM, (g + 1) * HEAD_DIM)
            k_aug = jnp.concatenate([ks_ref[0, :klen, ksl].astype(BF16), block_onehot], axis=1)
            p_s, l_s = _softmax_rows(jnp.where(causal_r, _nt_dot(qs_aug_all[g], k_aug), NEG))
            oslc_ref[g] = _dot(p_s.astype(BF16), vs_ref[0, :klen, ksl].astype(BF16)) / l_s

    _for_key_extent(q0 + tq - 1, seq, slc_body)

    for g in range(NSA_KV_HEADS):
        ksl = slice(g * HEAD_DIM, (g + 1) * HEAD_DIM)
        o_cmp, o_slc = o_cmp_all[g], oslc_ref[g]

        kw = kw_ref[0, pl.ds(w0, wlen), ksl].astype(BF16)
        vw = vw_ref[0, pl.ds(w0, wlen), ksl].astype(BF16)
        p_w, l_w = _softmax_rows(jnp.where(band_r, _nt_dot(qs_all[g], kw), NEG))
        o_win = _dot(p_w.astype(BF16), vw) / l_w

        for r in range(NSA_REP):
            head = g * NSA_REP + r
            rs = slice(r * tq, (r + 1) * tq)

            def gate(branch, head=head):
                col = SMALL_G + branch * NSA_HEADS + head
                return 1.0 / (1.0 + jnp.exp(-gates[:, col:col + 1]))

            out = gate(0) * o_cmp[rs] + gate(1) * o_slc[rs] + gate(2) * o_win[rs]
            o_ref[0, :, head * HEAD_DIM:(head + 1) * HEAD_DIM] = out.astype(o_ref.dtype)


def _nsa_attention(hr3, hp3, k_cmp, v_cmp):
    b, seq, _ = hr3.shape
    tq = min(256, seq)
    wlen = min(WINDOW + tq, seq)
    qw = NSA_HEADS * HEAD_DIM
    kvw = NSA_KV_HEADS * HEAD_DIM
    n_c = k_cmp.shape[2]
    cmp_spec = pl.BlockSpec((1, NSA_KV_HEADS, n_c, HEAD_DIM), lambda i, j: (i, 0, 0, 0))

    def seq_spec(col):
        return pl.BlockSpec((1, seq, kvw), lambda i, j: (i, 0, col // kvw))

    return pl.pallas_call(
        functools.partial(_nsa_kernel, tq=tq, seq=seq, wlen=wlen),
        grid=(b, seq // tq),
        in_specs=[pl.BlockSpec((1, tq, qw), lambda i, j: (i, j, ROPE_POS['nsa_q'] // qw)),
                  cmp_spec, cmp_spec,
                  seq_spec(ROPE_POS['nsa_ks']), seq_spec(ROPE_POS['nsa_kw']),
                  seq_spec(PLAIN_POS['nsa_vs']), seq_spec(PLAIN_POS['nsa_vw']),
                  pl.BlockSpec((1, tq, LANES), lambda i, j: (i, j, SMALL_COL // LANES))],
        out_specs=pl.BlockSpec((1, tq, qw), lambda i, j: (i, j, 0)),
        out_shape=jax.ShapeDtypeStruct((b, seq, qw), BF16),
        scratch_shapes=[pltpu.VMEM((NSA_KV_HEADS, NSA_REP * tq, HEAD_DIM), F32)],
        compiler_params=_cparams(("parallel", "arbitrary")),
    )(hr3, k_cmp, v_cmp, hr3, hr3, hp3, hp3, hp3)


def _dsa_kernel(q_ref, kk_ref, v_ref, iq_ref, sm_ref, o_ref, key_ref, bias_ref, *, tq, seq, topk):
    q0 = pl.program_id(1) * tq
    trow = q0 + lax.broadcasted_iota(I32, (1, tq), 1)
    w_idx = sm_ref[0].T[SMALL_W:SMALL_W + IDX_HEADS, :] * (IDX_HEADS ** -0.5) * (IDX_DIM ** -0.5)
    kf = float(topk)

    def body(klen):
        ik = kk_ref[0, :klen, HEAD_DIM:2 * HEAD_DIM].astype(BF16)
        score = jnp.zeros((klen, tq), F32)
        for h in range(IDX_HEADS):
            d = _nt_dot(ik, iq_ref[0, :, h * IDX_DIM:(h + 1) * IDX_DIM].astype(BF16))
            score = score + w_idx[h:h + 1, :] * jnp.maximum(d, 0.0)
        score = jnp.where(lax.broadcasted_iota(I32, (klen, tq), 0) <= trow, score, NEG)

        bits = lax.bitcast_convert_type(score, I32)
        key_ref[:klen, :] = jnp.where(bits < 0, bits ^ 0x7FFFFFFF, bits)

        def count(compare, bound):
            bound8 = jnp.broadcast_to(bound, (8, tq))
            accs = [jnp.zeros((8, tq), F32) for _ in range(4)]
            for j in range(klen // 8):
                accs[j % 4] = accs[j % 4] + compare(key_ref[j * 8:(j + 1) * 8, :], bound8).astype(F32)
            return jnp.sum((accs[0] + accs[1]) + (accs[2] + accs[3]), axis=0, keepdims=True)

        def tau_step(i, tau):
            cand = tau + jnp.left_shift(jnp.int32(1), 31 - i)
            return jnp.where(count(jnp.greater_equal, cand) >= kf, cand, tau)

        tau = lax.fori_loop(0, 32, tau_step, jnp.full((1, tq), -2 ** 31, I32))
        need = kf - count(jnp.greater, tau)

        tri = (lax.broadcasted_iota(I32, (LANES, LANES), 0) >= lax.broadcasted_iota(I32, (LANES, LANES), 1)).astype(BF16)
        seen = jnp.zeros((1, tq), F32)
        for blk in range(klen // LANES):
            ks = slice(blk * LANES, (blk + 1) * LANES)
            key = key_ref[ks, :]
            tie = key == tau
            running = _dot(tri, jnp.where(tie, 1.0, 0.0).astype(BF16)) + seen
            seen = running[LANES - 1:LANES, :]
            kpos = blk * LANES + lax.broadcasted_iota(I32, (LANES, tq), 0)
            chosen = ((key > tau) | (tie & (running <= need))) & (kpos <= trow)
            bias_ref[:, ks] = jnp.where(chosen, 0.0, NEG).T

        k = kk_ref[0, :klen, 0:HEAD_DIM].astype(BF16)
        v = v_ref[0, :klen, 0:HEAD_DIM].astype(BF16)
        bias = bias_ref[:, :klen]
        for h in range(DSA_HEADS):
            sl = slice(h * HEAD_DIM, (h + 1) * HEAD_DIM)
            s = _nt_dot((q_ref[0, :, sl] * SCALE).astype(BF16), k)
            p, l = _softmax_rows(s + bias)
            o_ref[0, :, sl] = (_dot(p.astype(BF16), v) / l).astype(o_ref.dtype)

    _for_key_extent(q0 + tq - 1, seq, body)


def _dsa_attention(hr3, hp3):
    b, seq, _ = hr3.shape
    tq = min(256, seq)
    topk = min(DSA_TOPK, seq // 4)
    qw = DSA_HEADS * HEAD_DIM
    iqw = IDX_HEADS * IDX_DIM
    return pl.pallas_call(
        functools.partial(_dsa_kernel, tq=tq, seq=seq, topk=topk),
        grid=(b, seq // tq),
        in_specs=[pl.BlockSpec((1, tq, qw), lambda i, j: (i, j, ROPE_POS['dsa_q'] // qw)),
                  pl.BlockSpec((1, seq, LANES), lambda i, j: (i, 0, ROPE_POS['dsa_k'] // LANES)),
                  pl.BlockSpec((1, seq, LANES), lambda i, j: (i, 0, PLAIN_POS['dsa_v'] // LANES)),
                  pl.BlockSpec((1, tq, iqw), lambda i, j: (i, j, ROPE_POS['idx_q'] // iqw)),
                  pl.BlockSpec((1, tq, LANES), lambda i, j: (i, j, SMALL_COL // LANES))],
        out_specs=pl.BlockSpec((1, tq, qw), lambda i, j: (i, j, 0)),
        out_shape=jax.ShapeDtypeStruct((b, seq, qw), BF16),
        scratch_shapes=[pltpu.VMEM((seq, tq), I32), pltpu.VMEM((tq, seq), F32)],
        compiler_params=_cparams(("parallel", "arbitrary")),
    )(hr3, hr3, hp3, hr3, hp3)


def _outproj_kernel(of_ref, on_ref, od_ref, x_ref, wf_ref, wn_ref, wd_ref, g_ref, b_ref, o_ref, *, alpha):
    mix = _dot(of_ref[...], wf_ref[...]) + _dot(on_ref[...], wn_ref[...]) + _dot(od_ref[...], wd_ref[...])
    o_ref[...] = _layer_norm(alpha * x_ref[...] + mix, g_ref[...], b_ref[...])


def _outproj_ln(o_fox, o_nsa, o_dsa, x2d, w_f, w_n, w_d, g, b, alpha):
    n = x2d.shape[0]
    tm = min(512, n)
    row = lambda a: pl.BlockSpec((tm, a.shape[1]), lambda i: (i, 0))
    full = lambda a: pl.BlockSpec(a.shape, lambda i: (0, 0))
    return pl.pallas_call(
        functools.partial(_outproj_kernel, alpha=alpha),
        grid=(n // tm,),
        in_specs=[row(o_fox), row(o_nsa), row(o_dsa), row(x2d), full(w_f), full(w_n), full(w_d), full(g), full(b)],
        out_specs=pl.BlockSpec((tm, D_MODEL), lambda i: (i, 0)),
        out_shape=jax.ShapeDtypeStruct((n, D_MODEL), F32),
        compiler_params=_cparams(("parallel",)),
    )(o_fox, o_nsa, o_dsa, x2d, w_f, w_n, w_d, g, b)


def _router_kernel(x_ref, wr_ref, br_ref, g_ref):
    tn = x_ref.shape[0]
    logits = _nt_dot(wr_ref[...], x_ref[...].astype(BF16))
    s = 1.0 / (1.0 + jnp.exp(-logits))
    sb = s + br_ref[...]
    low = jnp.float32(-3e38)

    grp = []
    for gi in range(N_GROUPS):
        blk = sb[gi * GROUP_SIZE:(gi + 1) * GROUP_SIZE]
        m1 = jnp.max(blk, axis=0, keepdims=True)
        is_max = blk == m1
        n_max = jnp.sum(is_max.astype(F32), axis=0, keepdims=True)
        m2 = jnp.max(jnp.where(is_max, low, blk), axis=0, keepdims=True)
        grp.append(m1 + jnp.where(n_max >= 2.0, m1, m2))
    masked = []
    for gi in range(N_GROUPS):
        rank = jnp.zeros((1, tn), F32)
        for g2 in range(N_GROUPS):
            if g2 == gi:
                continue
            beats = (grp[g2] > grp[gi]) | ((grp[g2] == grp[gi]) if g2 < gi else False)
            rank = rank + beats.astype(F32)
        keep = rank < float(TOPK_GROUPS)
        masked.append(jnp.where(keep, sb[gi * GROUP_SIZE:(gi + 1) * GROUP_SIZE], NEG))
    masked = jnp.concatenate(masked, axis=0)

    eidx = lax.broadcasted_iota(I32, (N_EXPERTS, tn), 0).astype(F32)
    taken = jnp.zeros((N_EXPERTS, tn), F32)
    for _ in range(MOE_TOPK):
        top = jnp.max(masked, axis=0, keepdims=True)
        first = jnp.min(jnp.where(masked == top, eidx, float(N_EXPERTS)), axis=0, keepdims=True)
        hit = eidx == first
        taken = jnp.where(hit, 1.0, taken)
        masked = jnp.where(hit, low, masked)
    gw = s * taken
    g_ref[...] = gw / jnp.sum(gw, axis=0, keepdims=True) * ROUTED_SCALE


def _router(x2d, wr_t, br_col):
    n = x2d.shape[0]
    tn = min(512, n)
    return pl.pallas_call(
        _router_kernel,
        grid=(n // tn,),
        in_specs=[pl.BlockSpec((tn, D_MODEL), lambda i: (i, 0)),
                  pl.BlockSpec(wr_t.shape, lambda i: (0, 0)),
                  pl.BlockSpec(br_col.shape, lambda i: (0, 0))],
        out_specs=pl.BlockSpec((N_EXPERTS, tn), lambda i: (0, i)),
        out_shape=jax.ShapeDtypeStruct((N_EXPERTS, n), F32),
        compiler_params=_cparams(("parallel",)),
    )(x2d, wr_t, br_col)


def _silu(x):
    return x / (1.0 + jnp.exp(-x))


def _moe_kernel(x_ref, gt_ref, wg_ref, wu_ref, wd_ref, sg_ref, su_ref, sd_ref, lg_ref, lb_ref, o_ref,
                xb_ref, acc_ref, *, alpha):
    e = pl.program_id(1)

    @pl.when(e == 0)
    def _():
        xb = x_ref[...].astype(BF16)
        xb_ref[...] = xb
        h = _silu(_dot(xb, sg_ref[...])) * _dot(xb, su_ref[...])
        acc_ref[...] = _dot(h.astype(BF16), sd_ref[...])

    xb = xb_ref[...]
    h = (_silu(_dot(xb, wg_ref[0, 0].astype(BF16))) * _dot(xb, wu_ref[0, 0].astype(BF16))).astype(BF16)
    gates = gt_ref[...]
    lane = lax.broadcasted_iota(I32, gates.shape, 1)
    gcol = jnp.sum(jnp.where(lane == e, gates, 0.0), axis=1, keepdims=True)
    chunk = 256
    for c in range(D_MODEL // chunk):
        cs = slice(c * chunk, (c + 1) * chunk)
        acc_ref[:, cs] += gcol * _dot(h, wd_ref[0, 0, :, cs].astype(BF16))

    @pl.when(e == pl.num_programs(1) - 1)
    def _():
        o_ref[...] = _layer_norm(alpha * x_ref[...] + acc_ref[...], lg_ref[...], lb_ref[...])


def _moe_ln(x2d, gates, layer, w_gate, w_up, w_down, ws_gate, ws_up, ws_down, g, b, alpha):
    n = x2d.shape[0]
    tn = min(2048, n)
    full = lambda a: pl.BlockSpec(a.shape, lambda i, e: (0,) * a.ndim)
    once = pl.Buffered(1)
    return pl.pallas_call(
        functools.partial(_moe_kernel, alpha=alpha),
        grid=(n // tn, N_EXPERTS),
        in_specs=[pl.BlockSpec((tn, D_MODEL), lambda i, e: (i, 0), pipeline_mode=once),
                  pl.BlockSpec((tn, N_EXPERTS), lambda i, e: (i, 0)),
                  pl.BlockSpec((1, 1, D_MODEL, EXPERT_DIM), lambda i, e: (layer, e, 0, 0)),
                  pl.BlockSpec((1, 1, D_MODEL, EXPERT_DIM), lambda i, e: (layer, e, 0, 0)),
                  pl.BlockSpec((1, 1, EXPERT_DIM, D_MODEL), lambda i, e: (layer, e, 0, 0)),
                  full(ws_gate), full(ws_up), full(ws_down), full(g), full(b)],
        out_specs=pl.BlockSpec((tn, D_MODEL), lambda i, e: (i, 0), pipeline_mode=once),
        out_shape=jax.ShapeDtypeStruct((n, D_MODEL), F32),
        scratch_shapes=[pltpu.VMEM((tn, D_MODEL), BF16), pltpu.VMEM((tn, D_MODEL), F32)],
        compiler_params=_cparams(("parallel", "arbitrary")),
    )(x2d, gates, w_gate, w_up, w_down, ws_gate, ws_up, ws_down, g, b)


def _gather_cols(w, order):
    parts = []
    for item in order:
        if isinstance(item, tuple):
            parts.append(jnp.zeros(w.shape[:-1] + (item[1],), w.dtype))
        else:
            off, width = SEG_OFF[item]
            parts.append(w[..., off:off + width])
    return jnp.concatenate(parts, axis=-1)


def _rotate_half_cols(w):
    lead = w.shape[:-1]
    w4 = w.reshape(lead + (-1, 2, HEAD_DIM // 2))
    return jnp.concatenate([-w4[..., 1:2, :], w4[..., 0:1, :]], axis=-2).reshape(w.shape)


def _rope_tables(seq, width):
    half = HEAD_DIM // 2
    inv_freq = ROPE_THETA ** (-jnp.arange(half, dtype=F32) * (2.0 / HEAD_DIM))
    ang = jnp.arange(seq).astype(F32)[:, None] * inv_freq[None, :]
    reps = width // half
    return jnp.tile(jnp.cos(ang), (1, reps)), jnp.tile(jnp.sin(ang), (1, reps))


def _token_rows(tok, b, seq):
    t4 = tok.reshape(b, seq, NSA_KV_HEADS, HEAD_DIM).transpose(0, 2, 1, 3)
    return t4.reshape(b * NSA_KV_HEADS, seq // CMP_STRIDE, CMP_STRIDE * HEAD_DIM)


def kernel(x, w_in, fox_forget_bias, cmp_pos_k, cmp_w1_k, cmp_w2_k, cmp_pos_v, cmp_w1_v, cmp_w2_v, w_out,
           ln1_g, ln1_b, w_router, b_router, w_gate, w_up, w_down, ws_gate, ws_up, ws_down, ln2_g, ln2_b):
    b, seq, dm = x.shape
    depth = w_in.shape[0]
    n = b * seq
    alpha = float((2 * depth) ** 0.25)

    w_plain = _gather_cols(w_in, PLAIN_ORDER).astype(BF16)
    w_rope = _gather_cols(w_in, ROPE_ORDER)
    w_rope_a = w_rope.astype(BF16)
    w_rope_b = _rotate_half_cols(w_rope).astype(BF16)
    cos_t, sin_t = _rope_tables(seq, PROJ_TILE_COLS)
    fw, nw = FOX_HEADS * HEAD_DIM, NSA_HEADS * HEAD_DIM
    w_out_b = w_out.astype(BF16)

    x2d = x.reshape(n, dm)
    for l in range(depth):
        hp, hr = _project(x2d, w_plain[l], w_rope_a[l], w_rope_b[l], cos_t, sin_t, seq)
        hp3 = hp.reshape(b, seq, PLAIN_W)
        hr3 = hr.reshape(b, seq, ROPE_W)

        f_logit = hp3[:, :, SMALL_COL + SMALL_F:SMALL_COL + SMALL_F + FOX_HEADS]
        c_rows = _fox_cumsum(f_logit.transpose(0, 2, 1), fox_forget_bias[l].reshape(FOX_HEADS, 1))
        o_fox = _fox_attention(hp3, c_rows.transpose(0, 2, 1), c_rows)

        kvw = NSA_KV_HEADS * HEAD_DIM
        rk = _token_rows(hr3[:, :, ROPE_POS['nsa_kc']:ROPE_POS['nsa_kc'] + kvw], b, seq)
        rv = _token_rows(hp3[:, :, PLAIN_POS['nsa_vc']:PLAIN_POS['nsa_vc'] + kvw], b, seq)
        k_cmp, v_cmp = _nsa_compress(
            rk, rv, cmp_pos_k[l].reshape(1, -1), cmp_pos_v[l].reshape(1, -1),
            cmp_w1_k[l].astype(BF16), cmp_w2_k[l].astype(BF16), cmp_w1_v[l].astype(BF16), cmp_w2_v[l].astype(BF16))
        n_rows = seq // CMP_STRIDE
        o_nsa = _nsa_attention(hr3, hp3, k_cmp.reshape(b, NSA_KV_HEADS, n_rows, HEAD_DIM),
                               v_cmp.reshape(b, NSA_KV_HEADS, n_rows, HEAD_DIM))

        o_dsa = _dsa_attention(hr3, hp3)

        x2d = _outproj_ln(o_fox.reshape(n, fw), o_nsa.reshape(n, nw), o_dsa.reshape(n, nw), x2d,
                          w_out_b[l, :fw], w_out_b[l, fw:fw + nw], w_out_b[l, fw + nw:],
                          ln1_g[l].reshape(1, dm), ln1_b[l].reshape(1, dm), alpha)

        gates_t = _router(x2d, w_router[l].T.astype(BF16), b_router[l].reshape(N_EXPERTS, 1))
        x2d = _moe_ln(x2d, gates_t.T, l, w_gate, w_up, w_down,
                      ws_gate[l].astype(BF16), ws_up[l].astype(BF16), ws_down[l].astype(BF16),
                      ln2_g[l].reshape(1, dm), ln2_b[l].reshape(1, dm), alpha)
    return x2d.reshape(b, seq, dm)
```

```python
import functools

import numpy as np
import jax
import jax.numpy as jnp
from jax import lax
from jax.experimental import pallas as pl
from jax.experimental.pallas import tpu as pltpu

D_MODEL = 1024
HEAD_DIM = 64
FOX_HEADS = 4
NSA_HEADS = 6
NSA_KV_HEADS = 2
NSA_REP = NSA_HEADS // NSA_KV_HEADS
DSA_HEADS = 6
ROPE_THETA = 10000.0
CMP_LEN = 32
CMP_STRIDE = 16
CMP_HIDDEN = 2 * HEAD_DIM
SEL_BLOCK = 64
SEL_TOPN = 16
WINDOW = 512
IDX_HEADS = 4
IDX_DIM = 64
DSA_TOPK = 256
N_EXPERTS = 64
N_GROUPS = 8
GROUP_SIZE = N_EXPERTS // N_GROUPS
TOPK_GROUPS = 4
MOE_TOPK = 8
EXPERT_DIM = 256
SHARED_DIM = 256
ROUTED_SCALE = 2.5
LN_EPS = 1e-5
NEG = -1e30
FORCE = 1e6
SCALE = HEAD_DIM ** -0.5

F32 = jnp.float32
BF16 = jnp.bfloat16
I32 = jnp.int32

VMEM_LIMIT_BYTES = 52 * 1024 * 1024
LANES = 128
KEY_STEP = 512
PROJ_TILE_COLS = 512

SEGMENTS = (
    ('fox_q', FOX_HEADS * HEAD_DIM), ('fox_k', FOX_HEADS * HEAD_DIM),
    ('fox_v', FOX_HEADS * HEAD_DIM), ('fox_f', FOX_HEADS),
    ('nsa_q', NSA_HEADS * HEAD_DIM),
    ('nsa_kc', NSA_KV_HEADS * HEAD_DIM), ('nsa_vc', NSA_KV_HEADS * HEAD_DIM),
    ('nsa_ks', NSA_KV_HEADS * HEAD_DIM), ('nsa_vs', NSA_KV_HEADS * HEAD_DIM),
    ('nsa_kw', NSA_KV_HEADS * HEAD_DIM), ('nsa_vw', NSA_KV_HEADS * HEAD_DIM),
    ('nsa_g', 3 * NSA_HEADS),
    ('dsa_q', DSA_HEADS * HEAD_DIM), ('dsa_k', HEAD_DIM), ('dsa_v', HEAD_DIM),
    ('idx_q', IDX_HEADS * IDX_DIM), ('idx_k', IDX_DIM), ('idx_w', IDX_HEADS),
)
SEG_OFF = {}
_off = 0
for _name, _width in SEGMENTS:
    SEG_OFF[_name] = (_off, _width)
    _off += _width
N_IN = _off

PLAIN_ORDER = ('fox_q', 'fox_k', 'fox_v', 'nsa_vc', 'nsa_vs', 'nsa_vw', 'dsa_v', (None, 64),
               'fox_f', 'nsa_g', 'idx_w', (None, 2 * LANES - FOX_HEADS - 3 * NSA_HEADS - IDX_HEADS))
ROPE_ORDER = ('nsa_q', 'nsa_kc', 'nsa_ks', 'nsa_kw', 'dsa_q', 'dsa_k', 'idx_k', 'idx_q')


def _layout(order):
    pos, off = {}, 0
    for item in order:
        if isinstance(item, tuple):
            off += item[1]
        else:
            pos[item] = off
            off += SEG_OFF[item][1]
    return pos, off


PLAIN_POS, PLAIN_W = _layout(PLAIN_ORDER)
ROPE_POS, ROPE_W = _layout(ROPE_ORDER)
SMALL_COL = PLAIN_POS['fox_f']
SMALL_F = 0
SMALL_G = FOX_HEADS
SMALL_W = FOX_HEADS + 3 * NSA_HEADS


def _cparams(sem):
    return pltpu.CompilerParams(dimension_semantics=sem, vmem_limit_bytes=VMEM_LIMIT_BYTES)


def _nt_dot(a, b):
    return lax.dot_general(a, b, (((1,), (1,)), ((), ())), preferred_element_type=F32)


def _dot(a, b):
    return jnp.dot(a, b, preferred_element_type=F32)


def _softmax_rows(s):
    m = jnp.max(s, axis=-1, keepdims=True)
    p = jnp.exp(s - m)
    return p, jnp.sum(p, axis=-1, keepdims=True)


def _for_key_extent(q_last, seq, body, step=KEY_STEP):
    n = seq // step
    if n <= 1:
        body(seq)
        return
    c = q_last // step
    for i in range(n):
        pl.when(c == i)(functools.partial(body, (i + 1) * step))


def _layer_norm(z, g, b):
    mu = jnp.mean(z, axis=-1, keepdims=True)
    zc = z - mu
    var = jnp.mean(zc * zc, axis=-1, keepdims=True)
    return zc * lax.rsqrt(var + LN_EPS) * g + b


def _proj_kernel(x_ref, w_ref, o_ref, xb_ref):
    @pl.when(pl.program_id(1) == 0)
    def _():
        xb_ref[...] = x_ref[...].astype(BF16)

    o_ref[...] = _dot(xb_ref[...], w_ref[...])


def _proj_rope_kernel(x_ref, wa_ref, wb_ref, cos_ref, sin_ref, o_ref, xb_ref):
    @pl.when(pl.program_id(1) == 0)
    def _():
        xb_ref[...] = x_ref[...].astype(BF16)

    xb = xb_ref[...]
    o_ref[...] = _dot(xb, wa_ref[...]) * cos_ref[...] + _dot(xb, wb_ref[...]) * sin_ref[...]


def _project(x2d, w_plain, w_a, w_b, cos_t, sin_t, seq):
    n = x2d.shape[0]
    tm = min(1024, seq)
    tc = cos_t.shape[1]
    plain = pl.pallas_call(
        _proj_kernel,
        grid=(n // tm, PLAIN_W // tc),
        in_specs=[pl.BlockSpec((tm, D_MODEL), lambda i, j: (i, 0)),
                  pl.BlockSpec((D_MODEL, tc), lambda i, j: (0, j))],
        out_specs=pl.BlockSpec((tm, tc), lambda i, j: (i, j)),
        out_shape=jax.ShapeDtypeStruct((n, PLAIN_W), F32),
        scratch_shapes=[pltpu.VMEM((tm, D_MODEL), BF16)],
        compiler_params=_cparams(("parallel", "arbitrary")),
    )(x2d, w_plain)
    nt = seq // tm
    roped = pl.pallas_call(
        _proj_rope_kernel,
        grid=(n // tm, ROPE_W // tc),
        in_specs=[pl.BlockSpec((tm, D_MODEL), lambda i, j: (i, 0)),
                  pl.BlockSpec((D_MODEL, tc), lambda i, j: (0, j)),
                  pl.BlockSpec((D_MODEL, tc), lambda i, j: (0, j)),
                  pl.BlockSpec((tm, tc), lambda i, j: (i % nt, 0)),
                  pl.BlockSpec((tm, tc), lambda i, j: (i % nt, 0))],
        out_specs=pl.BlockSpec((tm, tc), lambda i, j: (i, j)),
        out_shape=jax.ShapeDtypeStruct((n, ROPE_W), F32),
        scratch_shapes=[pltpu.VMEM((tm, D_MODEL), BF16)],
        compiler_params=_cparams(("parallel", "arbitrary")),
    )(x2d, w_a, w_b, cos_t, sin_t)
    return plain, roped


def _fox_cum_kernel(f_ref, fb_ref, o_ref):
    x = f_ref[0] + fb_ref[...]
    c = jnp.minimum(x, 0.0) - jnp.log1p(jnp.exp(-jnp.abs(x)))
    seq = c.shape[-1]
    lane = lax.broadcasted_iota(I32, c.shape, 1)
    sh = 1
    while sh < seq:
        c = c + jnp.where(lane >= sh, pltpu.roll(c, sh, 1), 0.0)
        sh *= 2
    o_ref[0] = c


def _fox_cumsum(f_rows, f_bias):
    b, h, seq = f_rows.shape
    return pl.pallas_call(
        _fox_cum_kernel,
        grid=(b,),
        in_specs=[pl.BlockSpec((1, h, seq), lambda i: (i, 0, 0)),
                  pl.BlockSpec((h, 1), lambda i: (0, 0))],
        out_specs=pl.BlockSpec((1, h, seq), lambda i: (i, 0, 0)),
        out_shape=jax.ShapeDtypeStruct((b, h, seq), F32),
        compiler_params=_cparams(("parallel",)),
    )(f_rows, f_bias)


def _fox_kernel(q_ref, k_ref, v_ref, cc_ref, cr_ref, o_ref, *, tq, seq):
    q0 = pl.program_id(1) * tq

    def body(klen):
        qpos = q0 + lax.broadcasted_iota(I32, (tq, klen), 0)
        causal = lax.broadcasted_iota(I32, (tq, klen), 1) <= qpos
        for h in range(FOX_HEADS):
            sl = slice(h * HEAD_DIM, (h + 1) * HEAD_DIM)
            qh = (q_ref[0, :, sl] * SCALE).astype(BF16)
            kh = k_ref[0, :klen, sl].astype(BF16)
            vh = v_ref[0, :klen, sl].astype(BF16)
            s = _nt_dot(qh, kh) + (cc_ref[0, :, h:h + 1] - cr_ref[0, h:h + 1, :klen])
            p, l = _softmax_rows(jnp.where(causal, s, NEG))
            o_ref[0, :, sl] = (_dot(p.astype(BF16), vh) / l).astype(o_ref.dtype)

    _for_key_extent(q0 + tq - 1, seq, body, step=tq)


def _fox_attention(hp3, c_cols, c_rows):
    b, seq, _ = hp3.shape
    tq = min(256, seq)
    w = FOX_HEADS * HEAD_DIM
    return pl.pallas_call(
        functools.partial(_fox_kernel, tq=tq, seq=seq),
        grid=(b, seq // tq),
        in_specs=[pl.BlockSpec((1, tq, w), lambda i, j: (i, j, PLAIN_POS['fox_q'] // w)),
                  pl.BlockSpec((1, seq, w), lambda i, j: (i, 0, PLAIN_POS['fox_k'] // w)),
                  pl.BlockSpec((1, seq, w), lambda i, j: (i, 0, PLAIN_POS['fox_v'] // w)),
                  pl.BlockSpec((1, tq, FOX_HEADS), lambda i, j: (i, j, 0)),
                  pl.BlockSpec((1, FOX_HEADS, seq), lambda i, j: (i, 0, 0))],
        out_specs=pl.BlockSpec((1, tq, w), lambda i, j: (i, j, 0)),
        out_shape=jax.ShapeDtypeStruct((b, seq, w), BF16),
        compiler_params=_cparams(("parallel", "arbitrary")),
    )(hp3, hp3, hp3, c_cols, c_rows)


def _gelu_tanh(x):
    return 0.5 * x * (1.0 + jnp.tanh(np.float32(np.sqrt(2.0 / np.pi)) * (x + 0.044715 * (x * x * x))))


def _compress_kernel(rk_ref, rv_ref, pek_ref, pev_ref, w1k_ref, w2k_ref, w1v_ref, w2v_ref, ok_ref, ov_ref):
    half = CMP_STRIDE * HEAD_DIM

    def one(r_ref, pe_ref, w1_ref, w2_ref, o_ref):
        r = r_ref[0]
        n_rows = r.shape[0]
        lo = _dot((r + pe_ref[:, :half]).astype(BF16), w1_ref[:half, :])
        hi = _dot((r + pe_ref[:, half:]).astype(BF16), w1_ref[half:, :])
        hid = _gelu_tanh(lo + pltpu.roll(hi, n_rows - 1, 0))
        o_ref[0] = _dot(hid.astype(BF16), w2_ref[...])

    one(rk_ref, pek_ref, w1k_ref, w2k_ref, ok_ref)
    one(rv_ref, pev_ref, w1v_ref, w2v_ref, ov_ref)


def _nsa_compress(rk, rv, pek, pev, w1k, w2k, w1v, w2v):
    bg, rows, width = rk.shape
    tok = pl.BlockSpec((1, rows, width), lambda i: (i, 0, 0))
    full = lambda a: pl.BlockSpec(a.shape, lambda i: (0,) * a.ndim)
    out = pl.BlockSpec((1, rows, HEAD_DIM), lambda i: (i, 0, 0))
    return pl.pallas_call(
        _compress_kernel,
        grid=(bg,),
        in_specs=[tok, tok, full(pek), full(pev), full(w1k), full(w2k), full(w1v), full(w2v)],
        out_specs=[out, out],
        out_shape=[jax.ShapeDtypeStruct((bg, rows, HEAD_DIM), F32)] * 2,
        compiler_params=_cparams(("parallel",)),
    )(rk, rv, pek, pev, w1k, w2k, w1v, w2v)


def _split3_nt_dot(b01, a):
    a1 = a.astype(BF16)
    r1 = a - a1.astype(F32)
    a2 = r1.astype(BF16)
    a3 = (r1 - a2.astype(F32)).astype(BF16)
    return _nt_dot(b01, a1) + _nt_dot(b01, a2) + _nt_dot(b01, a3)


def _nsa_kernel(q_ref, kc_ref, vc_ref, ks_ref, kw_ref, vs_ref, vw_ref, sm_ref, o_ref, oslc_ref, *, tq, seq, wlen):
    q0 = pl.program_id(1) * tq
    n_c = kc_ref.shape[2]
    n_s = seq // SEL_BLOCK
    rows = NSA_REP * tq
    tcol = q0 + lax.broadcasted_iota(I32, (tq, 1), 0)
    tcol_r = jnp.concatenate([tcol] * NSA_REP, axis=0)

    cidx = lax.broadcasted_iota(I32, (rows, n_c), 1)
    vis_r = (cidx * CMP_STRIDE + (CMP_LEN - 1)) <= tcol_r
    any_vis_r = (tcol_r >= (CMP_LEN - 1)).astype(F32)
    oc = lax.broadcasted_iota(I32, (n_s, n_c), 1) * CMP_STRIDE
    ob = lax.broadcasted_iota(I32, (n_s, n_c), 0) * SEL_BLOCK
    overlap_t = ((oc < ob + SEL_BLOCK) & (oc + CMP_LEN > ob)).astype(BF16)
    jj_t = lax.broadcasted_iota(I32, (n_s, tq), 0)
    blk_t = (q0 + lax.broadcasted_iota(I32, (1, tq), 1)) // SEL_BLOCK
    w0 = pl.multiple_of(jnp.maximum(q0 + tq - wlen, 0), 8)
    dist = tcol_r - (w0 + lax.broadcasted_iota(I32, (rows, wlen), 1))
    band_r = (dist >= 0) & (dist < WINDOW)
    gates = sm_ref[0]

    qs_all, qs_aug_all, o_cmp_all = [], [], []
    for g in range(NSA_KV_HEADS):
        qs = (jnp.concatenate(
            [q_ref[0, :, (g * NSA_REP + r) * HEAD_DIM:(g * NSA_REP + r + 1) * HEAD_DIM] for r in range(NSA_REP)],
            axis=0) * SCALE).astype(BF16)

        s_c = _nt_dot(qs, kc_ref[0, g].astype(BF16))
        p_c, l_c = _softmax_rows(jnp.where(vis_r, s_c, NEG))
        p_c = p_c / l_c * any_vis_r
        o_cmp_all.append(_dot(p_c.astype(BF16), vc_ref[0, g].astype(BF16)))

        p_sum = p_c[0:tq]
        for r in range(1, NSA_REP):
            p_sum = p_sum + p_c[r * tq:(r + 1) * tq]
        imp = _split3_nt_dot(overlap_t, p_sum)
        forced = (jj_t == 0) | (jj_t == blk_t) | (jj_t == blk_t - 1)
        imp = jnp.where(jj_t <= blk_t, jnp.where(forced, FORCE, imp), -1.0)
        rank = jnp.zeros((n_s, tq), F32)
        for j2 in range(n_s):
            row = imp[j2:j2 + 1, :]
            beats = (row > imp) | ((row == imp) & (j2 < jj_t))
            rank = rank + beats.astype(F32)
        sel = (rank < float(min(SEL_TOPN, n_s))) & (jj_t <= blk_t)
        bias_t = jnp.concatenate([jnp.where(sel, 0.0, NEG), jnp.zeros((tq - n_s, tq), F32)], axis=0)
        sel_bias = bias_t.T[:, :LANES - HEAD_DIM]
        qs_all.append(qs)
        qs_aug_all.append(jnp.concatenate([qs, jnp.concatenate([sel_bias] * NSA_REP, axis=0).astype(BF16)], axis=1))

    def slc_body(klen):
        block_onehot = (lax.broadcasted_iota(I32, (klen, LANES - HEAD_DIM), 0) // SEL_BLOCK
                        == lax.broadcasted_iota(I32, (klen, LANES - HEAD_DIM), 1)).astype(BF16)
        causal_r = lax.broadcasted_iota(I32, (rows, klen), 1) <= tcol_r
        for g in range(NSA_KV_HEADS):
            ksl = slice(g * HEAD_DIM, (g + 1) * HEAD_DIM)
            k_aug = jnp.concatenate([ks_ref[0, :klen, ksl].astype(BF16), block_onehot], axis=1)
            p_s, l_s = _softmax_rows(jnp.where(causal_r, _nt_dot(qs_aug_all[g], k_aug), NEG))
            oslc_ref[g] = _dot(p_s.astype(BF16), vs_ref[0, :klen, ksl].astype(BF16)) / l_s

    _for_key_extent(q0 + tq - 1, seq, slc_body)

    for g in range(NSA_KV_HEADS):
        ksl = slice(g * HEAD_DIM, (g + 1) * HEAD_DIM)
        o_cmp, o_slc = o_cmp_all[g], oslc_ref[g]

        kw = kw_ref[0, pl.ds(w0, wlen), ksl].astype(BF16)
        vw = vw_ref[0, pl.ds(w0, wlen), ksl].astype(BF16)
        p_w, l_w = _softmax_rows(jnp.where(band_r, _nt_dot(qs_all[g], kw), NEG))
        o_win = _dot(p_w.astype(BF16), vw) / l_w

        for r in range(NSA_REP):
            head = g * NSA_REP + r
            rs = slice(r * tq, (r + 1) * tq)

            def gate(branch, head=head):
                col = SMALL_G + branch * NSA_HEADS + head
                return 1.0 / (1.0 + jnp.exp(-gates[:, col:col + 1]))

            out = gate(0) * o_cmp[rs] + gate(1) * o_slc[rs] + gate(2) * o_win[rs]
            o_ref[0, :, head * HEAD_DIM:(head + 1) * HEAD_DIM] = out.astype(o_ref.dtype)


def _nsa_attention(hr3, hp3, k_cmp, v_cmp):
    b, seq, _ = hr3.shape
    tq = min(256, seq)
    wlen = min(WINDOW + tq, seq)
    qw = NSA_HEADS * HEAD_DIM
    kvw = NSA_KV_HEADS * HEAD_DIM
    n_c = k_cmp.shape[2]
    cmp_spec = pl.BlockSpec((1, NSA_KV_HEADS, n_c, HEAD_DIM), lambda i, j: (i, 0, 0, 0))

    def seq_spec(col):
        return pl.BlockSpec((1, seq, kvw), lambda i, j: (i, 0, col // kvw))

    return pl.pallas_call(
        functools.partial(_nsa_kernel, tq=tq, seq=seq, wlen=wlen),
        grid=(b, seq // tq),
        in_specs=[pl.BlockSpec((1, tq, qw), lambda i, j: (i, j, ROPE_POS['nsa_q'] // qw)),
                  cmp_spec, cmp_spec,
                  seq_spec(ROPE_POS['nsa_ks']), seq_spec(ROPE_POS['nsa_kw']),
                  seq_spec(PLAIN_POS['nsa_vs']), seq_spec(PLAIN_POS['nsa_vw']),
                  pl.BlockSpec((1, tq, LANES), lambda i, j: (i, j, SMALL_COL // LANES))],
        out_specs=pl.BlockSpec((1, tq, qw), lambda i, j: (i, j, 0)),
        out_shape=jax.ShapeDtypeStruct((b, seq, qw), BF16),
        scratch_shapes=[pltpu.VMEM((NSA_KV_HEADS, NSA_REP * tq, HEAD_DIM), F32)],
        compiler_params=_cparams(("parallel", "arbitrary")),
    )(hr3, k_cmp, v_cmp, hr3, hr3, hp3, hp3, hp3)


def _dsa_kernel(q_ref, kk_ref, v_ref, iq_ref, sm_ref, o_ref, key_ref, bias_ref, *, tq, seq, topk):
    q0 = pl.program_id(1) * tq
    trow = q0 + lax.broadcasted_iota(I32, (1, tq), 1)
    w_idx = sm_ref[0].T[SMALL_W:SMALL_W + IDX_HEADS, :] * (IDX_HEADS ** -0.5) * (IDX_DIM ** -0.5)
    kf = float(topk)

    def body(klen):
        ik = kk_ref[0, :klen, HEAD_DIM:2 * HEAD_DIM].astype(BF16)
        score = jnp.zeros((klen, tq), F32)
        for h in range(IDX_HEADS):
            d = _nt_dot(ik, iq_ref[0, :, h * IDX_DIM:(h + 1) * IDX_DIM].astype(BF16))
            score = score + w_idx[h:h + 1, :] * jnp.maximum(d, 0.0)
        score = jnp.where(lax.broadcasted_iota(I32, (klen, tq), 0) <= trow, score, NEG)

        bits = lax.bitcast_convert_type(score, I32)
        key_ref[:klen, :] = jnp.where(bits < 0, bits ^ 0x7FFFFFFF, bits)

        def count(compare, bound):
            bound8 = jnp.broadcast_to(bound, (8, tq))
            accs = [jnp.zeros((8, tq), F32) for _ in range(4)]
            for j in range(klen // 8):
                accs[j % 4] = accs[j % 4] + compare(key_ref[j * 8:(j + 1) * 8, :], bound8).astype(F32)
            return jnp.sum((accs[0] + accs[1]) + (accs[2] + accs[3]), axis=0, keepdims=True)

        def tau_step(i, tau):
            cand = tau + jnp.left_shift(jnp.int32(1), 31 - i)
            return jnp.where(count(jnp.greater_equal, cand) >= kf, cand, tau)

        tau = lax.fori_loop(0, 32, tau_step, jnp.full((1, tq), -2 ** 31, I32))
        need = kf - count(jnp.greater, tau)

        tri = (lax.broadcasted_iota(I32, (LANES, LANES), 0) >= lax.broadcasted_iota(I32, (LANES, LANES), 1)).astype(BF16)
        seen = jnp.zeros((1, tq), F32)
        for blk in range(klen // LANES):
            ks = slice(blk * LANES, (blk + 1) * LANES)
            key = key_ref[ks, :]
            tie = key == tau
            running = _dot(tri, jnp.where(tie, 1.0, 0.0).astype(BF16)) + seen
            seen = running[LANES - 1:LANES, :]
            kpos = blk * LANES + lax.broadcasted_iota(I32, (LANES, tq), 0)
            chosen = ((key > tau) | (tie & (running <= need))) & (kpos <= trow)
            bias_ref[:, ks] = jnp.where(chosen, 0.0, NEG).T

        k = kk_ref[0, :klen, 0:HEAD_DIM].astype(BF16)
        v = v_ref[0, :klen, 0:HEAD_DIM].astype(BF16)
        bias = bias_ref[:, :klen]
        for h in range(DSA_HEADS):
            sl = slice(h * HEAD_DIM, (h + 1) * HEAD_DIM)
            s = _nt_dot((q_ref[0, :, sl] * SCALE).astype(BF16), k)
            p, l = _softmax_rows(s + bias)
            o_ref[0, :, sl] = (_dot(p.astype(BF16), v) / l).astype(o_ref.dtype)

    _for_key_extent(q0 + tq - 1, seq, body)


def _dsa_attention(hr3, hp3):
    b, seq, _ = hr3.shape
    tq = min(256, seq)
    topk = min(DSA_TOPK, seq // 4)
    qw = DSA_HEADS * HEAD_DIM
    iqw = IDX_HEADS * IDX_DIM
    return pl.pallas_call(
        functools.partial(_dsa_kernel, tq=tq, seq=seq, topk=topk),
        grid=(b, seq // tq),
        in_specs=[pl.BlockSpec((1, tq, qw), lambda i, j: (i, j, ROPE_POS['dsa_q'] // qw)),
                  pl.BlockSpec((1, seq, LANES), lambda i, j: (i, 0, ROPE_POS['dsa_k'] // LANES)),
                  pl.BlockSpec((1, seq, LANES), lambda i, j: (i, 0, PLAIN_POS['dsa_v'] // LANES)),
                  pl.BlockSpec((1, tq, iqw), lambda i, j: (i, j, ROPE_POS['idx_q'] // iqw)),
                  pl.BlockSpec((1, tq, LANES), lambda i, j: (i, j, SMALL_COL // LANES))],
        out_specs=pl.BlockSpec((1, tq, qw), lambda i, j: (i, j, 0)),
        out_shape=jax.ShapeDtypeStruct((b, seq, qw), BF16),
        scratch_shapes=[pltpu.VMEM((seq, tq), I32), pltpu.VMEM((tq, seq), F32)],
        compiler_params=_cparams(("parallel", "arbitrary")),
    )(hr3, hr3, hp3, hr3, hp3)


def _route_gates(x, wr_t, br_col):
    tn = x.shape[0]
    logits = _nt_dot(wr_t, x.astype(BF16))
    s = 1.0 / (1.0 + jnp.exp(-logits))
    sb = s + br_col
    low = jnp.float32(-3e38)

    grp = []
    for gi in range(N_GROUPS):
        blk = sb[gi * GROUP_SIZE:(gi + 1) * GROUP_SIZE]
        m1 = jnp.max(blk, axis=0, keepdims=True)
        is_max = blk == m1
        n_max = jnp.sum(is_max.astype(F32), axis=0, keepdims=True)
        m2 = jnp.max(jnp.where(is_max, low, blk), axis=0, keepdims=True)
        grp.append(m1 + jnp.where(n_max >= 2.0, m1, m2))
    masked = []
    for gi in range(N_GROUPS):
        rank = jnp.zeros((1, tn), F32)
        for g2 in range(N_GROUPS):
            if g2 == gi:
                continue
            beats = (grp[g2] > grp[gi]) | ((grp[g2] == grp[gi]) if g2 < gi else False)
            rank = rank + beats.astype(F32)
        keep = rank < float(TOPK_GROUPS)
        masked.append(jnp.where(keep, sb[gi * GROUP_SIZE:(gi + 1) * GROUP_SIZE], NEG))
    masked = jnp.concatenate(masked, axis=0)

    eidx = lax.broadcasted_iota(I32, (N_EXPERTS, tn), 0).astype(F32)
    taken = jnp.zeros((N_EXPERTS, tn), F32)
    for _ in range(MOE_TOPK):
        top = jnp.max(masked, axis=0, keepdims=True)
        first = jnp.min(jnp.where(masked == top, eidx, float(N_EXPERTS)), axis=0, keepdims=True)
        hit = eidx == first
        taken = jnp.where(hit, 1.0, taken)
        masked = jnp.where(hit, low, masked)
    gw = s * taken
    return gw / jnp.sum(gw, axis=0, keepdims=True) * ROUTED_SCALE


def _outproj_kernel(of_ref, on_ref, od_ref, x_ref, wf_ref, wn_ref, wd_ref, g_ref, b_ref, wr_ref, br_ref,
                    o_ref, gate_ref, *, alpha):
    mix = _dot(of_ref[...], wf_ref[...]) + _dot(on_ref[...], wn_ref[...]) + _dot(od_ref[...], wd_ref[...])
    x_new = _layer_norm(alpha * x_ref[...] + mix, g_ref[...], b_ref[...])
    o_ref[...] = x_new
    gate_ref[...] = _route_gates(x_new, wr_ref[...], br_ref[...])


def _outproj_ln_route(o_fox, o_nsa, o_dsa, x2d, w_f, w_n, w_d, g, b, wr_t, br_col, alpha):
    n = x2d.shape[0]
    tm = min(512, n)
    row = lambda a: pl.BlockSpec((tm, a.shape[1]), lambda i: (i, 0))
    full = lambda a: pl.BlockSpec(a.shape, lambda i: (0, 0))
    return pl.pallas_call(
        functools.partial(_outproj_kernel, alpha=alpha),
        grid=(n // tm,),
        in_specs=[row(o_fox), row(o_nsa), row(o_dsa), row(x2d), full(w_f), full(w_n), full(w_d), full(g), full(b),
                  full(wr_t), full(br_col)],
        out_specs=[pl.BlockSpec((tm, D_MODEL), lambda i: (i, 0)), pl.BlockSpec((N_EXPERTS, tm), lambda i: (0, i))],
        out_shape=[jax.ShapeDtypeStruct((n, D_MODEL), F32), jax.ShapeDtypeStruct((N_EXPERTS, n), F32)],
        compiler_params=_cparams(("parallel",)),
    )(o_fox, o_nsa, o_dsa, x2d, w_f, w_n, w_d, g, b, wr_t, br_col)


def _silu(x):
    return x / (1.0 + jnp.exp(-x))


def _moe_kernel(x_ref, gt_ref, wg_ref, wu_ref, wd_ref, sg_ref, su_ref, sd_ref, lg_ref, lb_ref, o_ref,
                xb_ref, acc_ref, *, alpha):
    e = pl.program_id(1)

    @pl.when(e == 0)
    def _():
        xb = x_ref[...].astype(BF16)
        xb_ref[...] = xb
        h = _silu(_dot(xb, sg_ref[...])) * _dot(xb, su_ref[...])
        acc_ref[...] = _dot(h.astype(BF16), sd_ref[...])

    xb = xb_ref[...]
    h = (_silu(_dot(xb, wg_ref[0, 0].astype(BF16))) * _dot(xb, wu_ref[0, 0].astype(BF16))).astype(BF16)
    gates = gt_ref[...]
    lane = lax.broadcasted_iota(I32, gates.shape, 1)
    gcol = jnp.sum(jnp.where(lane == e, gates, 0.0), axis=1, keepdims=True)
    chunk = 256
    for c in range(D_MODEL // chunk):
        cs = slice(c * chunk, (c + 1) * chunk)
        acc_ref[:, cs] += gcol * _dot(h, wd_ref[0, 0, :, cs].astype(BF16))

    @pl.when(e == pl.num_programs(1) - 1)
    def _():
        o_ref[...] = _layer_norm(alpha * x_ref[...] + acc_ref[...], lg_ref[...], lb_ref[...])


def _moe_ln(x2d, gates, layer, w_gate, w_up, w_down, ws_gate, ws_up, ws_down, g, b, alpha):
    n = x2d.shape[0]
    tn = min(2048, n)
    full = lambda a: pl.BlockSpec(a.shape, lambda i, e: (0,) * a.ndim)
    once = pl.Buffered(1)
    return pl.pallas_call(
        functools.partial(_moe_kernel, alpha=alpha),
        grid=(n // tn, N_EXPERTS),
        in_specs=[pl.BlockSpec((tn, D_MODEL), lambda i, e: (i, 0), pipeline_mode=once),
                  pl.BlockSpec((tn, N_EXPERTS), lambda i, e: (i, 0)),
                  pl.BlockSpec((1, 1, D_MODEL, EXPERT_DIM), lambda i, e: (layer, e, 0, 0)),
                  pl.BlockSpec((1, 1, D_MODEL, EXPERT_DIM), lambda i, e: (layer, e, 0, 0)),
                  pl.BlockSpec((1, 1, EXPERT_DIM, D_MODEL), lambda i, e: (layer, e, 0, 0)),
                  full(ws_gate), full(ws_up), full(ws_down), full(g), full(b)],
        out_specs=pl.BlockSpec((tn, D_MODEL), lambda i, e: (i, 0), pipeline_mode=once),
        out_shape=jax.ShapeDtypeStruct((n, D_MODEL), F32),
        scratch_shapes=[pltpu.VMEM((tn, D_MODEL), BF16), pltpu.VMEM((tn, D_MODEL), F32)],
        compiler_params=_cparams(("parallel", "arbitrary")),
    )(x2d, gates, w_gate, w_up, w_down, ws_gate, ws_up, ws_down, g, b)


def _gather_cols(w, order):
    parts = []
    for item in order:
        if isinstance(item, tuple):
            parts.append(jnp.zeros(w.shape[:-1] + (item[1],), w.dtype))
        else:
            off, width = SEG_OFF[item]
            parts.append(w[..., off:off + width])
    return jnp.concatenate(parts, axis=-1)


def _rotate_half_cols(w):
    lead = w.shape[:-1]
    w4 = w.reshape(lead + (-1, 2, HEAD_DIM // 2))
    return jnp.concatenate([-w4[..., 1:2, :], w4[..., 0:1, :]], axis=-2).reshape(w.shape)


def _rope_tables(seq, width):
    half = HEAD_DIM // 2
    inv_freq = ROPE_THETA ** (-jnp.arange(half, dtype=F32) * (2.0 / HEAD_DIM))
    ang = jnp.arange(seq).astype(F32)[:, None] * inv_freq[None, :]
    reps = width // half
    return jnp.tile(jnp.cos(ang), (1, reps)), jnp.tile(jnp.sin(ang), (1, reps))


def _token_rows(tok, b, seq):
    t4 = tok.reshape(b, seq, NSA_KV_HEADS, HEAD_DIM).transpose(0, 2, 1, 3)
    return t4.reshape(b * NSA_KV_HEADS, seq // CMP_STRIDE, CMP_STRIDE * HEAD_DIM)


def kernel(x, w_in, fox_forget_bias, cmp_pos_k, cmp_w1_k, cmp_w2_k, cmp_pos_v, cmp_w1_v, cmp_w2_v, w_out,
           ln1_g, ln1_b, w_router, b_router, w_gate, w_up, w_down, ws_gate, ws_up, ws_down, ln2_g, ln2_b):
    b, seq, dm = x.shape
    depth = w_in.shape[0]
    n = b * seq
    alpha = float((2 * depth) ** 0.25)

    w_plain = _gather_cols(w_in, PLAIN_ORDER).astype(BF16)
    w_rope = _gather_cols(w_in, ROPE_ORDER)
    w_rope_a = w_rope.astype(BF16)
    w_rope_b = _rotate_half_cols(w_rope).astype(BF16)
    cos_t, sin_t = _rope_tables(seq, PROJ_TILE_COLS)
    fw, nw = FOX_HEADS * HEAD_DIM, NSA_HEADS * HEAD_DIM
    w_out_b = w_out.astype(BF16)

    x2d = x.reshape(n, dm)
    for l in range(depth):
        hp, hr = _project(x2d, w_plain[l], w_rope_a[l], w_rope_b[l], cos_t, sin_t, seq)
        hp3 = hp.reshape(b, seq, PLAIN_W)
        hr3 = hr.reshape(b, seq, ROPE_W)

        f_logit = hp3[:, :, SMALL_COL + SMALL_F:SMALL_COL + SMALL_F + FOX_HEADS]
        c_rows = _fox_cumsum(f_logit.transpose(0, 2, 1), fox_forget_bias[l].reshape(FOX_HEADS, 1))
        o_fox = _fox_attention(hp3, c_rows.transpose(0, 2, 1), c_rows)

        kvw = NSA_KV_HEADS * HEAD_DIM
        rk = _token_rows(hr3[:, :, ROPE_POS['nsa_kc']:ROPE_POS['nsa_kc'] + kvw], b, seq)
        rv = _token_rows(hp3[:, :, PLAIN_POS['nsa_vc']:PLAIN_POS['nsa_vc'] + kvw], b, seq)
        k_cmp, v_cmp = _nsa_compress(
            rk, rv, cmp_pos_k[l].reshape(1, -1), cmp_pos_v[l].reshape(1, -1),
            cmp_w1_k[l].astype(BF16), cmp_w2_k[l].astype(BF16), cmp_w1_v[l].astype(BF16), cmp_w2_v[l].astype(BF16))
        n_rows = seq // CMP_STRIDE
        o_nsa = _nsa_attention(hr3, hp3, k_cmp.reshape(b, NSA_KV_HEADS, n_rows, HEAD_DIM),
                               v_cmp.reshape(b, NSA_KV_HEADS, n_rows, HEAD_DIM))

        o_dsa = _dsa_attention(hr3, hp3)

        x2d, gates_t = _outproj_ln_route(
            o_fox.reshape(n, fw), o_nsa.reshape(n, nw), o_dsa.reshape(n, nw), x2d,
            w_out_b[l, :fw], w_out_b[l, fw:fw + nw], w_out_b[l, fw + nw:],
            ln1_g[l].reshape(1, dm), ln1_b[l].reshape(1, dm),
            w_router[l].T.astype(BF16), b_router[l].reshape(N_EXPERTS, 1), alpha)
        x2d = _moe_ln(x2d, gates_t.T, l, w_gate, w_up, w_down,
                      ws_gate[l].astype(BF16), ws_up[l].astype(BF16), ws_down[l].astype(BF16),
                      ln2_g[l].reshape(1, dm), ln2_b[l].reshape(1, dm), alpha)
    return x2d.reshape(b, seq, dm)
```
